```python
import jax, jax.numpy as jnp
from jax import lax
import numpy as np

D_MODEL = 1024
BATCH = 4
SEQ = 4096
DEPTH = 4
DEC_BATCH = 32
DEC_SEQ = 1
PAST_LEN = 8192
PAGE_SIZE = 128

D_MIX = D_MODEL
HEAD_DIM = 64
ATT_WIDTH = 3 * D_MIX // 8
ATT_HEADS = ATT_WIDTH // HEAD_DIM
SSD_WIDTH = 3 * D_MIX // 8
SSD_HEAD_DIM = 64
SSD_HEADS = SSD_WIDTH // SSD_HEAD_DIM
SSD_GROUPS = 2
D_STATE = 64
SSD_CONV = 4
XBC_WIDTH = SSD_WIDTH + 2 * SSD_GROUPS * D_STATE
CONV_WIDTH = D_MIX - ATT_WIDTH - SSD_WIDTH
SCONV = 3
D_FF = 4 * D_MODEL
ATT_BLOCK = 128
SSD_CHUNK = 128
RMS_EPS = 1e-6
IN_SIZES = (ATT_WIDTH, ATT_WIDTH, ATT_WIDTH, ATT_HEADS, SSD_WIDTH, XBC_WIDTH, SSD_HEADS, CONV_WIDTH, CONV_WIDTH, CONV_WIDTH)

kernel_name = 'hybrid_fox_ssd_shortconv_decode_step'


def rmsnorm(x, g):
    xf = x.astype(jnp.float32)
    y = xf * lax.rsqrt(jnp.mean(xf * xf, axis=-1, keepdims=True) + RMS_EPS)
    return (y * g.astype(jnp.float32)).astype(x.dtype)


def causal_conv(u, w, prev):
    K = w.shape[0]
    L = u.shape[1]
    up = jnp.concatenate([prev.astype(u.dtype), u], axis=1)
    y = w[0] * up[:, 0:L]
    for j in range(1, K):
        y = y + w[j] * up[:, j:j + L]
    return y, up[:, -(K - 1):]


def fox_attend(q, cq, qpos, k, v, ck, kpos):
    s = jnp.einsum('bqhd,bkhd->bhqk', q, k).astype(jnp.float32) * (HEAD_DIM ** -0.5)
    bias = jnp.transpose(cq, (0, 2, 1))[..., :, None] - jnp.transpose(ck, (0, 2, 1))[..., None, :]
    mask = qpos[:, None] >= kpos[None, :]
    s = jnp.where(mask, s + bias, -jnp.inf)
    p = jax.nn.softmax(s, axis=-1).astype(v.dtype)
    return jnp.einsum('bhqk,bkhd->bqhd', p, v)


def ssd_chunked(x, dt, A, Bh, Ch, S0):
    b, L, H, P = x.shape
    N = Bh.shape[-1]
    nc = L // SSD_CHUNK
    x = x.astype(jnp.float32).reshape(b, nc, SSD_CHUNK, H, P)
    Bh = Bh.astype(jnp.float32).reshape(b, nc, SSD_CHUNK, H, N)
    Ch = Ch.astype(jnp.float32).reshape(b, nc, SSD_CHUNK, H, N)
    dt = dt.reshape(b, nc, SSD_CHUNK, H)
    acum = jnp.cumsum(dt * A, axis=2)
    diff = acum[:, :, :, None, :] - acum[:, :, None, :, :]
    causal = jnp.tril(jnp.ones((SSD_CHUNK, SSD_CHUNK), dtype=bool))[None, None, :, :, None]
    decay = jnp.exp(jnp.where(causal, diff, -jnp.inf))
    cb = jnp.einsum('bcthn,bcshn->bctsh', Ch, Bh)
    y_intra = jnp.einsum('bctsh,bcshp->bcthp', cb * decay * dt[:, :, None], x)
    decay_end = jnp.exp(acum[:, :, -1:, :] - acum)
    chunk_states = jnp.einsum('bcsh,bcshn,bcshp->bchpn', decay_end * dt, Bh, x)
    chunk_decay = jnp.exp(acum[:, :, -1, :])

    def step(S, inp):
        cs, cd = inp
        return cd[:, :, None, None] * S + cs, S

    S_final, S_in = lax.scan(step, S0.astype(jnp.float32),
                             (jnp.moveaxis(chunk_states, 1, 0), jnp.moveaxis(chunk_decay, 1, 0)))
    S_in = jnp.moveaxis(S_in, 0, 1)
    y_inter = jnp.einsum('bcthn,bchpn->bcthp', Ch, S_in) * jnp.exp(acum)[..., None]
    return (y_intra + y_inter).reshape(b, L, H, P), S_final


def ssd_recurrent(x, dt, A, Bh, Ch, S0):
    def step(S, inp):
        xt, dtt, bt, ct = inp
        S = jnp.exp(dtt * A)[..., None, None] * S + (dtt[..., None, None] * xt[..., :, None]) * bt[..., None, :]
        return S, jnp.einsum('bhpn,bhn->bhp', S, ct)

    seq = tuple(jnp.moveaxis(a.astype(jnp.float32), 1, 0) for a in (x, dt, Bh, Ch))
    S, y = lax.scan(step, S0.astype(jnp.float32), seq)
    return jnp.moveaxis(y, 0, 1), S


def mixer(h, mw, past, ssm0, conv0, sconv0, prompt):
    (w_in, b_f, ssd_conv_w, ssd_conv_b, dt_bias, a_log, d_skip, ssd_norm_g, sconv_w, w_out) = mw
    b, L, _ = h.shape
    split_points = [int(i) for i in np.cumsum(IN_SIZES)[:-1]]
    proj = jnp.einsum('bld,de->ble', h, w_in)
    q, k, v, f, z, xbc, dt_raw, gb, gc, hc = jnp.split(proj, split_points, axis=-1)

    q = q.reshape(b, L, ATT_HEADS, HEAD_DIM)
    k = k.reshape(b, L, ATT_HEADS, HEAD_DIM)
    v = v.reshape(b, L, ATT_HEADS, HEAD_DIM)
    logf = jax.nn.log_sigmoid(f.astype(jnp.float32) + b_f.astype(jnp.float32))
    if prompt:
        k_all, v_all, logf_all = k, v, logf
    else:
        k_past, v_past, logf_past = past
        k_all = jnp.concatenate([k_past, k], axis=1)
        v_all = jnp.concatenate([v_past, v], axis=1)
        logf_all = jnp.concatenate([logf_past.astype(jnp.float32), logf], axis=1)
    c_all = jnp.cumsum(logf_all, axis=1)
    Lk = k_all.shape[1]
    kpos = jnp.arange(Lk)
    qpos = (Lk - L) + jnp.arange(L)
    cq = c_all[:, Lk - L:]
    if prompt:
        nb = L // ATT_BLOCK
        qb = jnp.moveaxis(q.reshape(b, nb, ATT_BLOCK, ATT_HEADS, HEAD_DIM), 1, 0)
        cqb = jnp.moveaxis(cq.reshape(b, nb, ATT_BLOCK, ATT_HEADS), 1, 0)
        qpb = qpos.reshape(nb, ATT_BLOCK)
        att = lax.map(lambda a: fox_attend(a[0], a[1], a[2], k_all, v_all, c_all, kpos), (qb, cqb, qpb))
        att = jnp.moveaxis(att, 0, 1)
    else:
        att = fox_attend(q, cq, qpos, k_all, v_all, c_all, kpos)
    att = att.reshape(b, L, ATT_WIDTH).astype(h.dtype)

    xbc_c, conv_new = causal_conv(xbc, ssd_conv_w, conv0)
    xbc_c = jax.nn.silu(xbc_c + ssd_conv_b)
    xs, bm, cm = jnp.split(xbc_c, [SSD_WIDTH, SSD_WIDTH + SSD_GROUPS * D_STATE], axis=-1)
    xs = xs.reshape(b, L, SSD_HEADS, SSD_HEAD_DIM)
    rep = SSD_HEADS // SSD_GROUPS
    bm = jnp.repeat(bm.reshape(b, L, SSD_GROUPS, D_STATE), rep, axis=2)
    cm = jnp.repeat(cm.reshape(b, L, SSD_GROUPS, D_STATE), rep, axis=2)
    dt = jax.nn.softplus(dt_raw.astype(jnp.float32) + dt_bias.astype(jnp.float32))
    A = -jnp.exp(a_log.astype(jnp.float32))
    if prompt:
        y, S = ssd_chunked(xs, dt, A, bm, cm, ssm0)
    else:
        y, S = ssd_recurrent(xs, dt, A, bm, cm, ssm0)
    y = y + d_skip.astype(jnp.float32)[:, None] * xs.astype(jnp.float32)
    y = y.reshape(b, L, SSD_WIDTH).astype(h.dtype)
    y_ssd = rmsnorm(y * jax.nn.silu(z), ssd_norm_g)

    u = gc * hc
    uc, sconv_new = causal_conv(u, sconv_w, sconv0)
    y_conv = gb * uc

    out = jnp.einsum('ble,ed->bld', jnp.concatenate([att, y_ssd, y_conv], axis=-1), w_out)
    new_state = (k, v, logf.astype(h.dtype), S.astype(h.dtype), conv_new, sconv_new)
    return out, new_state


def layer(x, lw, past, ssm0, conv0, sconv0, prompt):
    (g_mix_pre, g_mix_post, g_mlp_pre, g_mlp_post, w_up, w_down, mw) = lw
    m, st = mixer(rmsnorm(x, g_mix_pre), mw, past, ssm0, conv0, sconv0, prompt)
    x = x + rmsnorm(m, g_mix_post)
    hmid = rmsnorm(x, g_mlp_pre)
    ffn = jnp.square(jax.nn.relu(hmid @ w_up)) @ w_down
    x = x + rmsnorm(ffn, g_mlp_post)
    return x, st


def setup_inputs(seed: int = 0) -> dict:
    key = jax.random.key(seed)
    ks = jax.random.split(key, 32)
    f32 = jnp.float32
    n_pages = PAST_LEN // PAGE_SIZE
    n_used = DEC_BATCH * n_pages
    n_pool = n_used + n_used // 4
    d_in = sum(IN_SIZES)

    def nrm(k, shape, s=1.0):
        return s * jax.random.normal(k, shape, f32)

    perm = jax.random.permutation(ks[0], n_pool)
    page_table = perm[:n_used].reshape(DEC_BATCH, n_pages).astype(jnp.int32)
    dt0 = jnp.exp(jax.random.uniform(ks[1], (DEPTH, SSD_HEADS), f32, np.log(1e-3), np.log(1e-1)))
    dt_bias = dt0 + jnp.log(-jnp.expm1(-dt0))
    return {
        'x_prompt': nrm(ks[2], (BATCH, SEQ, D_MODEL)),
        'x_sample': nrm(ks[3], (DEC_BATCH, DEC_SEQ, D_MODEL)),
        'cache_k': nrm(ks[4], (DEPTH, n_pool, PAGE_SIZE, ATT_HEADS, HEAD_DIM)),
        'cache_v': nrm(ks[5], (DEPTH, n_pool, PAGE_SIZE, ATT_HEADS, HEAD_DIM)),
        'cache_logf': jax.nn.log_sigmoid(4.0 + nrm(ks[6], (DEPTH, n_pool, PAGE_SIZE, ATT_HEADS), 0.5)),
        'state_ssm': nrm(ks[7], (DEPTH, DEC_BATCH, SSD_HEADS, SSD_HEAD_DIM, D_STATE), 0.1),
        'state_ssd_conv': nrm(ks[8], (DEPTH, DEC_BATCH, SSD_CONV - 1, XBC_WIDTH)),
        'state_sconv': nrm(ks[9], (DEPTH, DEC_BATCH, SCONV - 1, CONV_WIDTH)),
        'page_table': page_table,
        'w_in': nrm(ks[10], (DEPTH, D_MODEL, d_in), D_MODEL ** -0.5),
        'b_f': 2.0 + 3.0 * jax.random.uniform(ks[11], (DEPTH, ATT_HEADS), f32),
        'ssd_conv_w': nrm(ks[12], (DEPTH, SSD_CONV, XBC_WIDTH), SSD_CONV ** -0.5),
        'ssd_conv_b': nrm(ks[13], (DEPTH, XBC_WIDTH), 0.01),
        'dt_bias': dt_bias,
        'a_log': jnp.log(jax.random.uniform(ks[14], (DEPTH, SSD_HEADS), f32, 1.0, 16.0)),
        'd_skip': 1.0 + nrm(ks[15], (DEPTH, SSD_HEADS), 0.1),
        'ssd_norm_g': 1.0 + nrm(ks[16], (DEPTH, SSD_WIDTH), 0.05),
        'sconv_w': nrm(ks[17], (DEPTH, SCONV, CONV_WIDTH), SCONV ** -0.5),
        'w_out': nrm(ks[18], (DEPTH, D_MIX, D_MODEL), D_MIX ** -0.5),
        'g_mix_pre': 1.0 + nrm(ks[19], (DEPTH, D_MODEL), 0.05),
        'g_mix_post': 1.0 + nrm(ks[20], (DEPTH, D_MODEL), 0.05),
        'g_mlp_pre': 1.0 + nrm(ks[21], (DEPTH, D_MODEL), 0.05),
        'g_mlp_post': 1.0 + nrm(ks[22], (DEPTH, D_MODEL), 0.05),
        'w_mlp_up': nrm(ks[23], (DEPTH, D_MODEL, D_FF), D_MODEL ** -0.5),
        'w_mlp_down': nrm(ks[24], (DEPTH, D_FF, D_MODEL), D_FF ** -0.5),
    }


def reference(x_prompt, x_sample, cache_k, cache_v, cache_logf, state_ssm, state_ssd_conv, state_sconv, page_table,
              w_in, b_f, ssd_conv_w, ssd_conv_b, dt_bias, a_log, d_skip, ssd_norm_g, sconv_w, w_out,
              g_mix_pre, g_mix_post, g_mlp_pre, g_mlp_post, w_mlp_up, w_mlp_down):
    n_pages = page_table.shape[1]
    db = x_sample.shape[0]
    bp = x_prompt.shape[0]
    xp, xs = x_prompt, x_sample
    st_p = [[] for _ in range(6)]
    st_s = [[] for _ in range(6)]
    for l in range(DEPTH):
        mw = (w_in[l], b_f[l], ssd_conv_w[l], ssd_conv_b[l], dt_bias[l], a_log[l], d_skip[l], ssd_norm_g[l], sconv_w[l], w_out[l])
        lw = (g_mix_pre[l], g_mix_post[l], g_mlp_pre[l], g_mlp_post[l], w_mlp_up[l], w_mlp_down[l], mw)
        ssm0 = jnp.zeros((bp, SSD_HEADS, SSD_HEAD_DIM, D_STATE), jnp.float32)
        conv0 = jnp.zeros((bp, SSD_CONV - 1, XBC_WIDTH), xp.dtype)
        sconv0 = jnp.zeros((bp, SCONV - 1, CONV_WIDTH), xp.dtype)
        xp, sp = layer(xp, lw, None, ssm0, conv0, sconv0, True)
        k_past = cache_k[l][page_table].reshape(db, n_pages * PAGE_SIZE, ATT_HEADS, HEAD_DIM)
        v_past = cache_v[l][page_table].reshape(db, n_pages * PAGE_SIZE, ATT_HEADS, HEAD_DIM)
        lf_past = cache_logf[l][page_table].reshape(db, n_pages * PAGE_SIZE, ATT_HEADS)
        xs, ss = layer(xs, lw, (k_past, v_past, lf_past), state_ssm[l], state_ssd_conv[l], state_sconv[l], False)
        for i in range(6):
            st_p[i].append(sp[i])
            st_s[i].append(ss[i])
    k_p, v_p, lf_p, ssm_p, conv_p, sconv_p = [jnp.stack(a) for a in st_p]
    k_s, v_s, lf_s, ssm_s, conv_s, sconv_s = [jnp.stack(a) for a in st_s]
    return (xp, xs, k_p, v_p, lf_p, ssm_p, conv_p, sconv_p, k_s, v_s, lf_s, ssm_s, conv_s, sconv_s)
```

```python
import functools

import jax
import jax.numpy as jnp
from jax import lax
from jax.experimental import pallas as pl
from jax.experimental.pallas import tpu as pltpu

f32 = jnp.float32
bf16 = jnp.bfloat16

D_MODEL = 1024
HEAD_DIM = 64
ATT_HEADS = 6
ATT_WIDTH = ATT_HEADS * HEAD_DIM
SSD_HEADS = 6
SSD_HEAD_DIM = 64
SSD_WIDTH = SSD_HEADS * SSD_HEAD_DIM
SSD_GROUPS = 2
D_STATE = 64
SSD_CONV = 4
XBC_WIDTH = SSD_WIDTH + 2 * SSD_GROUPS * D_STATE
CONV_WIDTH = 256
SCONV = 3
D_FF = 4 * D_MODEL
SSD_CHUNK = 128
PAGE_SIZE = 128
RMS_EPS = 1e-6
IN_SIZES = (ATT_WIDTH, ATT_WIDTH, ATT_WIDTH, ATT_HEADS, SSD_WIDTH, XBC_WIDTH, SSD_HEADS,
            CONV_WIDTH, CONV_WIDTH, CONV_WIDTH)

LANES = 128
SUBLANES = 8
VMEM_LIMIT_BYTES = 56 * 1024 * 1024

OFF_Q = 0
OFF_K = OFF_Q + ATT_WIDTH
OFF_V = OFF_K + ATT_WIDTH
OFF_Z = OFF_V + ATT_WIDTH
OFF_XBC = OFF_Z + SSD_WIDTH
OFF_GB = OFF_XBC + XBC_WIDTH
OFF_GC = OFF_GB + CONV_WIDTH
OFF_HC = OFF_GC + CONV_WIDTH
OFF_SMALL = OFF_HC + CONV_WIDTH
PROJ_WIDTH = OFF_SMALL + LANES
COL_LOGF = 0
COL_DT = ATT_HEADS
PADDED_HEAD = LANES
NEG_BIG = -1e30


def _rms(x, g):
    var = jnp.mean(x * x, axis=-1, keepdims=True)
    return (x * lax.rsqrt(var + RMS_EPS)) * g


def _silu(x):
    return x * (1.0 / (1.0 + jnp.exp(-x)))


def _split3(a):
    a1 = a.astype(bf16)
    r1 = a - a1.astype(f32)
    a2 = r1.astype(bf16)
    a3 = (r1 - a2.astype(f32)).astype(bf16)
    return a1, a2, a3


def _exact_dot(m01, a):
    a1, a2, a3 = _split3(a)
    d = functools.partial(jnp.dot, preferred_element_type=f32)
    return d(m01, a1) + d(m01, a2) + d(m01, a3)


def _dot_nt(a, b):
    return lax.dot_general(a, b, (((1,), (1,)), ((), ())), preferred_element_type=f32)


def _softplus_parts(t):
    sp = jnp.log1p(jnp.exp(-jnp.abs(t)))
    return jnp.minimum(t, 0.0) - sp, jnp.maximum(t, 0.0) + sp


def _shift_rows(x, k, tail):
    row = lax.broadcasted_iota(jnp.int32, x.shape, 0)
    y = pltpu.roll(x, k, 0)
    for r in range(k):
        y = jnp.where(row == r, tail[SUBLANES - k + r:SUBLANES - k + r + 1, :], y)
    return y


def _prompt_in_body(x_ref, g_ref, w_ref, bias_ref, tri_ref, scw_ref,
                    qp_ref, kp_ref, vp_ref, k32_ref, v32_ref, z_ref, xbc_ref, yconv_ref,
                    lfdt_ref, ccol_ref, crow_ref, utail_ref,
                    carry_ref, tail_ref, *, tk):
    tm = x_ref.shape[1]

    @pl.when(pl.program_id(1) == 0)
    def _():
        carry_ref[...] = jnp.zeros_like(carry_ref)
        tail_ref[...] = jnp.zeros_like(tail_ref)

    h = _rms(x_ref[0], g_ref[...]).astype(bf16)
    dot = functools.partial(jnp.dot, preferred_element_type=f32)

    qkv = dot(h, w_ref[:, OFF_Q:OFF_Z])
    q = qkv[:, 0:ATT_WIDTH] * (HEAD_DIM ** -0.5)
    k = qkv[:, ATT_WIDTH:2 * ATT_WIDTH]
    v = qkv[:, 2 * ATT_WIDTH:3 * ATT_WIDTH]
    k32_ref[0] = k
    v32_ref[0] = v
    zpad = jnp.zeros((tm, PADDED_HEAD - HEAD_DIM), bf16)
    for hh in range(ATT_HEADS):
        src = slice(HEAD_DIM * hh, HEAD_DIM * (hh + 1))
        lo = slice(PADDED_HEAD * hh, PADDED_HEAD * hh + HEAD_DIM)
        hi = slice(PADDED_HEAD * hh + HEAD_DIM, PADDED_HEAD * (hh + 1))
        qp_ref[0, :, lo] = q[:, src].astype(bf16)
        qp_ref[0, :, hi] = zpad
        kp_ref[0, :, lo] = k[:, src].astype(bf16)
        kp_ref[0, :, hi] = zpad
        vp_ref[0, :, lo] = v[:, src].astype(bf16)
        vp_ref[0, :, hi] = zpad

    z_ref[0] = dot(h, w_ref[:, OFF_Z:OFF_XBC])
    xbc_ref[0] = dot(h, w_ref[:, OFF_XBC:OFF_GB])

    gates = dot(h, w_ref[:, OFF_GB:OFF_SMALL])
    gb = gates[:, 0:CONV_WIDTH]
    u = gates[:, CONV_WIDTH:2 * CONV_WIDTH] * gates[:, 2 * CONV_WIDTH:3 * CONV_WIDTH]
    tail = tail_ref[...]
    uc = (scw_ref[0:1, :] * _shift_rows(u, 2, tail) + scw_ref[1:2, :] * _shift_rows(u, 1, tail)
          + scw_ref[2:3, :] * u)
    yconv_ref[0] = (gb * uc).astype(bf16)
    tail_ref[...] = u[tm - SUBLANES:tm, :]
    utail_ref[0] = u[tm - SUBLANES:tm, :]

    t = dot(h, w_ref[:, OFF_SMALL:PROJ_WIDTH]) + bias_ref[...]
    logf, dt = _softplus_parts(t)
    col = lax.broadcasted_iota(jnp.int32, t.shape, 1)
    lfdt_ref[0] = jnp.where(col < COL_DT, logf, dt)
    c = _exact_dot(tri_ref[...], logf) + carry_ref[...]
    carry_ref[...] = c[tm - 1:tm, :]
    ccol_ref[0] = c
    ct = c.T
    for i in range(tm // tk):
        crow_ref[0, i] = ct[0:SUBLANES, i * tk:(i + 1) * tk]


def _prompt_in(x, g, w, bias, scw, *, tm, tk):
    B, L, _ = x.shape
    nj = L // tm
    tri = jnp.tril(jnp.ones((tm, tm), f32)).astype(bf16)
    row = lambda width: pl.BlockSpec((1, tm, width), lambda b, j: (b, j, 0))
    const = lambda shape: pl.BlockSpec(shape, lambda b, j: (0,) * len(shape))
    out_shapes = (
        jax.ShapeDtypeStruct((B, L, ATT_HEADS * PADDED_HEAD), bf16),
        jax.ShapeDtypeStruct((B, L, ATT_HEADS * PADDED_HEAD), bf16),
        jax.ShapeDtypeStruct((B, L, ATT_HEADS * PADDED_HEAD), bf16),
        jax.ShapeDtypeStruct((B, L, ATT_WIDTH), f32),
        jax.ShapeDtypeStruct((B, L, ATT_WIDTH), f32),
        jax.ShapeDtypeStruct((B, L, SSD_WIDTH), f32),
        jax.ShapeDtypeStruct((B, L, XBC_WIDTH), f32),
        jax.ShapeDtypeStruct((B, L, CONV_WIDTH), bf16),
        jax.ShapeDtypeStruct((B, L, LANES), f32),
        jax.ShapeDtypeStruct((B, L, LANES), f32),
        jax.ShapeDtypeStruct((B, L // tk, SUBLANES, tk), f32),
        jax.ShapeDtypeStruct((B, SUBLANES, CONV_WIDTH), f32),
    )
    out_specs = (
        row(ATT_HEADS * PADDED_HEAD), row(ATT_HEADS * PADDED_HEAD), row(ATT_HEADS * PADDED_HEAD),
        row(ATT_WIDTH), row(ATT_WIDTH), row(SSD_WIDTH), row(XBC_WIDTH), row(CONV_WIDTH),
        row(LANES), row(LANES),
        pl.BlockSpec((1, tm // tk, SUBLANES, tk), lambda b, j: (b, j, 0, 0)),
        pl.BlockSpec((1, SUBLANES, CONV_WIDTH), lambda b, j: (b, 0, 0)),
    )
    return pl.pallas_call(
        functools.partial(_prompt_in_body, tk=tk),
        out_shape=out_shapes,
        grid=(B, nj),
        in_specs=[row(D_MODEL), const((1, D_MODEL)), const((D_MODEL, PROJ_WIDTH)), const((1, LANES)),
                  const((tm, tm)), const((SCONV, CONV_WIDTH))],
        out_specs=out_specs,
        scratch_shapes=[pltpu.VMEM((1, LANES), f32), pltpu.VMEM((SUBLANES, CONV_WIDTH), f32)],
        compiler_params=pltpu.CompilerParams(dimension_semantics=("arbitrary", "arbitrary"),
                                             vmem_limit_bytes=VMEM_LIMIT_BYTES),
        name="prompt_in",
    )(x, g, w, bias, tri, scw)


def _prompt_attn_body(qp_ref, kp_ref, vp_ref, ccol_ref, crow_ref, o_ref, *, t):
    qi = pl.program_id(1)
    row = lax.broadcasted_iota(jnp.int32, (t, t), 0)
    colm = lax.broadcasted_iota(jnp.int32, (t, t), 1)
    causal = row >= colm

    for hh in range(ATT_HEADS):
        hs = slice(PADDED_HEAD * hh, PADDED_HEAD * (hh + 1))
        q = qp_ref[0, :, hs]
        cq = ccol_ref[0, :, hh:hh + 1]

        def block(kj, carry, masked):
            m, l, acc = carry
            ks = pl.multiple_of(kj * t, t)
            kb = kp_ref[0, pl.ds(ks, t), hs]
            vb = vp_ref[0, pl.ds(ks, t), hs]
            ck = crow_ref[0, kj, hh:hh + 1, :]
            s = _dot_nt(q, kb) + (cq - ck)
            if masked:
                s = jnp.where(causal, s, NEG_BIG)
            m_new = jnp.maximum(m, jnp.max(s, axis=-1, keepdims=True))
            p = jnp.exp(s - m_new)
            alpha = jnp.exp(m - m_new)
            l = alpha * l + jnp.sum(p, axis=-1, keepdims=True)
            acc = alpha * acc + jnp.dot(p.astype(bf16), vb, preferred_element_type=f32)
            return m_new, l, acc

        init = (jnp.full((t, 1), NEG_BIG, f32), jnp.zeros((t, 1), f32), jnp.zeros((t, PADDED_HEAD), f32))
        carry = lax.fori_loop(0, qi, functools.partial(block, masked=False), init)
        _, l, acc = block(qi, carry, True)
        o_ref[0, :, HEAD_DIM * hh:HEAD_DIM * (hh + 1)] = (acc[:, 0:HEAD_DIM] / l).astype(bf16)


def _prompt_attn(qp, kp, vp, ccol, crow, *, t):
    B, L, W = qp.shape
    nq = L // t
    return pl.pallas_call(
        functools.partial(_prompt_attn_body, t=t),
        out_shape=jax.ShapeDtypeStruct((B, L, ATT_WIDTH), bf16),
        grid=(B, nq),
        in_specs=[pl.BlockSpec((1, t, W), lambda b, i: (b, i, 0)),
                  pl.BlockSpec((1, L, W), lambda b, i: (b, 0, 0)),
                  pl.BlockSpec((1, L, W), lambda b, i: (b, 0, 0)),
                  pl.BlockSpec((1, t, LANES), lambda b, i: (b, i, 0)),
                  pl.BlockSpec((1, nq, SUBLANES, t), lambda b, i: (b, 0, 0, 0))],
        out_specs=pl.BlockSpec((1, t, ATT_WIDTH), lambda b, i: (b, i, 0)),
        compiler_params=pltpu.CompilerParams(dimension_semantics=("arbitrary", "arbitrary"),
                                             vmem_limit_bytes=VMEM_LIMIT_BYTES),
        name="prompt_attn",
    )(qp, kp, vp, ccol, crow)


def _prompt_ssd_body(xbc_ref, lfdt_ref, z_ref, cw_ref, cb_ref, a_ref, dsk_ref, ng_ref, tri_ref,
                     y_ref, sout_ref, s_ref, tail_ref, ybuf_ref):
    tm = xbc_ref.shape[1]
    Q = SSD_CHUNK

    @pl.when(pl.program_id(1) == 0)
    def _():
        s_ref[...] = jnp.zeros_like(s_ref)
        tail_ref[...] = jnp.zeros_like(tail_ref)

    x = xbc_ref[0]
    tail = tail_ref[...]
    xc = (cw_ref[0:1, :] * _shift_rows(x, 3, tail) + cw_ref[1:2, :] * _shift_rows(x, 2, tail)
          + cw_ref[2:3, :] * _shift_rows(x, 1, tail) + cw_ref[3:4, :] * x + cb_ref[...])
    xc = _silu(xc)
    tail_ref[...] = x[tm - SUBLANES:tm, :]

    dt_all = lfdt_ref[0]
    dta_all = dt_all * a_ref[...]
    row = lax.broadcasted_iota(jnp.int32, (Q, Q), 0)
    colm = lax.broadcasted_iota(jnp.int32, (Q, Q), 1)
    causal = row >= colm
    tri = tri_ref[...]

    for c in range(tm // Q):
        rs = slice(c * Q, (c + 1) * Q)
        dt = dt_all[rs]
        acum = _exact_dot(tri, dta_all[rs])
        acum_t = acum.T
        dt_t = dt.T
        xs = xc[rs, 0:SSD_WIDTH]
        xs_t = xs.T.astype(bf16)
        xs_b = xs.astype(bf16)
        bm = xc[rs, SSD_WIDTH:SSD_WIDTH + SSD_GROUPS * D_STATE]
        cm = xc[rs, SSD_WIDTH + SSD_GROUPS * D_STATE:XBC_WIDTH].astype(bf16)
        cb = []
        for g in range(SSD_GROUPS):
            gs = slice(g * D_STATE, (g + 1) * D_STATE)
            cb.append(_dot_nt(cm[:, gs], bm[:, gs].astype(bf16)))
        for hh in range(SSD_HEADS):
            g = hh // (SSD_HEADS // SSD_GROUPS)
            gs = slice(g * D_STATE, (g + 1) * D_STATE)
            hs = slice(hh * SSD_HEAD_DIM, (hh + 1) * SSD_HEAD_DIM)
            cc = COL_DT + hh
            a_col = acum[:, cc:cc + 1]
            a_row = acum_t[cc:cc + 1, :]
            a_last = acum[Q - 1:Q, cc:cc + 1]
            decay = jnp.exp(jnp.where(causal, a_col - a_row, -jnp.inf))
            mat = cb[g] * decay * dt_t[cc:cc + 1, :]
            y_intra = jnp.dot(mat.astype(bf16), xs_b[:, hs], preferred_element_type=f32)
            s_in = s_ref[hh]
            y_inter = _dot_nt(cm[:, gs], s_in.astype(bf16)) * jnp.exp(a_col)
            wcol = jnp.exp(a_last - a_col) * dt[:, cc:cc + 1]
            bw = (bm[:, gs] * wcol).astype(bf16)
            s_ref[hh] = jnp.exp(a_last) * s_in + jnp.dot(xs_t[hs, :], bw, preferred_element_type=f32)
            ybuf_ref[:, hs] = y_intra + y_inter + dsk_ref[:, hs] * xs[:, hs]
        gated = ybuf_ref[...] * _silu(z_ref[0, rs, :])
        y_ref[0, rs, :] = _rms(gated, ng_ref[...]).astype(bf16)
    sout_ref[0] = s_ref[...]


def _prompt_ssd(xbc, lfdt, z, cw, cb, a_row, dsk, ng, *, tm):
    B, L, _ = xbc.shape
    tri = jnp.tril(jnp.ones((SSD_CHUNK, SSD_CHUNK), f32)).astype(bf16)
    row = lambda width: pl.BlockSpec((1, tm, width), lambda b, j: (b, j, 0))
    const = lambda shape: pl.BlockSpec(shape, lambda b, j: (0,) * len(shape))
    return pl.pallas_call(
        _prompt_ssd_body,
        out_shape=(jax.ShapeDtypeStruct((B, L, SSD_WIDTH), bf16),
                   jax.ShapeDtypeStruct((B, SSD_HEADS, SSD_HEAD_DIM, D_STATE), f32)),
        grid=(B, L // tm),
        in_specs=[row(XBC_WIDTH), row(LANES), row(SSD_WIDTH), const((SSD_CONV, XBC_WIDTH)),
                  const((1, XBC_WIDTH)), const((1, LANES)), const((1, SSD_WIDTH)), const((1, SSD_WIDTH)),
                  const((SSD_CHUNK, SSD_CHUNK))],
        out_specs=(row(SSD_WIDTH),
                   pl.BlockSpec((1, SSD_HEADS, SSD_HEAD_DIM, D_STATE), lambda b, j: (b, 0, 0, 0))),
        scratch_shapes=[pltpu.VMEM((SSD_HEADS, SSD_HEAD_DIM, D_STATE), f32),
                        pltpu.VMEM((SUBLANES, XBC_WIDTH), f32),
                        pltpu.VMEM((SSD_CHUNK, SSD_WIDTH), f32)],
        compiler_params=pltpu.CompilerParams(dimension_semantics=("arbitrary", "arbitrary"),
                                             vmem_limit_bytes=VMEM_LIMIT_BYTES),
        name="prompt_ssd",
    )(xbc, lfdt, z, cw, cb, a_row, dsk, ng, tri)


def _mlp_tail(x, mix, gpost, gpre, gmpost, wup_ref, wdn_ref, ff_chunk):
    x1 = x + _rms(mix, gpost)
    hmid = _rms(x1, gpre).astype(bf16)
    acc = jnp.zeros_like(x1)
    for c in range(D_FF // ff_chunk):
        cs = slice(c * ff_chunk, (c + 1) * ff_chunk)
        up = jnp.dot(hmid, wup_ref[:, cs], preferred_element_type=f32)
        act = jnp.square(jnp.maximum(up, 0.0)).astype(bf16)
        acc = acc + jnp.dot(act, wdn_ref[cs, :], preferred_element_type=f32)
    return x1 + _rms(acc, gmpost)


def _prompt_out_body(x_ref, att_ref, yssd_ref, yconv_ref, wo_ref, gpost_ref, gpre_ref, gmpost_ref,
                     wup_ref, wdn_ref, o_ref, *, ff_chunk):
    dot = functools.partial(jnp.dot, preferred_element_type=f32)
    mix = (dot(att_ref[...], wo_ref[0:ATT_WIDTH, :])
           + dot(yssd_ref[...], wo_ref[ATT_WIDTH:ATT_WIDTH + SSD_WIDTH, :])
           + dot(yconv_ref[...], wo_ref[ATT_WIDTH + SSD_WIDTH:D_MODEL, :]))
    o_ref[...] = _mlp_tail(x_ref[...], mix, gpost_ref[...], gpre_ref[...], gmpost_ref[...],
                           wup_ref, wdn_ref, ff_chunk)


def _prompt_out(x, att, yssd, yconv, wo, gpost, gpre, gmpost, wup, wdn, *, tm):
    M = x.shape[0]
    row = lambda width: pl.BlockSpec((tm, width), lambda i: (i, 0))
    const = lambda shape: pl.BlockSpec(shape, lambda i: (0, 0), pipeline_mode=pl.Buffered(1))
    return pl.pallas_call(
        functools.partial(_prompt_out_body, ff_chunk=512),
        out_shape=jax.ShapeDtypeStruct((M, D_MODEL), f32),
        grid=(M // tm,),
        in_specs=[row(D_MODEL), row(ATT_WIDTH), row(SSD_WIDTH), row(CONV_WIDTH),
                  const((D_MODEL, D_MODEL)), const((1, D_MODEL)), const((1, D_MODEL)), const((1, D_MODEL)),
                  const((D_MODEL, D_FF)), const((D_FF, D_MODEL))],
        out_specs=row(D_MODEL),
        compiler_params=pltpu.CompilerParams(dimension_semantics=("arbitrary",),
                                             vmem_limit_bytes=VMEM_LIMIT_BYTES),
        name="prompt_out_mlp",
    )(x, att, yssd, yconv, wo, gpost, gpre, gmpost, wup, wdn)


def _sample_in_body(x_ref, g_ref, w_ref, bias_ref, cst_ref, cw_ref, cb_ref, aexp_ref,
                    q_ref, k_ref, v_ref, lf_ref, z_ref, gates_ref, xs_ref, bm_ref, cm_ref,
                    dtxt_ref, dect_ref, cnew_ref):
    n = x_ref.shape[0]
    h = _rms(x_ref[...], g_ref[...]).astype(bf16)
    proj = jnp.dot(h, w_ref[...], preferred_element_type=f32)
    q_ref[...] = proj[:, OFF_Q:OFF_K] * (HEAD_DIM ** -0.5)
    k_ref[...] = proj[:, OFF_K:OFF_V]
    v_ref[...] = proj[:, OFF_V:OFF_Z]
    z_ref[...] = proj[:, OFF_Z:OFF_XBC]
    gates_ref[...] = proj[:, OFF_GB:OFF_SMALL]
    logf, dt = _softplus_parts(proj[:, OFF_SMALL:PROJ_WIDTH] + bias_ref[...])
    lf_ref[...] = logf

    xbc = proj[:, OFF_XBC:OFF_GB]
    xc = (cw_ref[0:1, :] * cst_ref[0] + cw_ref[1:2, :] * cst_ref[1] + cw_ref[2:3, :] * cst_ref[2]
          + cw_ref[3:4, :] * xbc + cb_ref[...])
    xc = _silu(xc)
    cnew_ref[0] = cst_ref[1]
    cnew_ref[1] = cst_ref[2]
    cnew_ref[2] = xbc
    xs = xc[:, 0:SSD_WIDTH]
    xs_ref[...] = xs
    bm_ref[...] = xc[:, SSD_WIDTH:SSD_WIDTH + SSD_GROUPS * D_STATE]
    cm_ref[...] = xc[:, SSD_WIDTH + SSD_GROUPS * D_STATE:XBC_WIDTH]

    head = lax.broadcasted_iota(jnp.int32, (n, SSD_WIDTH), 1) // SSD_HEAD_DIM
    dt_exp = jnp.zeros((n, SSD_WIDTH), f32)
    for hh in range(SSD_HEADS):
        dt_exp = jnp.where(head == hh, dt[:, COL_DT + hh:COL_DT + hh + 1], dt_exp)
    pad = jnp.zeros((LANES - n, SSD_WIDTH), f32)
    dtxt_ref[...] = jnp.concatenate([dt_exp * xs, pad], axis=0).T
    dect_ref[...] = jnp.concatenate([jnp.exp(dt_exp * aexp_ref[...]), pad], axis=0).T


def _sample_in(x, g, w, bias, cst, cw, cb, aexp):
    n = x.shape[0]
    s = lambda *shape: jax.ShapeDtypeStruct(shape, f32)
    return pl.pallas_call(
        _sample_in_body,
        out_shape=(s(n, ATT_WIDTH), s(n, ATT_WIDTH), s(n, ATT_WIDTH), s(n, LANES), s(n, SSD_WIDTH),
                   s(n, 3 * CONV_WIDTH), s(n, SSD_WIDTH), s(n, SSD_GROUPS * D_STATE), s(n, SSD_GROUPS * D_STATE),
                   s(SSD_WIDTH, LANES), s(SSD_WIDTH, LANES), s(SSD_CONV - 1, n, XBC_WIDTH)),
        compiler_params=pltpu.CompilerParams(vmem_limit_bytes=VMEM_LIMIT_BYTES),
        name="sample_in",
    )(x, g, w, bias, cst, cw, cb, aexp)


def _sample_ssd_body(s_ref, dtxt_ref, dect_ref, bm_ref, cm_ref, snew_ref, yt_ref):
    b = pl.program_id(0)
    rows = SSD_WIDTH
    half = rows // SSD_GROUPS
    lane = lax.broadcasted_iota(jnp.int32, (rows, LANES), 1)
    sel = lane == b
    dcol = jnp.sum(jnp.where(sel, dect_ref[...], 0.0), axis=1, keepdims=True)
    xcol = jnp.sum(jnp.where(sel, dtxt_ref[...], 0.0), axis=1, keepdims=True)

    def expand(ref):
        r = ref[pl.ds(b, 1), :]
        return jnp.concatenate([jnp.broadcast_to(r[:, g * D_STATE:(g + 1) * D_STATE], (half, D_STATE))
                                for g in range(SSD_GROUPS)], axis=0)

    s_old = s_ref[0].reshape(rows, D_STATE)
    s_new = dcol * s_old + xcol * expand(bm_ref)
    snew_ref[0] = s_new.reshape(SSD_HEADS, SSD_HEAD_DIM, D_STATE)
    ycol = jnp.sum(s_new * expand(cm_ref), axis=1, keepdims=True)

    @pl.when(b == 0)
    def _():
        yt_ref[...] = jnp.zeros_like(yt_ref)

    yt_ref[...] += jnp.where(sel, ycol, 0.0)


def _sample_ssd(state, dtxt, dect, bm, cm):
    n = state.shape[0]
    const = lambda shape: pl.BlockSpec(shape, lambda b: (0, 0))
    sblk = pl.BlockSpec((1, SSD_HEADS, SSD_HEAD_DIM, D_STATE), lambda b: (b, 0, 0, 0))
    return pl.pallas_call(
        _sample_ssd_body,
        out_shape=(jax.ShapeDtypeStruct(state.shape, f32), jax.ShapeDtypeStruct((SSD_WIDTH, LANES), f32)),
        grid=(n,),
        in_specs=[sblk, const((SSD_WIDTH, LANES)), const((SSD_WIDTH, LANES)),
                  const(bm.shape), const(cm.shape)],
        out_specs=(sblk, const((SSD_WIDTH, LANES))),
        compiler_params=pltpu.CompilerParams(dimension_semantics=("arbitrary",)),
        name="sample_ssd",
    )(state, dtxt, dect, bm, cm)


def _sample_attn_body(pt_ref, q_ref, kn_ref, vn_ref, lfn_ref, k_ref, v_ref, lf_ref, eye_ref, upper_ref,
                      o_ref, m_ref, l_ref, acc_ref, r_ref):
    j = pl.program_id(1)
    nj = pl.num_programs(1)
    q = q_ref[0]
    qb = q.astype(bf16)
    hrow = lax.broadcasted_iota(jnp.int32, (SUBLANES, LANES), 0)

    @pl.when(j == 0)
    def _():
        s0 = jnp.sum(q * kn_ref[0], axis=1, keepdims=True)
        m_ref[...] = jnp.broadcast_to(s0, m_ref.shape)
        l_ref[...] = jnp.ones_like(l_ref)
        acc_ref[...] = vn_ref[0]
        r_ref[...] = lfn_ref[0]

    lf = lf_ref[...]
    l1, l2, l3 = _split3(lf)
    eye = eye_ref[...]
    lf_t = _dot_nt(eye, l1) + _dot_nt(eye, l2) + _dot_nt(eye, l3)
    suffix = _exact_dot_rhs(lf_t, upper_ref[...])
    bias = r_ref[...] + suffix
    r_ref[...] = r_ref[...] + jnp.sum(lf_t, axis=1, keepdims=True)

    s = jnp.zeros((SUBLANES, PAGE_SIZE), f32)
    for hh in range(ATT_HEADS):
        kh = k_ref[:, hh, :].astype(bf16)
        s = jnp.where(hrow == hh, _dot_nt(qb, kh), s)
    s = s + bias
    m_old = m_ref[...]
    m_new = jnp.maximum(m_old, jnp.max(s, axis=1, keepdims=True))
    p = jnp.exp(s - m_new)
    alpha = jnp.exp(m_old - m_new)
    l_ref[...] = alpha * l_ref[...] + jnp.sum(p, axis=1, keepdims=True)
    m_ref[...] = m_new
    pb = p.astype(bf16)
    pv = jnp.zeros((SUBLANES, HEAD_DIM), f32)
    hrow_v = lax.broadcasted_iota(jnp.int32, (SUBLANES, HEAD_DIM), 0)
    for hh in range(ATT_HEADS):
        vh = v_ref[:, hh, :].astype(bf16)
        pv = jnp.where(hrow_v == hh, jnp.dot(pb, vh, preferred_element_type=f32), pv)
    acc_ref[...] = alpha[:, 0:HEAD_DIM] * acc_ref[...] + pv

    @pl.when(j == nj - 1)
    def _():
        o_ref[0] = acc_ref[...] / l_ref[:, 0:HEAD_DIM]


def _exact_dot_rhs(a, m01):
    a1, a2, a3 = _split3(a)
    d = functools.partial(jnp.dot, preferred_element_type=f32)
    return d(a1, m01) + d(a2, m01) + d(a3, m01)


def _sample_attn(page_table, q8, kn8, vn8, lfn8, cache_k, cache_v, cache_logf, layer):
    n, n_pages = page_table.shape
    pt = page_table.reshape(-1)
    eye = jnp.eye(SUBLANES, ATT_HEADS, dtype=bf16)
    pos = jnp.arange(PAGE_SIZE)
    upper = (pos[:, None] > pos[None, :]).astype(bf16)
    seq = lambda shape: pl.BlockSpec((1,) + shape, lambda b, j, pt: (b, 0, 0))
    page = lambda b, j, pt: pt[b * n_pages + (n_pages - 1 - j)]
    kv_spec = pl.BlockSpec((None, None, PAGE_SIZE, ATT_HEADS, HEAD_DIM),
                           lambda b, j, pt: (layer, page(b, j, pt), 0, 0, 0))
    lf_spec = pl.BlockSpec((None, None, PAGE_SIZE, ATT_HEADS), lambda b, j, pt: (layer, page(b, j, pt), 0, 0))
    const = lambda shape: pl.BlockSpec(shape, lambda b, j, pt: (0, 0))
    return pl.pallas_call(
        _sample_attn_body,
        out_shape=jax.ShapeDtypeStruct((n, SUBLANES, HEAD_DIM), f32),
        grid_spec=pltpu.PrefetchScalarGridSpec(
            num_scalar_prefetch=1,
            grid=(n, n_pages),
            in_specs=[seq((SUBLANES, HEAD_DIM)), seq((SUBLANES, HEAD_DIM)), seq((SUBLANES, HEAD_DIM)),
                      seq((SUBLANES, LANES)), kv_spec, kv_spec, lf_spec,
                      const((SUBLANES, ATT_HEADS)), const((PAGE_SIZE, PAGE_SIZE))],
            out_specs=seq((SUBLANES, HEAD_DIM)),
            scratch_shapes=[pltpu.VMEM((SUBLANES, LANES), f32), pltpu.VMEM((SUBLANES, LANES), f32),
                            pltpu.VMEM((SUBLANES, HEAD_DIM), f32), pltpu.VMEM((SUBLANES, LANES), f32)],
        ),
        compiler_params=pltpu.CompilerParams(dimension_semantics=("arbitrary", "arbitrary")),
        name="sample_attn",
    )(pt, q8, kn8, vn8, lfn8, cache_k, cache_v, cache_logf, eye, upper)


def _sample_out_body(x_ref, att_ref, yt_ref, xs_ref, z_ref, gates_ref, sst_ref, scw_ref, dsk_ref, ng_ref,
                     wo_ref, gpost_ref, gpre_ref, gmpost_ref, wup_ref, wdn_ref, o_ref, snew_ref, *, ff_chunk):
    n = x_ref.shape[0]
    y = yt_ref[...].T[0:n, :] + dsk_ref[...] * xs_ref[...]
    yssd = _rms(y * _silu(z_ref[...]), ng_ref[...])
    gates = gates_ref[...]
    u = gates[:, CONV_WIDTH:2 * CONV_WIDTH] * gates[:, 2 * CONV_WIDTH:3 * CONV_WIDTH]
    uc = scw_ref[0:1, :] * sst_ref[0] + scw_ref[1:2, :] * sst_ref[1] + scw_ref[2:3, :] * u
    yconv = gates[:, 0:CONV_WIDTH] * uc
    snew_ref[0] = sst_ref[1]
    snew_ref[1] = u
    dot = functools.partial(jnp.dot, preferred_element_type=f32)
    mix = (dot(att_ref[...].astype(bf16), wo_ref[0:ATT_WIDTH, :])
           + dot(yssd.astype(bf16), wo_ref[ATT_WIDTH:ATT_WIDTH + SSD_WIDTH, :])
           + dot(yconv.astype(bf16), wo_ref[ATT_WIDTH + SSD_WIDTH:D_MODEL, :]))
    o_ref[...] = _mlp_tail(x_ref[...], mix, gpost_ref[...], gpre_ref[...], gmpost_ref[...],
                           wup_ref, wdn_ref, ff_chunk)


def _sample_out(x, att, yt, xs, z, gates, sst, scw, dsk, ng, wo, gpost, gpre, gmpost, wup, wdn):
    n = x.shape[0]
    return pl.pallas_call(
        functools.partial(_sample_out_body, ff_chunk=512),
        out_shape=(jax.ShapeDtypeStruct((n, D_MODEL), f32),
                   jax.ShapeDtypeStruct((SCONV - 1, n, CONV_WIDTH), f32)),
        compiler_params=pltpu.CompilerParams(vmem_limit_bytes=VMEM_LIMIT_BYTES),
        name="sample_out_mlp",
    )(x, att, yt, xs, z, gates, sst, scw, dsk, ng, wo, gpost, gpre, gmpost, wup, wdn)


def _prep_w_in(w):
    offs = [0]
    for s in IN_SIZES:
        offs.append(offs[-1] + s)
    q, k, v, f, z, xbc, dt, gb, gc, hc = [w[:, offs[i]:offs[i + 1]] for i in range(len(IN_SIZES))]
    small = jnp.concatenate([f, dt, jnp.zeros((w.shape[0], LANES - ATT_HEADS - SSD_HEADS), w.dtype)], axis=1)
    return jnp.concatenate([q, k, v, z, xbc, gb, gc, hc, small], axis=1).astype(bf16)


def _pick_tile(n, pref):
    t = min(n, pref)
    assert n % t == 0
    return t


def kernel(x_prompt, x_sample, cache_k, cache_v, cache_logf, state_ssm, state_ssd_conv, state_sconv, page_table,
           w_in, b_f, ssd_conv_w, ssd_conv_b, dt_bias, a_log, d_skip, ssd_norm_g, sconv_w, w_out,
           g_mix_pre, g_mix_post, g_mlp_pre, g_mlp_post, w_mlp_up, w_mlp_down):
    depth = w_in.shape[0]
    B, L, _ = x_prompt.shape
    n = x_sample.shape[0]
    assert x_sample.shape[1] == 1 and n <= LANES
    tm = _pick_tile(L, 512)
    t_att = _pick_tile(L, 256)

    xp = x_prompt
    xs = x_sample.reshape(n, D_MODEL)
    outs_p = [[] for _ in range(6)]
    outs_s = [[] for _ in range(6)]
    row = lambda a: a.reshape(1, -1)
    pad_heads = lambda a: jnp.pad(a.reshape(n, ATT_HEADS, -1), ((0, 0), (0, SUBLANES - ATT_HEADS), (0, 0)))

    for l in range(depth):
        w = _prep_w_in(w_in[l])
        wo = w_out[l].astype(bf16)
        wup = w_mlp_up[l].astype(bf16)
        wdn = w_mlp_down[l].astype(bf16)
        bias = jnp.concatenate([b_f[l], dt_bias[l], jnp.zeros((LANES - ATT_HEADS - SSD_HEADS,), f32)]).reshape(1, LANES)
        a_neg = -jnp.exp(a_log[l])
        a_row = jnp.concatenate([jnp.zeros((COL_DT,), f32), a_neg,
                                 jnp.zeros((LANES - COL_DT - SSD_HEADS,), f32)]).reshape(1, LANES)
        a_exp = jnp.repeat(a_neg, SSD_HEAD_DIM).reshape(1, SSD_WIDTH)
        dsk = jnp.repeat(d_skip[l], SSD_HEAD_DIM).reshape(1, SSD_WIDTH)
        ng = row(ssd_norm_g[l])
        cw, cb, scw = ssd_conv_w[l], row(ssd_conv_b[l]), sconv_w[l]
        gpre, gpost, gmpre, gmpost = row(g_mix_pre[l]), row(g_mix_post[l]), row(g_mlp_pre[l]), row(g_mlp_post[l])

        (qp, kp, vp, k32, v32, z, xbc, yconv, lfdt, ccol, crow, utail) = _prompt_in(
            xp, gpre, w, bias, scw, tm=tm, tk=t_att)
        att = _prompt_attn(qp, kp, vp, ccol, crow, t=t_att)
        yssd, s_fin = _prompt_ssd(xbc, lfdt, z, cw, cb, a_row, dsk, ng, tm=tm)
        xp = _prompt_out(xp.reshape(B * L, D_MODEL), att.reshape(B * L, ATT_WIDTH), yssd.reshape(B * L, SSD_WIDTH),
                         yconv.reshape(B * L, CONV_WIDTH), wo, gpost, gmpre, gmpost, wup, wdn,
                         tm=tm).reshape(B, L, D_MODEL)
        outs_p[0].append(k32.reshape(B, L, ATT_HEADS, HEAD_DIM))
        outs_p[1].append(v32.reshape(B, L, ATT_HEADS, HEAD_DIM))
        outs_p[2].append(lfdt[:, :, COL_LOGF:COL_LOGF + ATT_HEADS])
        outs_p[3].append(s_fin)
        outs_p[4].append(xbc[:, L - (SSD_CONV - 1):, :])
        outs_p[5].append(utail[:, SUBLANES - (SCONV - 1):, :])

        cst = jnp.transpose(state_ssd_conv[l], (1, 0, 2))
        sst = jnp.transpose(state_sconv[l], (1, 0, 2))
        (q, k, v, lf, z_s, gates, xs_s, bm, cm, dtxt, dect, cnew) = _sample_in(xs, gpre, w, bias, cst, cw, cb, a_exp)
        s_new, yt = _sample_ssd(state_ssm[l], dtxt, dect, bm, cm)
        lfn8 = jnp.broadcast_to(pad_heads(lf[:, COL_LOGF:COL_LOGF + ATT_HEADS]), (n, SUBLANES, LANES))
        att_s = _sample_attn(page_table, pad_heads(q), pad_heads(k), pad_heads(v), lfn8,
                             cache_k, cache_v, cache_logf, l)
        att_s = att_s[:, 0:ATT_HEADS, :].reshape(n, ATT_WIDTH)
        xs, snew = _sample_out(xs, att_s, yt, xs_s, z_s, gates, sst, scw, dsk, ng, wo, gpost, gmpre, gmpost, wup, wdn)
        outs_s[0].append(k.reshape(n, 1, ATT_HEADS, HEAD_DIM))
        outs_s[1].append(v.reshape(n, 1, ATT_HEADS, HEAD_DIM))
        outs_s[2].append(lf[:, COL_LOGF:COL_LOGF + ATT_HEADS].reshape(n, 1, ATT_HEADS))
        outs_s[3].append(s_new)
        outs_s[4].append(jnp.transpose(cnew, (1, 0, 2)))
        outs_s[5].append(jnp.transpose(snew, (1, 0, 2)))

    stacked_p = [jnp.stack(a) for a in outs_p]
    stacked_s = [jnp.stack(a) for a in outs_s]
    return (xp, xs.reshape(n, 1, D_MODEL), *stacked_p, *stacked_s)
```

```python
import functools

import jax
import jax.numpy as jnp
from jax import lax
from jax.experimental import pallas as pl
from jax.experimental.pallas import tpu as pltpu

f32 = jnp.float32
bf16 = jnp.bfloat16

D_MODEL = 1024
HEAD_DIM = 64
ATT_HEADS = 6
ATT_WIDTH = ATT_HEADS * HEAD_DIM
SSD_HEADS = 6
SSD_HEAD_DIM = 64
SSD_WIDTH = SSD_HEADS * SSD_HEAD_DIM
SSD_GROUPS = 2
D_STATE = 64
SSD_CONV = 4
XBC_WIDTH = SSD_WIDTH + 2 * SSD_GROUPS * D_STATE
CONV_WIDTH = 256
SCONV = 3
D_FF = 4 * D_MODEL
SSD_CHUNK = 128
PAGE_SIZE = 128
RMS_EPS = 1e-6
IN_SIZES = (ATT_WIDTH, ATT_WIDTH, ATT_WIDTH, ATT_HEADS, SSD_WIDTH, XBC_WIDTH, SSD_HEADS,
            CONV_WIDTH, CONV_WIDTH, CONV_WIDTH)

LANES = 128
SUBLANES = 8
VMEM_LIMIT_BYTES = 56 * 1024 * 1024

OFF_Q = 0
OFF_K = OFF_Q + ATT_WIDTH
OFF_V = OFF_K + ATT_WIDTH
OFF_Z = OFF_V + ATT_WIDTH
OFF_XBC = OFF_Z + SSD_WIDTH
OFF_GB = OFF_XBC + XBC_WIDTH
OFF_GC = OFF_GB + CONV_WIDTH
OFF_HC = OFF_GC + CONV_WIDTH
OFF_SMALL = OFF_HC + CONV_WIDTH
PROJ_WIDTH = OFF_SMALL + LANES
NOFF_Z = 0
NOFF_XBC = OFF_XBC - OFF_Z
NOFF_GATES = OFF_GB - OFF_Z
NOFF_SMALL = OFF_SMALL - OFF_Z
NOFF_END = PROJ_WIDTH - OFF_Z
COL_LOGF = 0
COL_DT = ATT_HEADS
PADDED_HEAD = LANES
NEG_BIG = -1e30


def _rms(x, g):
    var = jnp.mean(x * x, axis=-1, keepdims=True)
    return (x * lax.rsqrt(var + RMS_EPS)) * g


def _silu(x):
    return x * (1.0 / (1.0 + jnp.exp(-x)))


def _split3(a):
    a1 = a.astype(bf16)
    r1 = a - a1.astype(f32)
    a2 = r1.astype(bf16)
    a3 = (r1 - a2.astype(f32)).astype(bf16)
    return a1, a2, a3


def _exact_dot(m01, a):
    a1, a2, a3 = _split3(a)
    d = functools.partial(jnp.dot, preferred_element_type=f32)
    return d(m01, a1) + d(m01, a2) + d(m01, a3)


def _exact_dot_rhs(a, m01):
    a1, a2, a3 = _split3(a)
    d = functools.partial(jnp.dot, preferred_element_type=f32)
    return d(a1, m01) + d(a2, m01) + d(a3, m01)


def _dot_nt(a, b):
    return lax.dot_general(a, b, (((1,), (1,)), ((), ())), preferred_element_type=f32)


def _softplus_parts(t):
    sp = jnp.log1p(jnp.exp(-jnp.abs(t)))
    return jnp.minimum(t, 0.0) - sp, jnp.maximum(t, 0.0) + sp


def _shift_rows(x, k, tail):
    row = lax.broadcasted_iota(jnp.int32, x.shape, 0)
    y = pltpu.roll(x, k, 0)
    for r in range(k):
        y = jnp.where(row == r, tail[SUBLANES - k + r:SUBLANES - k + r + 1, :], y)
    return y


def _pick_tile(n, pref):
    t = min(n, pref)
    assert n % t == 0
    return t


def _prompt_in_body(x_ref, g_ref, wt_ref, wn_ref, bias_ref, tri_ref, scw_ref,
                    qtp_ref, kp_ref, vtp_ref, kt32_ref, vt32_ref, z_ref, xbc_ref, yconv_ref,
                    lfdt_ref, lft_ref, utail_ref,
                    carry_ref, tail_ref, *, tq, tk):
    tm = x_ref.shape[1]

    @pl.when(pl.program_id(1) == 0)
    def _():
        carry_ref[...] = jnp.zeros_like(carry_ref)
        tail_ref[...] = jnp.zeros_like(tail_ref)

    h = _rms(x_ref[0], g_ref[...]).astype(bf16)
    dot = functools.partial(jnp.dot, preferred_element_type=f32)

    t = dot(h, wn_ref[:, NOFF_SMALL:NOFF_END]) + bias_ref[...]
    logf, dt = _softplus_parts(t)
    col = lax.broadcasted_iota(jnp.int32, t.shape, 1)
    lfdt_ref[0] = jnp.where(col < COL_DT, logf, dt)
    lft_ref[0] = logf.T[0:SUBLANES, :]
    c = _exact_dot(tri_ref[...], logf) + carry_ref[...]
    carry_ref[...] = c[tm - 1:tm, :]
    c1, c2, c3 = [p.astype(f32) for p in _split3(c)]
    ct1, ct2, ct3 = [p.astype(f32) for p in _split3(c.T[0:SUBLANES, :])]

    qkv_t = _dot_nt(wt_ref[...], h)
    kt = qkv_t[ATT_WIDTH:2 * ATT_WIDTH]
    vt = qkv_t[2 * ATT_WIDTH:3 * ATT_WIDTH]
    kt32_ref[0] = kt
    vt32_ref[0] = vt
    k = kt.T
    lane = lax.broadcasted_iota(jnp.int32, (tm, PADDED_HEAD - HEAD_DIM), 1)
    row = lax.broadcasted_iota(jnp.int32, (PADDED_HEAD - HEAD_DIM, tm), 0)
    ones_row = jnp.where(row == 0, 1.0, 0.0).astype(bf16)
    for hh in range(ATT_HEADS):
        src = slice(HEAD_DIM * hh, HEAD_DIM * (hh + 1))
        lo = slice(PADDED_HEAD * hh, PADDED_HEAD * hh + HEAD_DIM)
        hi = slice(PADDED_HEAD * hh + HEAD_DIM, PADDED_HEAD * (hh + 1))
        hc = slice(hh, hh + 1)
        kp_ref[0, :, lo] = k[:, src].astype(bf16)
        ek = jnp.where(lane < 3, 1.0, jnp.where(lane == 3, -c1[:, hc], jnp.where(
            lane == 4, -c2[:, hc], jnp.where(lane == 5, -c3[:, hc], 0.0))))
        kp_ref[0, :, hi] = ek.astype(bf16)
        eq = jnp.where(row == 0, ct1[hc, :], jnp.where(row == 1, ct2[hc, :], jnp.where(
            row == 2, ct3[hc, :], jnp.where(row < 6, 1.0, 0.0)))).astype(bf16)
        qh = (qkv_t[src] * (HEAD_DIM ** -0.5)).astype(bf16)
        for i in range(tm // tq):
            cs = slice(i * tq, (i + 1) * tq)
            qtp_ref[0, i, lo, :] = qh[:, cs]
            qtp_ref[0, i, hi, :] = eq[:, cs]
        vh = vt[src].astype(bf16)
        for i in range(tm // tk):
            cs = slice(i * tk, (i + 1) * tk)
            vtp_ref[0, i, lo, :] = vh[:, cs]
            vtp_ref[0, i, hi, :] = ones_row[:, cs]

    z_ref[0] = dot(h, wn_ref[:, NOFF_Z:NOFF_XBC])
    xbc_ref[0] = dot(h, wn_ref[:, NOFF_XBC:NOFF_GATES])

    gates = dot(h, wn_ref[:, NOFF_GATES:NOFF_SMALL])
    gb = gates[:, 0:CONV_WIDTH]
    u = gates[:, CONV_WIDTH:2 * CONV_WIDTH] * gates[:, 2 * CONV_WIDTH:3 * CONV_WIDTH]
    tail = tail_ref[...]
    uc = (scw_ref[0:1, :] * _shift_rows(u, 2, tail) + scw_ref[1:2, :] * _shift_rows(u, 1, tail)
          + scw_ref[2:3, :] * u)
    yconv_ref[0] = (gb * uc).astype(bf16)
    tail_ref[...] = u[tm - SUBLANES:tm, :]
    utail_ref[0] = u[tm - SUBLANES:tm, :]


def _prompt_in(x, g, wt, wn, bias, scw, *, tm, tq, tk):
    B, L, _ = x.shape
    nj = L // tm
    pw = ATT_HEADS * PADDED_HEAD
    tri = jnp.tril(jnp.ones((tm, tm), f32)).astype(bf16)
    row = lambda width: pl.BlockSpec((1, tm, width), lambda b, j: (b, j, 0))
    colb = lambda height: pl.BlockSpec((1, height, tm), lambda b, j: (b, 0, j))
    const = lambda shape: pl.BlockSpec(shape, lambda b, j: (0,) * len(shape))
    out_shapes = (
        jax.ShapeDtypeStruct((B, L // tq, pw, tq), bf16),
        jax.ShapeDtypeStruct((B, L, pw), bf16),
        jax.ShapeDtypeStruct((B, L // tk, pw, tk), bf16),
        jax.ShapeDtypeStruct((B, ATT_WIDTH, L), f32),
        jax.ShapeDtypeStruct((B, ATT_WIDTH, L), f32),
        jax.ShapeDtypeStruct((B, L, SSD_WIDTH), f32),
        jax.ShapeDtypeStruct((B, L, XBC_WIDTH), f32),
        jax.ShapeDtypeStruct((B, L, CONV_WIDTH), bf16),
        jax.ShapeDtypeStruct((B, L, LANES), f32),
        jax.ShapeDtypeStruct((B, SUBLANES, L), f32),
        jax.ShapeDtypeStruct((B, SUBLANES, CONV_WIDTH), f32),
    )
    out_specs = (
        pl.BlockSpec((1, tm // tq, pw, tq), lambda b, j: (b, j, 0, 0)),
        row(pw),
        pl.BlockSpec((1, tm // tk, pw, tk), lambda b, j: (b, j, 0, 0)),
        colb(ATT_WIDTH), colb(ATT_WIDTH), row(SSD_WIDTH), row(XBC_WIDTH), row(CONV_WIDTH),
        row(LANES), colb(SUBLANES),
        pl.BlockSpec((1, SUBLANES, CONV_WIDTH), lambda b, j: (b, 0, 0)),
    )
    return pl.pallas_call(
        functools.partial(_prompt_in_body, tq=tq, tk=tk),
        out_shape=out_shapes,
        grid=(B, nj),
        in_specs=[row(D_MODEL), const((1, D_MODEL)), const((3 * ATT_WIDTH, D_MODEL)),
                  const((D_MODEL, NOFF_END)), const((1, LANES)), const((tm, tm)), const((SCONV, CONV_WIDTH))],
        out_specs=out_specs,
        scratch_shapes=[pltpu.VMEM((1, LANES), f32), pltpu.VMEM((SUBLANES, CONV_WIDTH), f32)],
        compiler_params=pltpu.CompilerParams(dimension_semantics=("arbitrary", "arbitrary"),
                                             vmem_limit_bytes=VMEM_LIMIT_BYTES),
        name="prompt_in",
    )(x, g, wt, wn, bias, tri, scw)


def _prompt_attn_body(qt_ref, kp_ref, vt_ref, o_ref, m_ref, acc_ref, *, tq, tk):
    qi = pl.program_id(1)
    m_ref[...] = jnp.full(m_ref.shape, NEG_BIG, f32)
    acc_ref[...] = jnp.zeros_like(acc_ref)
    nfull = lax.div(qi * tq, tk)

    def step(kj, masked):
        ks = pl.multiple_of(kj * tk, tk)
        if masked:
            kpos = ks + lax.broadcasted_iota(jnp.int32, (tk, tq), 0)
            qpos = qi * tq + lax.broadcasted_iota(jnp.int32, (tk, tq), 1)
            valid = kpos <= qpos
        heads = [slice(PADDED_HEAD * hh, PADDED_HEAD * (hh + 1)) for hh in range(ATT_HEADS)]
        scores = []
        for hs in heads:
            st = jnp.dot(kp_ref[0, pl.ds(ks, tk), hs], qt_ref[0, 0, hs, :], preferred_element_type=f32)
            scores.append(jnp.where(valid, st, NEG_BIG) if masked else st)
        probs = []
        for hh, st in enumerate(scores):
            m_old = m_ref[hh]
            m_new = jnp.maximum(m_old, jnp.max(st, axis=0, keepdims=True))
            m_ref[hh] = m_new
            probs.append((jnp.exp(m_old[0:1] - m_new[0:1]),
                          jnp.exp(st - m_new[0:1]).astype(bf16)))
        for hh, (alpha, pt) in enumerate(probs):
            acc_ref[hh] = alpha * acc_ref[hh] + jnp.dot(vt_ref[0, kj, heads[hh], :], pt,
                                                        preferred_element_type=f32)

    def full_step(kj, carry):
        step(kj, False)
        return carry

    lax.fori_loop(0, nfull, full_step, 0)
    step(nfull, True)
    for hh in range(ATT_HEADS):
        a = acc_ref[hh].T
        o_ref[0, :, HEAD_DIM * hh:HEAD_DIM * (hh + 1)] = (
            a[:, 0:HEAD_DIM] / a[:, HEAD_DIM:HEAD_DIM + 1]).astype(bf16)


def _prompt_attn(qtp, kp, vtp, *, tq, tk):
    B, nq, W, _ = qtp.shape
    L = kp.shape[1]
    return pl.pallas_call(
        functools.partial(_prompt_attn_body, tq=tq, tk=tk),
        out_shape=jax.ShapeDtypeStruct((B, L, ATT_WIDTH), bf16),
        grid=(B, nq),
        in_specs=[pl.BlockSpec((1, 1, W, tq), lambda b, i: (b, i, 0, 0)),
                  pl.BlockSpec((1, L, W), lambda b, i: (b, 0, 0)),
                  pl.BlockSpec((1, L // tk, W, tk), lambda b, i: (b, 0, 0, 0))],
        out_specs=pl.BlockSpec((1, tq, ATT_WIDTH), lambda b, i: (b, i, 0)),
        scratch_shapes=[pltpu.VMEM((ATT_HEADS, SUBLANES, tq), f32),
                        pltpu.VMEM((ATT_HEADS, PADDED_HEAD, tq), f32)],
        compiler_params=pltpu.CompilerParams(dimension_semantics=("arbitrary", "arbitrary"),
                                             vmem_limit_bytes=VMEM_LIMIT_BYTES),
        name="prompt_attn",
    )(qtp, kp, vtp)


def _prompt_ssd_body(xbc_ref, lfdt_ref, z_ref, cw_ref, cb_ref, a_ref, dsk_ref, ng_ref, tri_ref,
                     y_ref, sout_ref, s_ref, tail_ref, ybuf_ref):
    tm = xbc_ref.shape[1]
    Q = SSD_CHUNK

    @pl.when(pl.program_id(1) == 0)
    def _():
        s_ref[...] = jnp.zeros_like(s_ref)
        tail_ref[...] = jnp.zeros_like(tail_ref)

    x = xbc_ref[0]
    tail = tail_ref[...]
    xc = (cw_ref[0:1, :] * _shift_rows(x, 3, tail) + cw_ref[1:2, :] * _shift_rows(x, 2, tail)
          + cw_ref[2:3, :] * _shift_rows(x, 1, tail) + cw_ref[3:4, :] * x + cb_ref[...])
    xc = _silu(xc)
    tail_ref[...] = x[tm - SUBLANES:tm, :]

    dt_all = lfdt_ref[0]
    dta_all = dt_all * a_ref[...]
    row = lax.broadcasted_iota(jnp.int32, (Q, Q), 0)
    colm = lax.broadcasted_iota(jnp.int32, (Q, Q), 1)
    causal = row >= colm
    tri = tri_ref[...]

    for c in range(tm // Q):
        rs = slice(c * Q, (c + 1) * Q)
        dt = dt_all[rs]
        acum = _exact_dot(tri, dta_all[rs])
        acum_t = acum.T
        dt_t = dt.T
        xs = xc[rs, 0:SSD_WIDTH]
        xs_t = xs.T.astype(bf16)
        xs_b = xs.astype(bf16)
        bm = xc[rs, SSD_WIDTH:SSD_WIDTH + SSD_GROUPS * D_STATE]
        cm = xc[rs, SSD_WIDTH + SSD_GROUPS * D_STATE:XBC_WIDTH].astype(bf16)
        cb = []
        for g in range(SSD_GROUPS):
            gs = slice(g * D_STATE, (g + 1) * D_STATE)
            cb.append(_dot_nt(cm[:, gs], bm[:, gs].astype(bf16)))
        for hh in range(SSD_HEADS):
            g = hh // (SSD_HEADS // SSD_GROUPS)
            gs = slice(g * D_STATE, (g + 1) * D_STATE)
            hs = slice(hh * SSD_HEAD_DIM, (hh + 1) * SSD_HEAD_DIM)
            cc = COL_DT + hh
            a_col = acum[:, cc:cc + 1]
            a_row = acum_t[cc:cc + 1, :]
            a_last = acum[Q - 1:Q, cc:cc + 1]
            decay = jnp.exp(jnp.where(causal, a_col - a_row, -jnp.inf))
            mat = cb[g] * decay * dt_t[cc:cc + 1, :]
            y_intra = jnp.dot(mat.astype(bf16), xs_b[:, hs], preferred_element_type=f32)
            s_in = s_ref[hh]
            y_inter = _dot_nt(cm[:, gs], s_in.astype(bf16)) * jnp.exp(a_col)
            wcol = jnp.exp(a_last - a_col) * dt[:, cc:cc + 1]
            bw = (bm[:, gs] * wcol).astype(bf16)
            s_ref[hh] = jnp.exp(a_last) * s_in + jnp.dot(xs_t[hs, :], bw, preferred_element_type=f32)
            ybuf_ref[:, hs] = y_intra + y_inter + dsk_ref[:, hs] * xs[:, hs]
        gated = ybuf_ref[...] * _silu(z_ref[0, rs, :])
        y_ref[0, rs, :] = _rms(gated, ng_ref[...]).astype(bf16)
    sout_ref[0] = s_ref[...]


def _prompt_ssd(xbc, lfdt, z, cw, cb, a_row, dsk, ng, *, tm):
    B, L, _ = xbc.shape
    tri = jnp.tril(jnp.ones((SSD_CHUNK, SSD_CHUNK), f32)).astype(bf16)
    row = lambda width: pl.BlockSpec((1, tm, width), lambda b, j: (b, j, 0))
    const = lambda shape: pl.BlockSpec(shape, lambda b, j: (0,) * len(shape))
    return pl.pallas_call(
        _prompt_ssd_body,
        out_shape=(jax.ShapeDtypeStruct((B, L, SSD_WIDTH), bf16),
                   jax.ShapeDtypeStruct((B, SSD_HEADS, SSD_HEAD_DIM, D_STATE), f32)),
        grid=(B, L // tm),
        in_specs=[row(XBC_WIDTH), row(LANES), row(SSD_WIDTH), const((SSD_CONV, XBC_WIDTH)),
                  const((1, XBC_WIDTH)), const((1, LANES)), const((1, SSD_WIDTH)), const((1, SSD_WIDTH)),
                  const((SSD_CHUNK, SSD_CHUNK))],
        out_specs=(row(SSD_WIDTH),
                   pl.BlockSpec((1, SSD_HEADS, SSD_HEAD_DIM, D_STATE), lambda b, j: (b, 0, 0, 0))),
        scratch_shapes=[pltpu.VMEM((SSD_HEADS, SSD_HEAD_DIM, D_STATE), f32),
                        pltpu.VMEM((SUBLANES, XBC_WIDTH), f32),
                        pltpu.VMEM((SSD_CHUNK, SSD_WIDTH), f32)],
        compiler_params=pltpu.CompilerParams(dimension_semantics=("arbitrary", "arbitrary"),
                                             vmem_limit_bytes=VMEM_LIMIT_BYTES),
        name="prompt_ssd",
    )(xbc, lfdt, z, cw, cb, a_row, dsk, ng, tri)


def _mlp_tail(x, mix, gpost, gpre, gmpost, wup_ref, wdn_ref, ff_chunk):
    x1 = x + _rms(mix, gpost)
    hmid = _rms(x1, gpre).astype(bf16)
    acc = jnp.zeros_like(x1)
    for c in range(D_FF // ff_chunk):
        cs = slice(c * ff_chunk, (c + 1) * ff_chunk)
        up = jnp.dot(hmid, wup_ref[:, cs], preferred_element_type=f32)
        act = jnp.square(jnp.maximum(up, 0.0)).astype(bf16)
        acc = acc + jnp.dot(act, wdn_ref[cs, :], preferred_element_type=f32)
    return x1 + _rms(acc, gmpost)


def _prompt_out_body(x_ref, att_ref, yssd_ref, yconv_ref, wo_ref, gpost_ref, gpre_ref, gmpost_ref,
                     wup_ref, wdn_ref, o_ref, *, ff_chunk):
    dot = functools.partial(jnp.dot, preferred_element_type=f32)
    mix = (dot(att_ref[...], wo_ref[0:ATT_WIDTH, :])
           + dot(yssd_ref[...], wo_ref[ATT_WIDTH:ATT_WIDTH + SSD_WIDTH, :])
           + dot(yconv_ref[...], wo_ref[ATT_WIDTH + SSD_WIDTH:D_MODEL, :]))
    o_ref[...] = _mlp_tail(x_ref[...], mix, gpost_ref[...], gpre_ref[...], gmpost_ref[...],
                           wup_ref, wdn_ref, ff_chunk)


def _prompt_out(x, att, yssd, yconv, wo, gpost, gpre, gmpost, wup, wdn, *, tm):
    M = x.shape[0]
    row = lambda width: pl.BlockSpec((tm, width), lambda i: (i, 0))
    const = lambda shape: pl.BlockSpec(shape, lambda i: (0, 0), pipeline_mode=pl.Buffered(1))
    return pl.pallas_call(
        functools.partial(_prompt_out_body, ff_chunk=512),
        out_shape=jax.ShapeDtypeStruct((M, D_MODEL), f32),
        grid=(M // tm,),
        in_specs=[row(D_MODEL), row(ATT_WIDTH), row(SSD_WIDTH), row(CONV_WIDTH),
                  const((D_MODEL, D_MODEL)), const((1, D_MODEL)), const((1, D_MODEL)), const((1, D_MODEL)),
                  const((D_MODEL, D_FF)), const((D_FF, D_MODEL))],
        out_specs=row(D_MODEL),
        compiler_params=pltpu.CompilerParams(dimension_semantics=("arbitrary",),
                                             vmem_limit_bytes=VMEM_LIMIT_BYTES),
        name="prompt_out_mlp",
    )(x, att, yssd, yconv, wo, gpost, gpre, gmpost, wup, wdn)


def _sample_in_body(x_ref, g_ref, w_ref, bias_ref, cst_ref, cw_ref, cb_ref, aexp_ref,
                    q_ref, k_ref, v_ref, lf_ref, z_ref, gates_ref, xs_ref, bm_ref, cm_ref,
                    dtxt_ref, dect_ref, cnew_ref):
    n = x_ref.shape[0]
    h = _rms(x_ref[...], g_ref[...]).astype(bf16)
    proj = jnp.dot(h, w_ref[...], preferred_element_type=f32)
    q_ref[...] = proj[:, OFF_Q:OFF_K] * (HEAD_DIM ** -0.5)
    k_ref[...] = proj[:, OFF_K:OFF_V]
    v_ref[...] = proj[:, OFF_V:OFF_Z]
    z_ref[...] = proj[:, OFF_Z:OFF_XBC]
    gates_ref[...] = proj[:, OFF_GB:OFF_SMALL]
    logf, dt = _softplus_parts(proj[:, OFF_SMALL:PROJ_WIDTH] + bias_ref[...])
    lf_ref[...] = logf

    xbc = proj[:, OFF_XBC:OFF_GB]
    xc = (cw_ref[0:1, :] * cst_ref[0] + cw_ref[1:2, :] * cst_ref[1] + cw_ref[2:3, :] * cst_ref[2]
          + cw_ref[3:4, :] * xbc + cb_ref[...])
    xc = _silu(xc)
    cnew_ref[0] = cst_ref[1]
    cnew_ref[1] = cst_ref[2]
    cnew_ref[2] = xbc
    xs = xc[:, 0:SSD_WIDTH]
    xs_ref[...] = xs
    bm_ref[...] = xc[:, SSD_WIDTH:SSD_WIDTH + SSD_GROUPS * D_STATE]
    cm_ref[...] = xc[:, SSD_WIDTH + SSD_GROUPS * D_STATE:XBC_WIDTH]

    head = lax.broadcasted_iota(jnp.int32, (n, SSD_WIDTH), 1) // SSD_HEAD_DIM
    dt_exp = jnp.zeros((n, SSD_WIDTH), f32)
    for hh in range(SSD_HEADS):
        dt_exp = jnp.where(head == hh, dt[:, COL_DT + hh:COL_DT + hh + 1], dt_exp)
    pad = jnp.zeros((LANES - n, SSD_WIDTH), f32)
    dtxt_ref[...] = jnp.concatenate([dt_exp * xs, pad], axis=0).T
    dect_ref[...] = jnp.concatenate([jnp.exp(dt_exp * aexp_ref[...]), pad], axis=0).T


def _sample_in(x, g, w, bias, cst, cw, cb, aexp):
    n = x.shape[0]
    s = lambda *shape: jax.ShapeDtypeStruct(shape, f32)
    return pl.pallas_call(
        _sample_in_body,
        out_shape=(s(n, ATT_WIDTH), s(n, ATT_WIDTH), s(n, ATT_WIDTH), s(n, LANES), s(n, SSD_WIDTH),
                   s(n, 3 * CONV_WIDTH), s(n, SSD_WIDTH), s(n, SSD_GROUPS * D_STATE), s(n, SSD_GROUPS * D_STATE),
                   s(SSD_WIDTH, LANES), s(SSD_WIDTH, LANES), s(SSD_CONV - 1, n, XBC_WIDTH)),
        compiler_params=pltpu.CompilerParams(vmem_limit_bytes=VMEM_LIMIT_BYTES),
        name="sample_in",
    )(x, g, w, bias, cst, cw, cb, aexp)


def _sample_ssd_body(s_ref, dtxt_ref, dect_ref, bm_ref, cm_ref, snew_ref, yt_ref):
    b = pl.program_id(0)
    rows = SSD_WIDTH
    half = rows // SSD_GROUPS
    lane = lax.broadcasted_iota(jnp.int32, (rows, LANES), 1)
    sel = lane == b
    dcol = jnp.sum(jnp.where(sel, dect_ref[...], 0.0), axis=1, keepdims=True)
    xcol = jnp.sum(jnp.where(sel, dtxt_ref[...], 0.0), axis=1, keepdims=True)

    def expand(ref):
        r = ref[pl.ds(b, 1), :]
        return jnp.concatenate([jnp.broadcast_to(r[:, g * D_STATE:(g + 1) * D_STATE], (half, D_STATE))
                                for g in range(SSD_GROUPS)], axis=0)

    s_old = s_ref[0].reshape(rows, D_STATE)
    s_new = dcol * s_old + xcol * expand(bm_ref)
    snew_ref[0] = s_new.reshape(SSD_HEADS, SSD_HEAD_DIM, D_STATE)
    ycol = jnp.sum(s_new * expand(cm_ref), axis=1, keepdims=True)

    @pl.when(b == 0)
    def _():
        yt_ref[...] = jnp.zeros_like(yt_ref)

    yt_ref[...] += jnp.where(sel, ycol, 0.0)


def _sample_ssd(state, dtxt, dect, bm, cm):
    n = state.shape[0]
    const = lambda shape: pl.BlockSpec(shape, lambda b: (0, 0))
    sblk = pl.BlockSpec((1, SSD_HEADS, SSD_HEAD_DIM, D_STATE), lambda b: (b, 0, 0, 0))
    return pl.pallas_call(
        _sample_ssd_body,
        out_shape=(jax.ShapeDtypeStruct(state.shape, f32), jax.ShapeDtypeStruct((SSD_WIDTH, LANES), f32)),
        grid=(n,),
        in_specs=[sblk, const((SSD_WIDTH, LANES)), const((SSD_WIDTH, LANES)),
                  const(bm.shape), const(cm.shape)],
        out_specs=(sblk, const((SSD_WIDTH, LANES))),
        compiler_params=pltpu.CompilerParams(dimension_semantics=("arbitrary",)),
        name="sample_ssd",
    )(state, dtxt, dect, bm, cm)


def _page_bias_body(lf_ref, upper_ref, o_ref):
    pb = lf_ref.shape[1]
    for hh in range(ATT_HEADS):
        x = lf_ref[hh]
        o_ref[:, hh, 0:LANES] = _exact_dot_rhs(x, upper_ref[...])
        o_ref[:, hh, LANES:2 * LANES] = jnp.broadcast_to(jnp.sum(x, axis=1, keepdims=True), x.shape)
    o_ref[:, ATT_HEADS:SUBLANES, :] = jnp.zeros((pb, SUBLANES - ATT_HEADS, 2 * LANES), f32)


def _page_bias(lft):
    depth, _, pool, _ = lft.shape
    pb = _pick_tile(pool, 256)
    pos = jnp.arange(PAGE_SIZE)
    upper = (pos[:, None] > pos[None, :]).astype(bf16)
    return pl.pallas_call(
        _page_bias_body,
        out_shape=jax.ShapeDtypeStruct((depth, pool, SUBLANES, 2 * LANES), f32),
        grid=(depth, pool // pb),
        in_specs=[pl.BlockSpec((None, ATT_HEADS, pb, PAGE_SIZE), lambda d, i: (d, 0, i, 0)),
                  pl.BlockSpec((PAGE_SIZE, PAGE_SIZE), lambda d, i: (0, 0))],
        out_specs=pl.BlockSpec((None, pb, SUBLANES, 2 * LANES), lambda d, i: (d, i, 0, 0)),
        compiler_params=pltpu.CompilerParams(dimension_semantics=("arbitrary", "arbitrary")),
        name="page_bias",
    )(lft, upper)


def _sample_attn_body(pt_ref, qc_ref, knc_ref, vnc_ref, lfn_ref, *rest, pps):
    k_refs = rest[0:pps]
    v_refs = rest[pps:2 * pps]
    sb_refs = rest[2 * pps:3 * pps]
    o_ref, m_ref, l_ref, acc_ref, r_ref = rest[3 * pps:]
    j = pl.program_id(1)
    hrow = lax.broadcasted_iota(jnp.int32, (SUBLANES, LANES), 0)
    heads = [slice(HEAD_DIM * hh, HEAD_DIM * (hh + 1)) for hh in range(ATT_HEADS)]

    def scores(k_of_head):
        s = jnp.zeros((SUBLANES, LANES), f32)
        for hh in range(ATT_HEADS):
            prod = qc_ref[0, heads[hh], :] * k_of_head(hh)
            s = jnp.where(hrow == hh, jnp.sum(prod, axis=0, keepdims=True), s)
        return s

    @pl.when(j == 0)
    def _():
        m_ref[...] = jnp.full(m_ref.shape, NEG_BIG, f32)
        l_ref[...] = jnp.zeros_like(l_ref)
        acc_ref[...] = jnp.zeros_like(acc_ref)
        r_ref[...] = lfn_ref[0]

    for i in range(pps):
        sb = sb_refs[i][...]
        r = r_ref[...]
        s = scores(lambda hh: k_refs[i][hh]) + (r + sb[:, 0:LANES])
        r_ref[...] = r + sb[:, LANES:2 * LANES]
        m_old = m_ref[...]
        m_new = jnp.maximum(m_old, s)
        alpha = jnp.exp(m_old - m_new)
        p = jnp.exp(s - m_new)
        l_ref[...] = alpha * l_ref[...] + p
        m_ref[...] = m_new
        for hh in range(ATT_HEADS):
            acc_ref[hh] = alpha[hh:hh + 1, :] * acc_ref[hh] + p[hh:hh + 1, :] * v_refs[i][hh]

    @pl.when(j == pl.num_programs(1) - 1)
    def _():
        m = m_ref[...]
        s_new = scores(lambda hh: knc_ref[0, heads[hh], :])
        mx = jnp.maximum(jnp.max(m, axis=1, keepdims=True), s_new)
        w = jnp.exp(m - mx)
        w_new = jnp.exp(s_new - mx)
        denom = jnp.sum(l_ref[...] * w, axis=1, keepdims=True) + w_new
        for hh in range(ATT_HEADS):
            num = (jnp.sum(acc_ref[hh] * w[hh:hh + 1, :], axis=1, keepdims=True)
                   + w_new[hh:hh + 1, :] * vnc_ref[0, heads[hh], :])
            o_ref[0, heads[hh], :] = num / denom[hh:hh + 1, :]


def _sample_attn(page_table, qc, knc, vnc, lfn, kt, vt, sb, layer):
    n, n_pages = page_table.shape
    pps = min(8, n_pages)
    assert n_pages % pps == 0
    pt = page_table.reshape(-1)
    seq = lambda rows: pl.BlockSpec((1, rows, LANES), lambda b, j, pt: (b, 0, 0))

    def page(b, j, pt, i):
        return pt[b * n_pages + n_pages - 1 - (j * pps + i)]

    kv_specs = [pl.BlockSpec((None, None, ATT_HEADS, HEAD_DIM, PAGE_SIZE),
                             lambda b, j, pt, i=i: (layer, page(b, j, pt, i), 0, 0, 0)) for i in range(pps)]
    sb_specs = [pl.BlockSpec((None, None, SUBLANES, 2 * LANES),
                             lambda b, j, pt, i=i: (layer, page(b, j, pt, i), 0, 0)) for i in range(pps)]
    return pl.pallas_call(
        functools.partial(_sample_attn_body, pps=pps),
        out_shape=jax.ShapeDtypeStruct((n, ATT_WIDTH, LANES), f32),
        grid_spec=pltpu.PrefetchScalarGridSpec(
            num_scalar_prefetch=1,
            grid=(n, n_pages // pps),
            in_specs=[seq(ATT_WIDTH), seq(ATT_WIDTH), seq(ATT_WIDTH), seq(SUBLANES)]
            + kv_specs + kv_specs + sb_specs,
            out_specs=seq(ATT_WIDTH),
            scratch_shapes=[pltpu.VMEM((SUBLANES, LANES), f32), pltpu.VMEM((SUBLANES, LANES), f32),
                            pltpu.VMEM((ATT_HEADS, HEAD_DIM, LANES), f32), pltpu.VMEM((SUBLANES, LANES), f32)],
        ),
        compiler_params=pltpu.CompilerParams(dimension_semantics=("arbitrary", "arbitrary"),
                                             vmem_limit_bytes=VMEM_LIMIT_BYTES),
        name="sample_attn",
    )(pt, qc, knc, vnc, lfn, *([kt] * pps), *([vt] * pps), *([sb] * pps))


def _sample_out_body(x_ref, att_ref, yt_ref, xs_ref, z_ref, gates_ref, sst_ref, scw_ref, dsk_ref, ng_ref,
                     wo_ref, gpost_ref, gpre_ref, gmpost_ref, wup_ref, wdn_ref, o_ref, snew_ref, *, ff_chunk):
    n = x_ref.shape[0]
    y = yt_ref[...].T[0:n, :] + dsk_ref[...] * xs_ref[...]
    yssd = _rms(y * _silu(z_ref[...]), ng_ref[...])
    gates = gates_ref[...]
    u = gates[:, CONV_WIDTH:2 * CONV_WIDTH] * gates[:, 2 * CONV_WIDTH:3 * CONV_WIDTH]
    uc = scw_ref[0:1, :] * sst_ref[0] + scw_ref[1:2, :] * sst_ref[1] + scw_ref[2:3, :] * u
    yconv = gates[:, 0:CONV_WIDTH] * uc
    snew_ref[0] = sst_ref[1]
    snew_ref[1] = u
    dot = functools.partial(jnp.dot, preferred_element_type=f32)
    mix = (dot(att_ref[...].astype(bf16), wo_ref[0:ATT_WIDTH, :])
           + dot(yssd.astype(bf16), wo_ref[ATT_WIDTH:ATT_WIDTH + SSD_WIDTH, :])
           + dot(yconv.astype(bf16), wo_ref[ATT_WIDTH + SSD_WIDTH:D_MODEL, :]))
    o_ref[...] = _mlp_tail(x_ref[...], mix, gpost_ref[...], gpre_ref[...], gmpost_ref[...],
                           wup_ref, wdn_ref, ff_chunk)


def _sample_out(x, att, yt, xs, z, gates, sst, scw, dsk, ng, wo, gpost, gpre, gmpost, wup, wdn):
    n = x.shape[0]
    return pl.pallas_call(
        functools.partial(_sample_out_body, ff_chunk=512),
        out_shape=(jax.ShapeDtypeStruct((n, D_MODEL), f32),
                   jax.ShapeDtypeStruct((SCONV - 1, n, CONV_WIDTH), f32)),
        compiler_params=pltpu.CompilerParams(vmem_limit_bytes=VMEM_LIMIT_BYTES),
        name="sample_out_mlp",
    )(x, att, yt, xs, z, gates, sst, scw, dsk, ng, wo, gpost, gpre, gmpost, wup, wdn)


def _prep_w_in(w):
    offs = [0]
    for s in IN_SIZES:
        offs.append(offs[-1] + s)
    q, k, v, f, z, xbc, dt, gb, gc, hc = [w[:, offs[i]:offs[i + 1]] for i in range(len(IN_SIZES))]
    small = jnp.concatenate([f, dt, jnp.zeros((w.shape[0], LANES - ATT_HEADS - SSD_HEADS), w.dtype)], axis=1)
    return jnp.concatenate([q, k, v, z, xbc, gb, gc, hc, small], axis=1).astype(bf16)


def kernel(x_prompt, x_sample, cache_k, cache_v, cache_logf, state_ssm, state_ssd_conv, state_sconv, page_table,
           w_in, b_f, ssd_conv_w, ssd_conv_b, dt_bias, a_log, d_skip, ssd_norm_g, sconv_w, w_out,
           g_mix_pre, g_mix_post, g_mlp_pre, g_mlp_post, w_mlp_up, w_mlp_down):
    depth = w_in.shape[0]
    B, L, _ = x_prompt.shape
    n = x_sample.shape[0]
    assert x_sample.shape[1] == 1 and n <= LANES
    tm = _pick_tile(L, 512)
    tq = _pick_tile(L, 256)
    tk = _pick_tile(L, 512)

    kt_cache = jnp.transpose(cache_k, (0, 1, 3, 4, 2))
    vt_cache = jnp.transpose(cache_v, (0, 1, 3, 4, 2))
    page_terms = _page_bias(jnp.transpose(cache_logf, (0, 3, 1, 2)))

    xp = x_prompt
    xs = x_sample.reshape(n, D_MODEL)
    outs_p = [[] for _ in range(6)]
    outs_s = [[] for _ in range(6)]
    row = lambda a: a.reshape(1, -1)
    lanes_bcast = lambda a: jnp.broadcast_to(a[:, :, None], a.shape + (LANES,))

    for l in range(depth):
        w = _prep_w_in(w_in[l])
        wt = w[:, OFF_Q:OFF_Z].T
        wn = w[:, OFF_Z:PROJ_WIDTH]
        wo = w_out[l].astype(bf16)
        wup = w_mlp_up[l].astype(bf16)
        wdn = w_mlp_down[l].astype(bf16)
        bias = jnp.concatenate([b_f[l], dt_bias[l], jnp.zeros((LANES - ATT_HEADS - SSD_HEADS,), f32)]).reshape(1, LANES)
        a_neg = -jnp.exp(a_log[l])
        a_row = jnp.concatenate([jnp.zeros((COL_DT,), f32), a_neg,
                                 jnp.zeros((LANES - COL_DT - SSD_HEADS,), f32)]).reshape(1, LANES)
        a_exp = jnp.repeat(a_neg, SSD_HEAD_DIM).reshape(1, SSD_WIDTH)
        dsk = jnp.repeat(d_skip[l], SSD_HEAD_DIM).reshape(1, SSD_WIDTH)
        ng = row(ssd_norm_g[l])
        cw, cb, scw = ssd_conv_w[l], row(ssd_conv_b[l]), sconv_w[l]
        gpre, gpost, gmpre, gmpost = row(g_mix_pre[l]), row(g_mix_post[l]), row(g_mlp_pre[l]), row(g_mlp_post[l])

        (qtp, kp, vtp, kt32, vt32, z, xbc, yconv, lfdt, lft, utail) = _prompt_in(
            xp, gpre, wt, wn, bias, scw, tm=tm, tq=tq, tk=tk)
        att = _prompt_attn(qtp, kp, vtp, tq=tq, tk=tk)
        yssd, s_fin = _prompt_ssd(xbc, lfdt, z, cw, cb, a_row, dsk, ng, tm=tm)
        xp = _prompt_out(xp.reshape(B * L, D_MODEL), att.reshape(B * L, ATT_WIDTH), yssd.reshape(B * L, SSD_WIDTH),
                         yconv.reshape(B * L, CONV_WIDTH), wo, gpost, gmpre, gmpost, wup, wdn,
                         tm=tm).reshape(B, L, D_MODEL)
        outs_p[0].append(kt32)
        outs_p[1].append(vt32)
        outs_p[2].append(lft[:, 0:ATT_HEADS, :])
        outs_p[3].append(s_fin)
        outs_p[4].append(xbc[:, L - (SSD_CONV - 1):, :])
        outs_p[5].append(utail[:, SUBLANES - (SCONV - 1):, :])

        cst = jnp.transpose(state_ssd_conv[l], (1, 0, 2))
        sst = jnp.transpose(state_sconv[l], (1, 0, 2))
        (q, k, v, lf, z_s, gates, xs_s, bm, cm, dtxt, dect, cnew) = _sample_in(xs, gpre, w, bias, cst, cw, cb, a_exp)
        s_new, yt = _sample_ssd(state_ssm[l], dtxt, dect, bm, cm)
        lf6 = lf[:, COL_LOGF:COL_LOGF + ATT_HEADS]
        lfn = lanes_bcast(jnp.pad(lf6, ((0, 0), (0, SUBLANES - ATT_HEADS))))
        att_s = _sample_attn(page_table, lanes_bcast(q), lanes_bcast(k), lanes_bcast(v), lfn,
                             kt_cache, vt_cache, page_terms, l)[:, :, 0]
        xs, snew = _sample_out(xs, att_s, yt, xs_s, z_s, gates, sst, scw, dsk, ng, wo, gpost, gmpre, gmpost, wup, wdn)
        outs_s[0].append(k.reshape(n, 1, ATT_HEADS, HEAD_DIM))
        outs_s[1].append(v.reshape(n, 1, ATT_HEADS, HEAD_DIM))
        outs_s[2].append(lf6.reshape(n, 1, ATT_HEADS))
        outs_s[3].append(s_new)
        outs_s[4].append(jnp.transpose(cnew, (1, 0, 2)))
        outs_s[5].append(jnp.transpose(snew, (1, 0, 2)))

    heads_last = lambda a: jnp.transpose(jnp.stack(a).reshape(depth, B, ATT_HEADS, HEAD_DIM, L), (0, 1, 4, 2, 3))
    k_p = heads_last(outs_p[0])
    v_p = heads_last(outs_p[1])
    lf_p = jnp.transpose(jnp.stack(outs_p[2]), (0, 1, 3, 2))
    rest_p = [jnp.stack(a) for a in outs_p[3:]]
    stacked_s = [jnp.stack(a) for a in outs_s]
    return (xp, xs.reshape(n, 1, D_MODEL), k_p, v_p, lf_p, *rest_p, *stacked_s)
```

```python
import functools

import jax
import jax.numpy as jnp
from jax import lax
from jax.experimental import pallas as pl
from jax.experimental.pallas import tpu as pltpu

f32 = jnp.float32
bf16 = jnp.bfloat16

D_MODEL = 1024
HEAD_DIM = 64
ATT_HEADS = 6
ATT_WIDTH = ATT_HEADS * HEAD_DIM
SSD_HEADS = 6
SSD_HEAD_DIM = 64
SSD_WIDTH = SSD_HEADS * SSD_HEAD_DIM
SSD_GROUPS = 2
D_STATE = 64
SSD_CONV = 4
XBC_WIDTH = SSD_WIDTH + 2 * SSD_GROUPS * D_STATE
CONV_WIDTH = 256
SCONV = 3
D_FF = 4 * D_MODEL
SSD_CHUNK = 128
PAGE_SIZE = 128
PAGES_PER_STEP = 16
RMS_EPS = 1e-6
IN_SIZES = (ATT_WIDTH, ATT_WIDTH, ATT_WIDTH, ATT_HEADS, SSD_WIDTH, XBC_WIDTH, SSD_HEADS,
            CONV_WIDTH, CONV_WIDTH, CONV_WIDTH)

LANES = 128
SUBLANES = 8
VMEM_LIMIT_BYTES = 56 * 1024 * 1024

OFF_Q = 0
OFF_K = OFF_Q + ATT_WIDTH
OFF_V = OFF_K + ATT_WIDTH
OFF_Z = OFF_V + ATT_WIDTH
OFF_XBC = OFF_Z + SSD_WIDTH
OFF_GB = OFF_XBC + XBC_WIDTH
OFF_GC = OFF_GB + CONV_WIDTH
OFF_HC = OFF_GC + CONV_WIDTH
OFF_SMALL = OFF_HC + CONV_WIDTH
PROJ_WIDTH = OFF_SMALL + LANES
COL_LOGF = 0
COL_DT = ATT_HEADS
PADDED_HEAD = LANES
NEG_BIG = -1e30
LOG2E = 1.4426950408889634
VALUE_ROWS = 80


def _rms(x, g):
    var = jnp.mean(x * x, axis=-1, keepdims=True)
    return (x * lax.rsqrt(var + RMS_EPS)) * g


def _silu(x):
    return x * (1.0 / (1.0 + jnp.exp(-x)))


def _split3(a):
    a1 = a.astype(bf16)
    r1 = a - a1.astype(f32)
    a2 = r1.astype(bf16)
    a3 = (r1 - a2.astype(f32)).astype(bf16)
    return a1, a2, a3


def _exact_dot(m01, a):
    a1, a2, a3 = _split3(a)
    d = functools.partial(jnp.dot, preferred_element_type=f32)
    return d(m01, a1) + d(m01, a2) + d(m01, a3)


def _exact_dot_rhs(a, m01):
    a1, a2, a3 = _split3(a)
    d = functools.partial(jnp.dot, preferred_element_type=f32)
    return d(a1, m01) + d(a2, m01) + d(a3, m01)


def _dot_nt(a, b):
    return lax.dot_general(a, b, (((1,), (1,)), ((), ())), preferred_element_type=f32)


def _softplus_parts(t):
    sp = jnp.log1p(jnp.exp(-jnp.abs(t)))
    return jnp.minimum(t, 0.0) - sp, jnp.maximum(t, 0.0) + sp


def _shift_rows(x, k, tail):
    row = lax.broadcasted_iota(jnp.int32, x.shape, 0)
    y = pltpu.roll(x, k, 0)
    for r in range(k):
        y = jnp.where(row == r, tail[SUBLANES - k + r:SUBLANES - k + r + 1, :], y)
    return y


def _pick_tile(n, pref):
    t = min(n, pref)
    assert n % t == 0
    return t


def _prompt_in_body(x_ref, g_ref, w_ref, bias_ref, tri_ref, scw_ref,
                    qtp_ref, kp_ref, vtp_ref, kt32_ref, vt32_ref, z_ref, xbc_ref, yconv_ref,
                    lfdt_ref, lft_ref, utail_ref,
                    carry_ref, tail_ref, *, tq, tk):
    tm = x_ref.shape[1]

    @pl.when(pl.program_id(1) == 0)
    def _():
        carry_ref[...] = jnp.zeros_like(carry_ref)
        tail_ref[...] = jnp.zeros_like(tail_ref)

    h = _rms(x_ref[0], g_ref[...]).astype(bf16)
    proj = lambda lo, hi: _dot_nt(h, w_ref[lo:hi, :])

    t = proj(OFF_SMALL, PROJ_WIDTH) + bias_ref[...]
    logf, dt = _softplus_parts(t)
    col = lax.broadcasted_iota(jnp.int32, t.shape, 1)
    lfdt_ref[0] = jnp.where(col < COL_DT, logf, dt)
    lft_ref[0] = logf.T[0:SUBLANES, :]
    c = _exact_dot(tri_ref[...], logf) + carry_ref[...]
    carry_ref[...] = c[tm - 1:tm, :]
    c = c * LOG2E
    c1, c2, c3 = [p.astype(f32) for p in _split3(c)]
    ct1, ct2, ct3 = [p.astype(f32) for p in _split3(c.T[0:SUBLANES, :])]

    qkv_t = _dot_nt(w_ref[OFF_Q:OFF_Z, :], h)
    kt = qkv_t[ATT_WIDTH:2 * ATT_WIDTH]
    vt = qkv_t[2 * ATT_WIDTH:3 * ATT_WIDTH]
    kt32_ref[0] = kt
    vt32_ref[0] = vt
    k = kt.T
    lane = lax.broadcasted_iota(jnp.int32, (tm, PADDED_HEAD - HEAD_DIM), 1)
    row = lax.broadcasted_iota(jnp.int32, (PADDED_HEAD - HEAD_DIM, tm), 0)
    vrow = lax.broadcasted_iota(jnp.int32, (VALUE_ROWS - HEAD_DIM, tm), 0)
    ones_row = jnp.where(vrow == 0, 1.0, 0.0).astype(bf16)
    for hh in range(ATT_HEADS):
        src = slice(HEAD_DIM * hh, HEAD_DIM * (hh + 1))
        lo = slice(PADDED_HEAD * hh, PADDED_HEAD * hh + HEAD_DIM)
        hi = slice(PADDED_HEAD * hh + HEAD_DIM, PADDED_HEAD * (hh + 1))
        hc = slice(hh, hh + 1)
        kp_ref[0, :, lo] = k[:, src].astype(bf16)
        ek = jnp.where(lane < 3, 1.0, jnp.where(lane == 3, -c1[:, hc], jnp.where(
            lane == 4, -c2[:, hc], jnp.where(lane == 5, -c3[:, hc], 0.0))))
        kp_ref[0, :, hi] = ek.astype(bf16)
        eq = jnp.where(row == 0, ct1[hc, :], jnp.where(row == 1, ct2[hc, :], jnp.where(
            row == 2, ct3[hc, :], jnp.where(row < 6, 1.0, 0.0)))).astype(bf16)
        qh = (qkv_t[src] * (HEAD_DIM ** -0.5 * LOG2E)).astype(bf16)
        for i in range(tm // tq):
            cs = slice(i * tq, (i + 1) * tq)
            qtp_ref[0, i, lo, :] = qh[:, cs]
            qtp_ref[0, i, hi, :] = eq[:, cs]
        vh = vt[src].astype(bf16)
        for i in range(tm // tk):
            cs = slice(i * tk, (i + 1) * tk)
            vtp_ref[0, i, VALUE_ROWS * hh:VALUE_ROWS * hh + HEAD_DIM, :] = vh[:, cs]
            vtp_ref[0, i, VALUE_ROWS * hh + HEAD_DIM:VALUE_ROWS * (hh + 1), :] = ones_row[:, cs]

    z_ref[0] = proj(OFF_Z, OFF_XBC)
    xbc_ref[0] = proj(OFF_XBC, OFF_GB)

    gates = proj(OFF_GB, OFF_SMALL)
    gb = gates[:, 0:CONV_WIDTH]
    u = gates[:, CONV_WIDTH:2 * CONV_WIDTH] * gates[:, 2 * CONV_WIDTH:3 * CONV_WIDTH]
    tail = tail_ref[...]
    uc = (scw_ref[0:1, :] * _shift_rows(u, 2, tail) + scw_ref[1:2, :] * _shift_rows(u, 1, tail)
          + scw_ref[2:3, :] * u)
    yconv_ref[0] = (gb * uc).astype(bf16)
    tail_ref[...] = u[tm - SUBLANES:tm, :]
    utail_ref[0] = u[tm - SUBLANES:tm, :]


def _prompt_in(x, g, w, bias, scw, *, tm, tq, tk):
    B, L, _ = x.shape
    nj = L // tm
    pw = ATT_HEADS * PADDED_HEAD
    tri = jnp.tril(jnp.ones((tm, tm), f32)).astype(bf16)
    row = lambda width: pl.BlockSpec((1, tm, width), lambda b, j: (b, j, 0))
    colb = lambda height: pl.BlockSpec((1, height, tm), lambda b, j: (b, 0, j))
    const = lambda shape: pl.BlockSpec(shape, lambda b, j: (0,) * len(shape))
    out_shapes = (
        jax.ShapeDtypeStruct((B, L // tq, pw, tq), bf16),
        jax.ShapeDtypeStruct((B, L, pw), bf16),
        jax.ShapeDtypeStruct((B, L // tk, ATT_HEADS * VALUE_ROWS, tk), bf16),
        jax.ShapeDtypeStruct((B, ATT_WIDTH, L), f32),
        jax.ShapeDtypeStruct((B, ATT_WIDTH, L), f32),
        jax.ShapeDtypeStruct((B, L, SSD_WIDTH), f32),
        jax.ShapeDtypeStruct((B, L, XBC_WIDTH), f32),
        jax.ShapeDtypeStruct((B, L, CONV_WIDTH), bf16),
        jax.ShapeDtypeStruct((B, L, LANES), f32),
        jax.ShapeDtypeStruct((B, SUBLANES, L), f32),
        jax.ShapeDtypeStruct((B, SUBLANES, CONV_WIDTH), f32),
    )
    out_specs = (
        pl.BlockSpec((1, tm // tq, pw, tq), lambda b, j: (b, j, 0, 0)),
        row(pw),
        pl.BlockSpec((1, tm // tk, ATT_HEADS * VALUE_ROWS, tk), lambda b, j: (b, j, 0, 0)),
        colb(ATT_WIDTH), colb(ATT_WIDTH), row(SSD_WIDTH), row(XBC_WIDTH), row(CONV_WIDTH),
        row(LANES), colb(SUBLANES),
        pl.BlockSpec((1, SUBLANES, CONV_WIDTH), lambda b, j: (b, 0, 0)),
    )
    return pl.pallas_call(
        functools.partial(_prompt_in_body, tq=tq, tk=tk),
        out_shape=out_shapes,
        grid=(B, nj),
        in_specs=[row(D_MODEL), const((1, D_MODEL)), const((PROJ_WIDTH, D_MODEL)), const((1, LANES)),
                  const((tm, tm)), const((SCONV, CONV_WIDTH))],
        out_specs=out_specs,
        scratch_shapes=[pltpu.VMEM((1, LANES), f32), pltpu.VMEM((SUBLANES, CONV_WIDTH), f32)],
        compiler_params=pltpu.CompilerParams(dimension_semantics=("arbitrary", "arbitrary"),
                                             vmem_limit_bytes=VMEM_LIMIT_BYTES),
        name="prompt_in",
    )(x, g, w, bias, tri, scw)


def _prompt_attn_body(qt_ref, kp_ref, vt_ref, o_ref, m_ref, acc_ref, st_ref, *, tq, tk):
    qi = pl.program_id(1)
    m_ref[...] = jnp.full(m_ref.shape, NEG_BIG, f32)
    acc_ref[...] = jnp.zeros_like(acc_ref)
    nfull = lax.div(qi * tq, tk)
    heads = [slice(PADDED_HEAD * hh, PADDED_HEAD * (hh + 1)) for hh in range(ATT_HEADS)]
    vrows = [slice(VALUE_ROWS * hh, VALUE_ROWS * (hh + 1)) for hh in range(ATT_HEADS)]

    def score(kj, slot, masked):
        ks = pl.multiple_of(kj * tk, tk)
        if masked:
            kpos = ks + lax.broadcasted_iota(jnp.int32, (tk, tq), 0)
            qpos = qi * tq + lax.broadcasted_iota(jnp.int32, (tk, tq), 1)
            valid = kpos <= qpos
        for hh, hs in enumerate(heads):
            st = jnp.dot(kp_ref[0, pl.ds(ks, tk), hs], qt_ref[0, 0, hs, :], preferred_element_type=f32)
            st_ref[slot, hh] = jnp.where(valid, st, NEG_BIG) if masked else st

    def absorb(kj, slot):
        probs = []
        for hh in range(ATT_HEADS):
            st = st_ref[slot, hh]
            m_old = m_ref[hh]
            m_new = jnp.maximum(m_old, jnp.max(st, axis=0, keepdims=True))
            m_ref[hh] = m_new
            probs.append((jnp.exp2(m_old[0:1] - m_new[0:1]), jnp.exp2(st - m_new[0:1]).astype(bf16)))
        for hh, (alpha, pt) in enumerate(probs):
            acc_ref[hh] = alpha * acc_ref[hh] + jnp.dot(vt_ref[0, kj, vrows[hh], :], pt,
                                                        preferred_element_type=f32)

    @pl.when(nfull > 0)
    def _():
        score(0, 0, False)

    def pair(pi, carry):
        k0 = 2 * pi
        score(k0 + 1, 1, False)
        absorb(k0, 0)
        score(k0 + 2, 0, False)
        absorb(k0 + 1, 1)
        return carry

    ntrip = lax.div(jnp.maximum(nfull - 1, 0), 2)
    lax.fori_loop(0, ntrip, pair, 0)
    done = 2 * ntrip
    rem = nfull - done

    @pl.when(rem == 2)
    def _():
        score(done + 1, 1, False)
        absorb(done, 0)
        score(done + 2, 0, True)
        absorb(done + 1, 1)
        absorb(done + 2, 0)

    @pl.when(rem == 1)
    def _():
        score(done + 1, 1, True)
        absorb(done, 0)
        absorb(done + 1, 1)

    @pl.when(rem == 0)
    def _():
        score(0, 0, True)
        absorb(0, 0)

    pad = jnp.zeros((PADDED_HEAD - VALUE_ROWS, tq), f32)
    for hh in range(ATT_HEADS):
        a = jnp.concatenate([acc_ref[hh], pad], axis=0).T
        o_ref[0, :, HEAD_DIM * hh:HEAD_DIM * (hh + 1)] = (
            a[:, 0:HEAD_DIM] / a[:, HEAD_DIM:HEAD_DIM + 1]).astype(bf16)


def _prompt_attn(qtp, kp, vtp, *, tq, tk):
    B, nq, W, _ = qtp.shape
    L = kp.shape[1]
    vw = ATT_HEADS * VALUE_ROWS
    return pl.pallas_call(
        functools.partial(_prompt_attn_body, tq=tq, tk=tk),
        out_shape=jax.ShapeDtypeStruct((B, L, ATT_WIDTH), bf16),
        grid=(B, nq),
        in_specs=[pl.BlockSpec((1, 1, W, tq), lambda b, i: (b, i, 0, 0)),
                  pl.BlockSpec((1, L, W), lambda b, i: (b, 0, 0)),
                  pl.BlockSpec((1, L // tk, vw, tk), lambda b, i: (b, 0, 0, 0))],
        out_specs=pl.BlockSpec((1, tq, ATT_WIDTH), lambda b, i: (b, i, 0)),
        scratch_shapes=[pltpu.VMEM((ATT_HEADS, SUBLANES, tq), f32),
                        pltpu.VMEM((ATT_HEADS, VALUE_ROWS, tq), f32),
                        pltpu.VMEM((2, ATT_HEADS, tk, tq), f32)],
        compiler_params=pltpu.CompilerParams(dimension_semantics=("arbitrary", "arbitrary"),
                                             vmem_limit_bytes=VMEM_LIMIT_BYTES),
        name="prompt_attn",
    )(qtp, kp, vtp)


def _prompt_ssd_body(xbc_ref, lfdt_ref, z_ref, cw_ref, cb_ref, a_ref, dsk_ref, ng_ref, tri_ref,
                     y_ref, sout_ref, s_ref, tail_ref, ybuf_ref):
    tm = xbc_ref.shape[1]
    Q = SSD_CHUNK

    @pl.when(pl.program_id(1) == 0)
    def _():
        s_ref[...] = jnp.zeros_like(s_ref)
        tail_ref[...] = jnp.zeros_like(tail_ref)

    x = xbc_ref[0]
    tail = tail_ref[...]
    xc = (cw_ref[0:1, :] * _shift_rows(x, 3, tail) + cw_ref[1:2, :] * _shift_rows(x, 2, tail)
          + cw_ref[2:3, :] * _shift_rows(x, 1, tail) + cw_ref[3:4, :] * x + cb_ref[...])
    xc = _silu(xc)
    tail_ref[...] = x[tm - SUBLANES:tm, :]

    dt_all = lfdt_ref[0]
    dta_all = dt_all * a_ref[...]
    row = lax.broadcasted_iota(jnp.int32, (Q, Q), 0)
    colm = lax.broadcasted_iota(jnp.int32, (Q, Q), 1)
    causal = row >= colm
    tri = tri_ref[...]

    for c in range(tm // Q):
        rs = slice(c * Q, (c + 1) * Q)
        dt = dt_all[rs]
        acum = _exact_dot(tri, dta_all[rs])
        acum_t = acum.T
        dt_t = dt.T
        xs = xc[rs, 0:SSD_WIDTH]
        xs_t = xs.T.astype(bf16)
        xs_b = xs.astype(bf16)
        bm = xc[rs, SSD_WIDTH:SSD_WIDTH + SSD_GROUPS * D_STATE]
        cm = xc[rs, SSD_WIDTH + SSD_GROUPS * D_STATE:XBC_WIDTH].astype(bf16)
        cb = []
        for g in range(SSD_GROUPS):
            gs = slice(g * D_STATE, (g + 1) * D_STATE)
            cb.append(_dot_nt(cm[:, gs], bm[:, gs].astype(bf16)))
        for hh in range(SSD_HEADS):
            g = hh // (SSD_HEADS // SSD_GROUPS)
            gs = slice(g * D_STATE, (g + 1) * D_STATE)
            hs = slice(hh * SSD_HEAD_DIM, (hh + 1) * SSD_HEAD_DIM)
            cc = COL_DT + hh
            a_col = acum[:, cc:cc + 1]
            a_row = acum_t[cc:cc + 1, :]
            a_last = acum[Q - 1:Q, cc:cc + 1]
            decay = jnp.exp(jnp.where(causal, a_col - a_row, -jnp.inf))
            mat = cb[g] * decay * dt_t[cc:cc + 1, :]
            y_intra = jnp.dot(mat.astype(bf16), xs_b[:, hs], preferred_element_type=f32)
            s_in = s_ref[hh]
            y_inter = _dot_nt(cm[:, gs], s_in.astype(bf16)) * jnp.exp(a_col)
            wcol = jnp.exp(a_last - a_col) * dt[:, cc:cc + 1]
            bw = (bm[:, gs] * wcol).astype(bf16)
            s_ref[hh] = jnp.exp(a_last) * s_in + jnp.dot(xs_t[hs, :], bw, preferred_element_type=f32)
            ybuf_ref[:, hs] = y_intra + y_inter + dsk_ref[:, hs] * xs[:, hs]
        gated = ybuf_ref[...] * _silu(z_ref[0, rs, :])
        y_ref[0, rs, :] = _rms(gated, ng_ref[...]).astype(bf16)
    sout_ref[0] = s_ref[...]


def _prompt_ssd(xbc, lfdt, z, cw, cb, a_row, dsk, ng, *, tm):
    B, L, _ = xbc.shape
    tri = jnp.tril(jnp.ones((SSD_CHUNK, SSD_CHUNK), f32)).astype(bf16)
    row = lambda width: pl.BlockSpec((1, tm, width), lambda b, j: (b, j, 0))
    const = lambda shape: pl.BlockSpec(shape, lambda b, j: (0,) * len(shape))
    return pl.pallas_call(
        _prompt_ssd_body,
        out_shape=(jax.ShapeDtypeStruct((B, L, SSD_WIDTH), bf16),
                   jax.ShapeDtypeStruct((B, SSD_HEADS, SSD_HEAD_DIM, D_STATE), f32)),
        grid=(B, L // tm),
        in_specs=[row(XBC_WIDTH), row(LANES), row(SSD_WIDTH), const((SSD_CONV, XBC_WIDTH)),
                  const((1, XBC_WIDTH)), const((1, LANES)), const((1, SSD_WIDTH)), const((1, SSD_WIDTH)),
                  const((SSD_CHUNK, SSD_CHUNK))],
        out_specs=(row(SSD_WIDTH),
                   pl.BlockSpec((1, SSD_HEADS, SSD_HEAD_DIM, D_STATE), lambda b, j: (b, 0, 0, 0))),
        scratch_shapes=[pltpu.VMEM((SSD_HEADS, SSD_HEAD_DIM, D_STATE), f32),
                        pltpu.VMEM((SUBLANES, XBC_WIDTH), f32),
                        pltpu.VMEM((SSD_CHUNK, SSD_WIDTH), f32)],
        compiler_params=pltpu.CompilerParams(dimension_semantics=("arbitrary", "arbitrary"),
                                             vmem_limit_bytes=VMEM_LIMIT_BYTES),
        name="prompt_ssd",
    )(xbc, lfdt, z, cw, cb, a_row, dsk, ng, tri)


def _mlp_tail(x, mix, gpost, gpre, gmpost, wup_ref, wdn_ref, ff_chunk):
    x1 = x + _rms(mix, gpost)
    hmid = _rms(x1, gpre).astype(bf16)
    acc = jnp.zeros_like(x1)
    for c in range(D_FF // ff_chunk):
        cs = slice(c * ff_chunk, (c + 1) * ff_chunk)
        up = jnp.dot(hmid, wup_ref[:, cs], preferred_element_type=f32)
        act = jnp.square(jnp.maximum(up, 0.0)).astype(bf16)
        acc = acc + jnp.dot(act, wdn_ref[cs, :], preferred_element_type=f32)
    return x1 + _rms(acc, gmpost)


def _prompt_out_body(x_ref, att_ref, yssd_ref, yconv_ref, wo_ref, gpost_ref, gpre_ref, gmpost_ref,
                     wup_ref, wdn_ref, o_ref, *, ff_chunk):
    dot = functools.partial(jnp.dot, preferred_element_type=f32)
    mix = (dot(att_ref[...], wo_ref[0:ATT_WIDTH, :])
           + dot(yssd_ref[...], wo_ref[ATT_WIDTH:ATT_WIDTH + SSD_WIDTH, :])
           + dot(yconv_ref[...], wo_ref[ATT_WIDTH + SSD_WIDTH:D_MODEL, :]))
    o_ref[...] = _mlp_tail(x_ref[...], mix, gpost_ref[...], gpre_ref[...], gmpost_ref[...],
                           wup_ref, wdn_ref, ff_chunk)


def _prompt_out(x, att, yssd, yconv, wo, gpost, gpre, gmpost, wup, wdn, *, tm):
    M = x.shape[0]
    row = lambda width: pl.BlockSpec((tm, width), lambda i: (i, 0))
    const = lambda shape: pl.BlockSpec(shape, lambda i: (0, 0), pipeline_mode=pl.Buffered(1))
    return pl.pallas_call(
        functools.partial(_prompt_out_body, ff_chunk=512),
        out_shape=jax.ShapeDtypeStruct((M, D_MODEL), f32),
        grid=(M // tm,),
        in_specs=[row(D_MODEL), row(ATT_WIDTH), row(SSD_WIDTH), row(CONV_WIDTH),
                  const((D_MODEL, D_MODEL)), const((1, D_MODEL)), const((1, D_MODEL)), const((1, D_MODEL)),
                  const((D_MODEL, D_FF)), const((D_FF, D_MODEL))],
        out_specs=row(D_MODEL),
        compiler_params=pltpu.CompilerParams(dimension_semantics=("arbitrary",),
                                             vmem_limit_bytes=VMEM_LIMIT_BYTES),
        name="prompt_out_mlp",
    )(x, att, yssd, yconv, wo, gpost, gpre, gmpost, wup, wdn)


def _sample_in_body(x_ref, g_ref, w_ref, bias_ref, cst_ref, cw_ref, cb_ref, aexp_ref,
                    q_ref, k_ref, v_ref, lf_ref, z_ref, gates_ref, xs_ref, bm_ref, cm_ref,
                    dtxt_ref, dect_ref, cnew_ref):
    n = x_ref.shape[0]
    h = _rms(x_ref[...], g_ref[...]).astype(bf16)
    proj = _dot_nt(h, w_ref[...])
    q_ref[...] = proj[:, OFF_Q:OFF_K] * (HEAD_DIM ** -0.5)
    k_ref[...] = proj[:, OFF_K:OFF_V]
    v_ref[...] = proj[:, OFF_V:OFF_Z]
    z_ref[...] = proj[:, OFF_Z:OFF_XBC]
    gates_ref[...] = proj[:, OFF_GB:OFF_SMALL]
    logf, dt = _softplus_parts(proj[:, OFF_SMALL:PROJ_WIDTH] + bias_ref[...])
    lf_ref[...] = logf

    xbc = proj[:, OFF_XBC:OFF_GB]
    xc = (cw_ref[0:1, :] * cst_ref[0] + cw_ref[1:2, :] * cst_ref[1] + cw_ref[2:3, :] * cst_ref[2]
          + cw_ref[3:4, :] * xbc + cb_ref[...])
    xc = _silu(xc)
    cnew_ref[0] = cst_ref[1]
    cnew_ref[1] = cst_ref[2]
    cnew_ref[2] = xbc
    xs = xc[:, 0:SSD_WIDTH]
    xs_ref[...] = xs
    bm_ref[...] = xc[:, SSD_WIDTH:SSD_WIDTH + SSD_GROUPS * D_STATE]
    cm_ref[...] = xc[:, SSD_WIDTH + SSD_GROUPS * D_STATE:XBC_WIDTH]

    head = lax.broadcasted_iota(jnp.int32, (n, SSD_WIDTH), 1) // SSD_HEAD_DIM
    dt_exp = jnp.zeros((n, SSD_WIDTH), f32)
    for hh in range(SSD_HEADS):
        dt_exp = jnp.where(head == hh, dt[:, COL_DT + hh:COL_DT + hh + 1], dt_exp)
    pad = jnp.zeros((LANES - n, SSD_WIDTH), f32)
    dtxt_ref[...] = jnp.concatenate([dt_exp * xs, pad], axis=0).T
    dect_ref[...] = jnp.concatenate([jnp.exp(dt_exp * aexp_ref[...]), pad], axis=0).T


def _sample_in(x, g, w, bias, cst, cw, cb, aexp):
    n = x.shape[0]
    s = lambda *shape: jax.ShapeDtypeStruct(shape, f32)
    return pl.pallas_call(
        _sample_in_body,
        out_shape=(s(n, ATT_WIDTH), s(n, ATT_WIDTH), s(n, ATT_WIDTH), s(n, LANES), s(n, SSD_WIDTH),
                   s(n, 3 * CONV_WIDTH), s(n, SSD_WIDTH), s(n, SSD_GROUPS * D_STATE), s(n, SSD_GROUPS * D_STATE),
                   s(SSD_WIDTH, LANES), s(SSD_WIDTH, LANES), s(SSD_CONV - 1, n, XBC_WIDTH)),
        compiler_params=pltpu.CompilerParams(vmem_limit_bytes=VMEM_LIMIT_BYTES),
        name="sample_in",
    )(x, g, w, bias, cst, cw, cb, aexp)


def _sample_ssd_body(s_ref, dtxt_ref, dect_ref, bm_ref, cm_ref, snew_ref, yt_ref):
    b = pl.program_id(0)
    rows = SSD_WIDTH
    half = rows // SSD_GROUPS
    lane = lax.broadcasted_iota(jnp.int32, (rows, LANES), 1)
    sel = lane == b
    dcol = jnp.sum(jnp.where(sel, dect_ref[...], 0.0), axis=1, keepdims=True)
    xcol = jnp.sum(jnp.where(sel, dtxt_ref[...], 0.0), axis=1, keepdims=True)

    def expand(ref):
        r = ref[pl.ds(b, 1), :]
        return jnp.concatenate([jnp.broadcast_to(r[:, g * D_STATE:(g + 1) * D_STATE], (half, D_STATE))
                                for g in range(SSD_GROUPS)], axis=0)

    s_old = s_ref[0].reshape(rows, D_STATE)
    s_new = dcol * s_old + xcol * expand(bm_ref)
    snew_ref[0] = s_new.reshape(SSD_HEADS, SSD_HEAD_DIM, D_STATE)
    ycol = jnp.sum(s_new * expand(cm_ref), axis=1, keepdims=True)

    @pl.when(b == 0)
    def _():
        yt_ref[...] = jnp.zeros_like(yt_ref)

    yt_ref[...] += jnp.where(sel, ycol, 0.0)


def _sample_ssd(state, dtxt, dect, bm, cm):
    n = state.shape[0]
    const = lambda shape: pl.BlockSpec(shape, lambda b: (0, 0))
    sblk = pl.BlockSpec((1, SSD_HEADS, SSD_HEAD_DIM, D_STATE), lambda b: (b, 0, 0, 0))
    return pl.pallas_call(
        _sample_ssd_body,
        out_shape=(jax.ShapeDtypeStruct(state.shape, f32), jax.ShapeDtypeStruct((SSD_WIDTH, LANES), f32)),
        grid=(n,),
        in_specs=[sblk, const((SSD_WIDTH, LANES)), const((SSD_WIDTH, LANES)),
                  const(bm.shape), const(cm.shape)],
        out_specs=(sblk, const((SSD_WIDTH, LANES))),
        compiler_params=pltpu.CompilerParams(dimension_semantics=("arbitrary",)),
        name="sample_ssd",
    )(state, dtxt, dect, bm, cm)


def _page_bias_body(lf_ref, upper_ref, o_ref):
    pb = lf_ref.shape[1]
    for hh in range(ATT_HEADS):
        x = lf_ref[hh]
        o_ref[:, hh, 0:LANES] = _exact_dot_rhs(x, upper_ref[...])
        o_ref[:, hh, LANES:2 * LANES] = jnp.broadcast_to(jnp.sum(x, axis=1, keepdims=True), x.shape)
    o_ref[:, ATT_HEADS:SUBLANES, :] = jnp.zeros((pb, SUBLANES - ATT_HEADS, 2 * LANES), f32)


def _page_bias(lft):
    depth, _, pool, _ = lft.shape
    pb = _pick_tile(pool, 256)
    pos = jnp.arange(PAGE_SIZE)
    upper = (pos[:, None] > pos[None, :]).astype(bf16)
    return pl.pallas_call(
        _page_bias_body,
        out_shape=jax.ShapeDtypeStruct((depth, pool, SUBLANES, 2 * LANES), f32),
        grid=(depth, pool // pb),
        in_specs=[pl.BlockSpec((None, ATT_HEADS, pb, PAGE_SIZE), lambda d, i: (d, 0, i, 0)),
                  pl.BlockSpec((PAGE_SIZE, PAGE_SIZE), lambda d, i: (0, 0))],
        out_specs=pl.BlockSpec((None, pb, SUBLANES, 2 * LANES), lambda d, i: (d, i, 0, 0)),
        compiler_params=pltpu.CompilerParams(dimension_semantics=("arbitrary", "arbitrary")),
        name="page_bias",
    )(lft, upper)


def _sample_attn_body(pt_ref, qc_ref, knc_ref, vnc_ref, lfn_ref, *rest, pps):
    k_refs = rest[0:pps]
    v_refs = rest[pps:2 * pps]
    sb_refs = rest[2 * pps:3 * pps]
    o_ref, m_ref, l_ref, acc_ref, r_ref = rest[3 * pps:]
    j = pl.program_id(1)
    hrow = lax.broadcasted_iota(jnp.int32, (SUBLANES, LANES), 0)
    heads = [slice(HEAD_DIM * hh, HEAD_DIM * (hh + 1)) for hh in range(ATT_HEADS)]

    def scores(k_of_head):
        s = jnp.zeros((SUBLANES, LANES), f32)
        for hh in range(ATT_HEADS):
            prod = qc_ref[0, heads[hh], :] * k_of_head(hh)
            s = jnp.where(hrow == hh, jnp.sum(prod, axis=0, keepdims=True), s)
        return s

    @pl.when(j == 0)
    def _():
        m_ref[...] = jnp.full(m_ref.shape, NEG_BIG, f32)
        l_ref[...] = jnp.zeros_like(l_ref)
        acc_ref[...] = jnp.zeros_like(acc_ref)
        r_ref[...] = lfn_ref[0]

    for i in range(pps):
        sb = sb_refs[i][...]
        r = r_ref[...]
        s = scores(lambda hh: k_refs[i][hh]) + (r + sb[:, 0:LANES])
        r_ref[...] = r + sb[:, LANES:2 * LANES]
        m_old = m_ref[...]
        m_new = jnp.maximum(m_old, s)
        alpha = jnp.exp(m_old - m_new)
        p = jnp.exp(s - m_new)
        l_ref[...] = alpha * l_ref[...] + p
        m_ref[...] = m_new
        for hh in range(ATT_HEADS):
            acc_ref[hh] = alpha[hh:hh + 1, :] * acc_ref[hh] + p[hh:hh + 1, :] * v_refs[i][hh]

    @pl.when(j == pl.num_programs(1) - 1)
    def _():
        m = m_ref[...]
        s_new = scores(lambda hh: knc_ref[0, heads[hh], :])
        mx = jnp.maximum(jnp.max(m, axis=1, keepdims=True), s_new)
        w = jnp.exp(m - mx)
        w_new = jnp.exp(s_new - mx)
        denom = jnp.sum(l_ref[...] * w, axis=1, keepdims=True) + w_new
        for hh in range(ATT_HEADS):
            num = (jnp.sum(acc_ref[hh] * w[hh:hh + 1, :], axis=1, keepdims=True)
                   + w_new[hh:hh + 1, :] * vnc_ref[0, heads[hh], :])
            o_ref[0, heads[hh], :] = num / denom[hh:hh + 1, :]


def _sample_attn(page_table, qc, knc, vnc, lfn, kt, vt, sb, layer):
    n, n_pages = page_table.shape
    pps = min(PAGES_PER_STEP, n_pages)
    assert n_pages % pps == 0
    pt = page_table.reshape(-1)
    seq = lambda rows: pl.BlockSpec((1, rows, LANES), lambda b, j, pt: (b, 0, 0))

    def page(b, j, pt, i):
        return pt[b * n_pages + n_pages - 1 - (j * pps + i)]

    kv_specs = [pl.BlockSpec((None, None, ATT_HEADS, HEAD_DIM, PAGE_SIZE),
                             lambda b, j, pt, i=i: (layer, page(b, j, pt, i), 0, 0, 0)) for i in range(pps)]
    sb_specs = [pl.BlockSpec((None, None, SUBLANES, 2 * LANES),
                             lambda b, j, pt, i=i: (layer, page(b, j, pt, i), 0, 0)) for i in range(pps)]
    return pl.pallas_call(
        functools.partial(_sample_attn_body, pps=pps),
        out_shape=jax.ShapeDtypeStruct((n, ATT_WIDTH, LANES), f32),
        grid_spec=pltpu.PrefetchScalarGridSpec(
            num_scalar_prefetch=1,
            grid=(n, n_pages // pps),
            in_specs=[seq(ATT_WIDTH), seq(ATT_WIDTH), seq(ATT_WIDTH), seq(SUBLANES)]
            + kv_specs + kv_specs + sb_specs,
            out_specs=seq(ATT_WIDTH),
            scratch_shapes=[pltpu.VMEM((SUBLANES, LANES), f32), pltpu.VMEM((SUBLANES, LANES), f32),
                            pltpu.VMEM((ATT_HEADS, HEAD_DIM, LANES), f32), pltpu.VMEM((SUBLANES, LANES), f32)],
        ),
        compiler_params=pltpu.CompilerParams(dimension_semantics=("arbitrary", "arbitrary"),
                                             vmem_limit_bytes=VMEM_LIMIT_BYTES),
        name="sample_attn",
    )(pt, qc, knc, vnc, lfn, *([kt] * pps), *([vt] * pps), *([sb] * pps))


def _sample_out_body(x_ref, att_ref, yt_ref, xs_ref, z_ref, gates_ref, sst_ref, scw_ref, dsk_ref, ng_ref,
                     wo_ref, gpost_ref, gpre_ref, gmpost_ref, wup_ref, wdn_ref, o_ref, snew_ref, *, ff_chunk):
    n = x_ref.shape[0]
    y = yt_ref[...].T[0:n, :] + dsk_ref[...] * xs_ref[...]
    yssd = _rms(y * _silu(z_ref[...]), ng_ref[...])
    gates = gates_ref[...]
    u = gates[:, CONV_WIDTH:2 * CONV_WIDTH] * gates[:, 2 * CONV_WIDTH:3 * CONV_WIDTH]
    uc = scw_ref[0:1, :] * sst_ref[0] + scw_ref[1:2, :] * sst_ref[1] + scw_ref[2:3, :] * u
    yconv = gates[:, 0:CONV_WIDTH] * uc
    snew_ref[0] = sst_ref[1]
    snew_ref[1] = u
    dot = functools.partial(jnp.dot, preferred_element_type=f32)
    mix = (dot(att_ref[...].astype(bf16), wo_ref[0:ATT_WIDTH, :])
           + dot(yssd.astype(bf16), wo_ref[ATT_WIDTH:ATT_WIDTH + SSD_WIDTH, :])
           + dot(yconv.astype(bf16), wo_ref[ATT_WIDTH + SSD_WIDTH:D_MODEL, :]))
    o_ref[...] = _mlp_tail(x_ref[...], mix, gpost_ref[...], gpre_ref[...], gmpost_ref[...],
                           wup_ref, wdn_ref, ff_chunk)


def _sample_out(x, att, yt, xs, z, gates, sst, scw, dsk, ng, wo, gpost, gpre, gmpost, wup, wdn):
    n = x.shape[0]
    return pl.pallas_call(
        functools.partial(_sample_out_body, ff_chunk=512),
        out_shape=(jax.ShapeDtypeStruct((n, D_MODEL), f32),
                   jax.ShapeDtypeStruct((SCONV - 1, n, CONV_WIDTH), f32)),
        compiler_params=pltpu.CompilerParams(vmem_limit_bytes=VMEM_LIMIT_BYTES),
        name="sample_out_mlp",
    )(x, att, yt, xs, z, gates, sst, scw, dsk, ng, wo, gpost, gpre, gmpost, wup, wdn)


def _prep_w_in(wt):
    offs = [0]
    for s in IN_SIZES:
        offs.append(offs[-1] + s)
    q, k, v, f, z, xbc, dt, gb, gc, hc = [wt[offs[i]:offs[i + 1]] for i in range(len(IN_SIZES))]
    small = jnp.concatenate([f, dt, jnp.zeros((LANES - ATT_HEADS - SSD_HEADS, wt.shape[1]), wt.dtype)], axis=0)
    return jnp.concatenate([q, k, v, z, xbc, gb, gc, hc, small], axis=0).astype(bf16)


def kernel(x_prompt, x_sample, cache_k, cache_v, cache_logf, state_ssm, state_ssd_conv, state_sconv, page_table,
           w_in, b_f, ssd_conv_w, ssd_conv_b, dt_bias, a_log, d_skip, ssd_norm_g, sconv_w, w_out,
           g_mix_pre, g_mix_post, g_mlp_pre, g_mlp_post, w_mlp_up, w_mlp_down):
    depth = w_in.shape[0]
    B, L, _ = x_prompt.shape
    n = x_sample.shape[0]
    assert x_sample.shape[1] == 1 and n <= LANES
    tm = _pick_tile(L, 512)
    tq = _pick_tile(L, 256)
    tk = _pick_tile(L, 512)

    kt_cache = jnp.transpose(cache_k, (0, 1, 3, 4, 2))
    vt_cache = jnp.transpose(cache_v, (0, 1, 3, 4, 2))
    page_terms = _page_bias(jnp.transpose(cache_logf, (0, 3, 1, 2)))

    w_in_t = jnp.transpose(w_in, (0, 2, 1))

    xp = x_prompt
    xs = x_sample.reshape(n, D_MODEL)
    outs_p = [[] for _ in range(6)]
    outs_s = [[] for _ in range(6)]
    row = lambda a: a.reshape(1, -1)
    lanes_bcast = lambda a: jnp.broadcast_to(a[:, :, None], a.shape + (LANES,))

    for l in range(depth):
        w = _prep_w_in(w_in_t[l])
        wo = w_out[l].astype(bf16)
        wup = w_mlp_up[l].astype(bf16)
        wdn = w_mlp_down[l].astype(bf16)
        bias = jnp.concatenate([b_f[l], dt_bias[l], jnp.zeros((LANES - ATT_HEADS - SSD_HEADS,), f32)]).reshape(1, LANES)
        a_neg = -jnp.exp(a_log[l])
        a_row = jnp.concatenate([jnp.zeros((COL_DT,), f32), a_neg,
                                 jnp.zeros((LANES - COL_DT - SSD_HEADS,), f32)]).reshape(1, LANES)
        a_exp = jnp.repeat(a_neg, SSD_HEAD_DIM).reshape(1, SSD_WIDTH)
        dsk = jnp.repeat(d_skip[l], SSD_HEAD_DIM).reshape(1, SSD_WIDTH)
        ng = row(ssd_norm_g[l])
        cw, cb, scw = ssd_conv_w[l], row(ssd_conv_b[l]), sconv_w[l]
        gpre, gpost, gmpre, gmpost = row(g_mix_pre[l]), row(g_mix_post[l]), row(g_mlp_pre[l]), row(g_mlp_post[l])

        (qtp, kp, vtp, kt32, vt32, z, xbc, yconv, lfdt, lft, utail) = _prompt_in(
            xp, gpre, w, bias, scw, tm=tm, tq=tq, tk=tk)
        att = _prompt_attn(qtp, kp, vtp, tq=tq, tk=tk)
        yssd, s_fin = _prompt_ssd(xbc, lfdt, z, cw, cb, a_row, dsk, ng, tm=tm)
        xp = _prompt_out(xp.reshape(B * L, D_MODEL), att.reshape(B * L, ATT_WIDTH), yssd.reshape(B * L, SSD_WIDTH),
                         yconv.reshape(B * L, CONV_WIDTH), wo, gpost, gmpre, gmpost, wup, wdn,
                         tm=tm).reshape(B, L, D_MODEL)
        outs_p[0].append(kt32)
        outs_p[1].append(vt32)
        outs_p[2].append(lft[:, 0:ATT_HEADS, :])
        outs_p[3].append(s_fin)
        outs_p[4].append(xbc[:, L - (SSD_CONV - 1):, :])
        outs_p[5].append(utail[:, SUBLANES - (SCONV - 1):, :])

        cst = jnp.transpose(state_ssd_conv[l], (1, 0, 2))
        sst = jnp.transpose(state_sconv[l], (1, 0, 2))
        (q, k, v, lf, z_s, gates, xs_s, bm, cm, dtxt, dect, cnew) = _sample_in(xs, gpre, w, bias, cst, cw, cb, a_exp)
        s_new, yt = _sample_ssd(state_ssm[l], dtxt, dect, bm, cm)
        lf6 = lf[:, COL_LOGF:COL_LOGF + ATT_HEADS]
        lfn = lanes_bcast(jnp.pad(lf6, ((0, 0), (0, SUBLANES - ATT_HEADS))))
        att_s = _sample_attn(page_table, lanes_bcast(q), lanes_bcast(k), lanes_bcast(v), lfn,
                             kt_cache, vt_cache, page_terms, l)[:, :, 0]
        xs, snew = _sample_out(xs, att_s, yt, xs_s, z_s, gates, sst, scw, dsk, ng, wo, gpost, gmpre, gmpost, wup, wdn)
        outs_s[0].append(k.reshape(n, 1, ATT_HEADS, HEAD_DIM))
        outs_s[1].append(v.reshape(n, 1, ATT_HEADS, HEAD_DIM))
        outs_s[2].append(lf6.reshape(n, 1, ATT_HEADS))
        outs_s[3].append(s_new)
        outs_s[4].append(jnp.transpose(cnew, (1, 0, 2)))
        outs_s[5].append(jnp.transpose(snew, (1, 0, 2)))

    heads_last = lambda a: jnp.transpose(jnp.stack(a).reshape(depth, B, ATT_HEADS, HEAD_DIM, L), (0, 1, 4, 2, 3))
    k_p = heads_last(outs_p[0])
    v_p = heads_last(outs_p[1])
    lf_p = jnp.transpose(jnp.stack(outs_p[2]), (0, 1, 3, 2))
    rest_p = [jnp.stack(a) for a in outs_p[3:]]
    stacked_s = [jnp.stack(a) for a in outs_s]
    return (xp, xs.reshape(n, 1, D_MODEL), k_p, v_p, lf_p, *rest_p, *stacked_s)
```

```python
import functools

import jax
import jax.numpy as jnp
from jax import lax
from jax.experimental import pallas as pl
from jax.experimental.pallas import tpu as pltpu

f32 = jnp.float32
bf16 = jnp.bfloat16

D_MODEL = 1024
HEAD_DIM = 64
ATT_HEADS = 6
ATT_WIDTH = ATT_HEADS * HEAD_DIM
SSD_HEADS = 6
SSD_HEAD_DIM = 64
SSD_WIDTH = SSD_HEADS * SSD_HEAD_DIM
SSD_GROUPS = 2
D_STATE = 64
SSD_CONV = 4
XBC_WIDTH = SSD_WIDTH + 2 * SSD_GROUPS * D_STATE
CONV_WIDTH = 256
SCONV = 3
D_FF = 4 * D_MODEL
SSD_CHUNK = 128
PAGE_SIZE = 128
PAGES_PER_STEP = 16
RMS_EPS = 1e-6
IN_SIZES = (ATT_WIDTH, ATT_WIDTH, ATT_WIDTH, ATT_HEADS, SSD_WIDTH, XBC_WIDTH, SSD_HEADS,
            CONV_WIDTH, CONV_WIDTH, CONV_WIDTH)

LANES = 128
SUBLANES = 8
VMEM_LIMIT_BYTES = 56 * 1024 * 1024

OFF_Q = 0
OFF_K = OFF_Q + ATT_WIDTH
OFF_V = OFF_K + ATT_WIDTH
OFF_Z = OFF_V + ATT_WIDTH
OFF_XBC = OFF_Z + SSD_WIDTH
OFF_GB = OFF_XBC + XBC_WIDTH
OFF_GC = OFF_GB + CONV_WIDTH
OFF_HC = OFF_GC + CONV_WIDTH
OFF_SMALL = OFF_HC + CONV_WIDTH
PROJ_WIDTH = OFF_SMALL + LANES
COL_LOGF = 0
COL_DT = ATT_HEADS
PADDED_HEAD = LANES
NEG_BIG = -1e30
LOG2E = 1.4426950408889634
VALUE_ROWS = 80


def _rms(x, g):
    var = jnp.mean(x * x, axis=-1, keepdims=True)
    return (x * lax.rsqrt(var + RMS_EPS)) * g


def _silu(x):
    return x * (1.0 / (1.0 + jnp.exp(-x)))


def _split3(a):
    a1 = a.astype(bf16)
    r1 = a - a1.astype(f32)
    a2 = r1.astype(bf16)
    a3 = (r1 - a2.astype(f32)).astype(bf16)
    return a1, a2, a3


def _exact_dot(m01, a):
    a1, a2, a3 = _split3(a)
    d = functools.partial(jnp.dot, preferred_element_type=f32)
    return d(m01, a1) + d(m01, a2) + d(m01, a3)


def _exact_dot_rhs(a, m01):
    a1, a2, a3 = _split3(a)
    d = functools.partial(jnp.dot, preferred_element_type=f32)
    return d(a1, m01) + d(a2, m01) + d(a3, m01)


def _dot_nt(a, b):
    return lax.dot_general(a, b, (((1,), (1,)), ((), ())), preferred_element_type=f32)


def _softplus_parts(t):
    sp = jnp.log1p(jnp.exp(-jnp.abs(t)))
    return jnp.minimum(t, 0.0) - sp, jnp.maximum(t, 0.0) + sp


def _shift_rows(x, k, tail):
    row = lax.broadcasted_iota(jnp.int32, x.shape, 0)
    y = pltpu.roll(x, k, 0)
    for r in range(k):
        y = jnp.where(row == r, tail[SUBLANES - k + r:SUBLANES - k + r + 1, :], y)
    return y


def _pick_tile(n, pref):
    t = min(n, pref)
    while n % t:
        t -= SUBLANES
    assert t > 0
    return t


def _prompt_in_body(x_ref, g_ref, w_ref, bias_ref, tri_ref, scw_ref, *refs, tq, tk, carried):
    _prompt_in_compute(x_ref, g_ref, w_ref, bias_ref, tri_ref, scw_ref, *refs[carried:], tq=tq, tk=tk)


def _prompt_in_compute(x_ref, g_ref, w_ref, bias_ref, tri_ref, scw_ref,
                    qtp_ref, kp_ref, vtp_ref, kt32_ref, vt32_ref, z_ref, xbc_ref, yconv_ref,
                    lfdt_ref, lft_ref, utail_ref,
                    carry_ref, tail_ref, *, tq, tk):
    tm = x_ref.shape[1]

    @pl.when(pl.program_id(1) == 0)
    def _():
        carry_ref[...] = jnp.zeros_like(carry_ref)
        tail_ref[...] = jnp.zeros_like(tail_ref)

    h = _rms(x_ref[0], g_ref[...]).astype(bf16)
    proj = lambda lo, hi: _dot_nt(h, w_ref[lo:hi, :])

    t = proj(OFF_SMALL, PROJ_WIDTH) + bias_ref[...]
    logf, dt = _softplus_parts(t)
    col = lax.broadcasted_iota(jnp.int32, t.shape, 1)
    lfdt_ref[0] = jnp.where(col < COL_DT, logf, dt)
    lft_ref[0] = logf.T[0:SUBLANES, :]
    c = _exact_dot(tri_ref[...], logf) + carry_ref[...]
    carry_ref[...] = c[tm - 1:tm, :]
    c = c * LOG2E
    c1, c2, c3 = [p.astype(f32) for p in _split3(c)]
    ct1, ct2, ct3 = [p.astype(f32) for p in _split3(c.T[0:SUBLANES, :])]

    qkv_t = _dot_nt(w_ref[OFF_Q:OFF_Z, :], h)
    kt = qkv_t[ATT_WIDTH:2 * ATT_WIDTH]
    vt = qkv_t[2 * ATT_WIDTH:3 * ATT_WIDTH]
    kt32_ref[0] = kt
    vt32_ref[0] = vt
    k = kt.T
    lane = lax.broadcasted_iota(jnp.int32, (tm, PADDED_HEAD - HEAD_DIM), 1)
    row = lax.broadcasted_iota(jnp.int32, (PADDED_HEAD - HEAD_DIM, tm), 0)
    vrow = lax.broadcasted_iota(jnp.int32, (VALUE_ROWS - HEAD_DIM, tm), 0)
    ones_row = jnp.where(vrow == 0, 1.0, 0.0).astype(bf16)
    for hh in range(ATT_HEADS):
        src = slice(HEAD_DIM * hh, HEAD_DIM * (hh + 1))
        lo = slice(PADDED_HEAD * hh, PADDED_HEAD * hh + HEAD_DIM)
        hi = slice(PADDED_HEAD * hh + HEAD_DIM, PADDED_HEAD * (hh + 1))
        hc = slice(hh, hh + 1)
        kp_ref[0, :, lo] = k[:, src].astype(bf16)
        ek = jnp.where(lane < 3, 1.0, jnp.where(lane == 3, -c1[:, hc], jnp.where(
            lane == 4, -c2[:, hc], jnp.where(lane == 5, -c3[:, hc], 0.0))))
        kp_ref[0, :, hi] = ek.astype(bf16)
        eq = jnp.where(row == 0, ct1[hc, :], jnp.where(row == 1, ct2[hc, :], jnp.where(
            row == 2, ct3[hc, :], jnp.where(row < 6, 1.0, 0.0)))).astype(bf16)
        qh = (qkv_t[src] * (HEAD_DIM ** -0.5 * LOG2E)).astype(bf16)
        for i in range(tm // tq):
            cs = slice(i * tq, (i + 1) * tq)
            qtp_ref[0, i, lo, :] = qh[:, cs]
            qtp_ref[0, i, hi, :] = eq[:, cs]
        vh = vt[src].astype(bf16)
        for i in range(tm // tk):
            cs = slice(i * tk, (i + 1) * tk)
            vtp_ref[0, i, VALUE_ROWS * hh:VALUE_ROWS * hh + HEAD_DIM, :] = vh[:, cs]
            vtp_ref[0, i, VALUE_ROWS * hh + HEAD_DIM:VALUE_ROWS * (hh + 1), :] = ones_row[:, cs]

    z_ref[0] = proj(OFF_Z, OFF_XBC)
    xbc_ref[0] = proj(OFF_XBC, OFF_GB)

    gates = proj(OFF_GB, OFF_SMALL)
    gb = gates[:, 0:CONV_WIDTH]
    u = gates[:, CONV_WIDTH:2 * CONV_WIDTH] * gates[:, 2 * CONV_WIDTH:3 * CONV_WIDTH]
    tail = tail_ref[...]
    uc = (scw_ref[0:1, :] * _shift_rows(u, 2, tail) + scw_ref[1:2, :] * _shift_rows(u, 1, tail)
          + scw_ref[2:3, :] * u)
    yconv_ref[0] = (gb * uc).astype(bf16)
    tail_ref[...] = u[tm - SUBLANES:tm, :]
    utail_ref[0] = u[tm - SUBLANES:tm, :]


def _prompt_in(x, g, w, bias, scw, kv_all, layer, depth, *, tm, tq, tk):
    B, L, _ = x.shape
    nj = L // tm
    pw = ATT_HEADS * PADDED_HEAD
    tri = jnp.tril(jnp.ones((tm, tm), f32)).astype(bf16)
    row = lambda width: pl.BlockSpec((1, tm, width), lambda b, j: (b, j, 0))
    colb = lambda height: pl.BlockSpec((1, height, tm), lambda b, j: (b, 0, j))
    const = lambda shape: pl.BlockSpec(shape, lambda b, j: (0,) * len(shape))
    layer_colb = pl.BlockSpec((None, 1, ATT_WIDTH, tm), lambda b, j: (layer, b, 0, j))
    carried = () if kv_all is None else tuple(kv_all)
    out_shapes = (
        jax.ShapeDtypeStruct((B, L // tq, pw, tq), bf16),
        jax.ShapeDtypeStruct((B, L, pw), bf16),
        jax.ShapeDtypeStruct((B, L // tk, ATT_HEADS * VALUE_ROWS, tk), bf16),
        jax.ShapeDtypeStruct((depth, B, ATT_WIDTH, L), f32),
        jax.ShapeDtypeStruct((depth, B, ATT_WIDTH, L), f32),
        jax.ShapeDtypeStruct((B, L, SSD_WIDTH), f32),
        jax.ShapeDtypeStruct((B, L, XBC_WIDTH), f32),
        jax.ShapeDtypeStruct((B, L, CONV_WIDTH), bf16),
        jax.ShapeDtypeStruct((B, L, LANES), f32),
        jax.ShapeDtypeStruct((B, SUBLANES, L), f32),
        jax.ShapeDtypeStruct((B, SUBLANES, CONV_WIDTH), f32),
    )
    out_specs = (
        pl.BlockSpec((1, tm // tq, pw, tq), lambda b, j: (b, j, 0, 0)),
        row(pw),
        pl.BlockSpec((1, tm // tk, ATT_HEADS * VALUE_ROWS, tk), lambda b, j: (b, j, 0, 0)),
        layer_colb, layer_colb, row(SSD_WIDTH), row(XBC_WIDTH), row(CONV_WIDTH),
        row(LANES), colb(SUBLANES),
        pl.BlockSpec((1, SUBLANES, CONV_WIDTH), lambda b, j: (b, 0, 0)),
    )
    return pl.pallas_call(
        functools.partial(_prompt_in_body, tq=tq, tk=tk, carried=len(carried)),
        out_shape=out_shapes,
        grid=(B, nj),
        in_specs=[row(D_MODEL), const((1, D_MODEL)), const((PROJ_WIDTH, D_MODEL)), const((1, LANES)),
                  const((tm, tm)), const((SCONV, CONV_WIDTH))] + [pl.BlockSpec(memory_space=pl.ANY)] * len(carried),
        input_output_aliases={6 + i: 3 + i for i in range(len(carried))},
        out_specs=out_specs,
        scratch_shapes=[pltpu.VMEM((1, LANES), f32), pltpu.VMEM((SUBLANES, CONV_WIDTH), f32)],
        compiler_params=pltpu.CompilerParams(dimension_semantics=("arbitrary", "arbitrary"),
                                             vmem_limit_bytes=VMEM_LIMIT_BYTES),
        name="prompt_in",
    )(x, g, w, bias, tri, scw, *carried)


def _prompt_attn_body(qt_ref, kp_ref, vt_ref, o_ref, m_ref, acc_ref, st_ref, *, tq, tk):
    qi = pl.program_id(1)
    m_ref[...] = jnp.full(m_ref.shape, NEG_BIG, f32)
    acc_ref[...] = jnp.zeros_like(acc_ref)
    nfull = lax.div(qi * tq, tk)
    heads = [slice(PADDED_HEAD * hh, PADDED_HEAD * (hh + 1)) for hh in range(ATT_HEADS)]
    vrows = [slice(VALUE_ROWS * hh, VALUE_ROWS * (hh + 1)) for hh in range(ATT_HEADS)]

    def score(kj, slot, masked):
        ks = pl.multiple_of(kj * tk, tk)
        if masked:
            kpos = ks + lax.broadcasted_iota(jnp.int32, (tk, tq), 0)
            qpos = qi * tq + lax.broadcasted_iota(jnp.int32, (tk, tq), 1)
            valid = kpos <= qpos
        for hh, hs in enumerate(heads):
            st = jnp.dot(kp_ref[0, pl.ds(ks, tk), hs], qt_ref[0, 0, hs, :], preferred_element_type=f32)
            st_ref[slot, hh] = jnp.where(valid, st, NEG_BIG) if masked else st

    def absorb(kj, slot):
        probs = []
        for hh in range(ATT_HEADS):
            st = st_ref[slot, hh]
            m_old = m_ref[hh]
            m_new = jnp.maximum(m_old, jnp.max(st, axis=0, keepdims=True))
            m_ref[hh] = m_new
            probs.append((jnp.exp2(m_old[0:1] - m_new[0:1]), jnp.exp2(st - m_new[0:1]).astype(bf16)))
        for hh, (alpha, pt) in enumerate(probs):
            acc_ref[hh] = alpha * acc_ref[hh] + jnp.dot(vt_ref[0, kj, vrows[hh], :], pt,
                                                        preferred_element_type=f32)

    @pl.when(nfull > 0)
    def _():
        score(0, 0, False)

    def pair(pi, carry):
        k0 = 2 * pi
        score(k0 + 1, 1, False)
        absorb(k0, 0)
        score(k0 + 2, 0, False)
        absorb(k0 + 1, 1)
        return carry

    ntrip = lax.div(jnp.maximum(nfull - 1, 0), 2)
    lax.fori_loop(0, ntrip, pair, 0)
    done = 2 * ntrip
    rem = nfull - done

    @pl.when(rem == 2)
    def _():
        score(done + 1, 1, False)
        absorb(done, 0)
        score(done + 2, 0, True)
        absorb(done + 1, 1)
        absorb(done + 2, 0)

    @pl.when(rem == 1)
    def _():
        score(done + 1, 1, True)
        absorb(done, 0)
        absorb(done + 1, 1)

    @pl.when(rem == 0)
    def _():
        score(0, 0, True)
        absorb(0, 0)

    pad = jnp.zeros((PADDED_HEAD - VALUE_ROWS, tq), f32)
    for hh in range(ATT_HEADS):
        a = jnp.concatenate([acc_ref[hh], pad], axis=0).T
        o_ref[0, :, HEAD_DIM * hh:HEAD_DIM * (hh + 1)] = (
            a[:, 0:HEAD_DIM] / a[:, HEAD_DIM:HEAD_DIM + 1]).astype(bf16)


def _prompt_attn(qtp, kp, vtp, *, tq, tk):
    B, nq, W, _ = qtp.shape
    L = kp.shape[1]
    vw = ATT_HEADS * VALUE_ROWS
    return pl.pallas_call(
        functools.partial(_prompt_attn_body, tq=tq, tk=tk),
        out_shape=jax.ShapeDtypeStruct((B, L, ATT_WIDTH), bf16),
        grid=(B, nq),
        in_specs=[pl.BlockSpec((1, 1, W, tq), lambda b, i: (b, i, 0, 0)),
                  pl.BlockSpec((1, L, W), lambda b, i: (b, 0, 0)),
                  pl.BlockSpec((1, L // tk, vw, tk), lambda b, i: (b, 0, 0, 0))],
        out_specs=pl.BlockSpec((1, tq, ATT_WIDTH), lambda b, i: (b, i, 0)),
        scratch_shapes=[pltpu.VMEM((ATT_HEADS, SUBLANES, tq), f32),
                        pltpu.VMEM((ATT_HEADS, VALUE_ROWS, tq), f32),
                        pltpu.VMEM((2, ATT_HEADS, tk, tq), f32)],
        compiler_params=pltpu.CompilerParams(dimension_semantics=("arbitrary", "arbitrary"),
                                             vmem_limit_bytes=VMEM_LIMIT_BYTES),
        name="prompt_attn",
    )(qtp, kp, vtp)


def _prompt_ssd_body(xbc_ref, lfdt_ref, z_ref, cw_ref, cb_ref, a_ref, dsk_ref, ng_ref, tri_ref,
                     y_ref, sout_ref, s_ref, tail_ref, ybuf_ref):
    tm = xbc_ref.shape[1]
    Q = SSD_CHUNK

    @pl.when(pl.program_id(1) == 0)
    def _():
        s_ref[...] = jnp.zeros_like(s_ref)
        tail_ref[...] = jnp.zeros_like(tail_ref)

    x = xbc_ref[0]
    tail = tail_ref[...]
    xc = (cw_ref[0:1, :] * _shift_rows(x, 3, tail) + cw_ref[1:2, :] * _shift_rows(x, 2, tail)
          + cw_ref[2:3, :] * _shift_rows(x, 1, tail) + cw_ref[3:4, :] * x + cb_ref[...])
    xc = _silu(xc)
    tail_ref[...] = x[tm - SUBLANES:tm, :]

    dt_all = lfdt_ref[0]
    dta_all = dt_all * a_ref[...]
    row = lax.broadcasted_iota(jnp.int32, (Q, Q), 0)
    colm = lax.broadcasted_iota(jnp.int32, (Q, Q), 1)
    causal = row >= colm
    tri = tri_ref[...]

    for c in range(tm // Q):
        rs = slice(c * Q, (c + 1) * Q)
        dt = dt_all[rs]
        acum = _exact_dot(tri, dta_all[rs])
        acum_t = acum.T
        dt_t = dt.T
        xs = xc[rs, 0:SSD_WIDTH]
        xs_t = xs.T.astype(bf16)
        xs_b = xs.astype(bf16)
        bm = xc[rs, SSD_WIDTH:SSD_WIDTH + SSD_GROUPS * D_STATE]
        cm = xc[rs, SSD_WIDTH + SSD_GROUPS * D_STATE:XBC_WIDTH].astype(bf16)
        cb = []
        for g in range(SSD_GROUPS):
            gs = slice(g * D_STATE, (g + 1) * D_STATE)
            cb.append(_dot_nt(cm[:, gs], bm[:, gs].astype(bf16)))
        for hh in range(SSD_HEADS):
            g = hh // (SSD_HEADS // SSD_GROUPS)
            gs = slice(g * D_STATE, (g + 1) * D_STATE)
            hs = slice(hh * SSD_HEAD_DIM, (hh + 1) * SSD_HEAD_DIM)
            cc = COL_DT + hh
            a_col = acum[:, cc:cc + 1]
            a_row = acum_t[cc:cc + 1, :]
            a_last = acum[Q - 1:Q, cc:cc + 1]
            decay = jnp.exp(jnp.where(causal, a_col - a_row, -jnp.inf))
            mat = cb[g] * decay * dt_t[cc:cc + 1, :]
            y_intra = jnp.dot(mat.astype(bf16), xs_b[:, hs], preferred_element_type=f32)
            s_in = s_ref[hh]
            y_inter = _dot_nt(cm[:, gs], s_in.astype(bf16)) * jnp.exp(a_col)
            wcol = jnp.exp(a_last - a_col) * dt[:, cc:cc + 1]
            bw = (bm[:, gs] * wcol).astype(bf16)
            s_ref[hh] = jnp.exp(a_last) * s_in + jnp.dot(xs_t[hs, :], bw, preferred_element_type=f32)
            ybuf_ref[:, hs] = y_intra + y_inter + dsk_ref[:, hs] * xs[:, hs]
        gated = ybuf_ref[...] * _silu(z_ref[0, rs, :])
        y_ref[0, rs, :] = _rms(gated, ng_ref[...]).astype(bf16)
    sout_ref[0] = s_ref[...]


def _prompt_ssd(xbc, lfdt, z, cw, cb, a_row, dsk, ng, *, tm):
    B, L, _ = xbc.shape
    tri = jnp.tril(jnp.ones((SSD_CHUNK, SSD_CHUNK), f32)).astype(bf16)
    row = lambda width: pl.BlockSpec((1, tm, width), lambda b, j: (b, j, 0))
    const = lambda shape: pl.BlockSpec(shape, lambda b, j: (0,) * len(shape))
    return pl.pallas_call(
        _prompt_ssd_body,
        out_shape=(jax.ShapeDtypeStruct((B, L, SSD_WIDTH), bf16),
                   jax.ShapeDtypeStruct((B, SSD_HEADS, SSD_HEAD_DIM, D_STATE), f32)),
        grid=(B, L // tm),
        in_specs=[row(XBC_WIDTH), row(LANES), row(SSD_WIDTH), const((SSD_CONV, XBC_WIDTH)),
                  const((1, XBC_WIDTH)), const((1, LANES)), const((1, SSD_WIDTH)), const((1, SSD_WIDTH)),
                  const((SSD_CHUNK, SSD_CHUNK))],
        out_specs=(row(SSD_WIDTH),
                   pl.BlockSpec((1, SSD_HEADS, SSD_HEAD_DIM, D_STATE), lambda b, j: (b, 0, 0, 0))),
        scratch_shapes=[pltpu.VMEM((SSD_HEADS, SSD_HEAD_DIM, D_STATE), f32),
                        pltpu.VMEM((SUBLANES, XBC_WIDTH), f32),
                        pltpu.VMEM((SSD_CHUNK, SSD_WIDTH), f32)],
        compiler_params=pltpu.CompilerParams(dimension_semantics=("arbitrary", "arbitrary"),
                                             vmem_limit_bytes=VMEM_LIMIT_BYTES),
        name="prompt_ssd",
    )(xbc, lfdt, z, cw, cb, a_row, dsk, ng, tri)


def _mlp_tail(x, mix, gpost, gpre, gmpost, wup_ref, wdn_ref, ff_chunk):
    x1 = x + _rms(mix, gpost)
    hmid = _rms(x1, gpre).astype(bf16)
    acc = jnp.zeros_like(x1)
    for c in range(D_FF // ff_chunk):
        cs = slice(c * ff_chunk, (c + 1) * ff_chunk)
        up = jnp.dot(hmid, wup_ref[:, cs], preferred_element_type=f32)
        act = jnp.square(jnp.maximum(up, 0.0)).astype(bf16)
        acc = acc + jnp.dot(act, wdn_ref[cs, :], preferred_element_type=f32)
    return x1 + _rms(acc, gmpost)


def _out_mlp_attn_body(pt_ref, x_ref, att_ref, yssd_ref, yconv_ref, wo_ref, gpost_ref, gpre_ref, gmpost_ref,
                       wup_ref, wdn_ref, qc_ref, knc_ref, vnc_ref, lfn_ref, *rest, pps, ff_chunk):
    k_refs = rest[0:pps]
    v_refs = rest[pps:2 * pps]
    sb_refs = rest[2 * pps:3 * pps]
    o_ref, oa_ref, x1_ref, h_ref, acc_ref, m_ref, l_ref, acca_ref, r_ref = rest[3 * pps:]
    j = pl.program_id(1)
    last = pl.num_programs(1) - 1
    dot = functools.partial(jnp.dot, preferred_element_type=f32)

    @pl.when(j == 0)
    def _():
        mix = (dot(att_ref[...], wo_ref[0:ATT_WIDTH, :])
               + dot(yssd_ref[...], wo_ref[ATT_WIDTH:ATT_WIDTH + SSD_WIDTH, :])
               + dot(yconv_ref[...], wo_ref[ATT_WIDTH + SSD_WIDTH:D_MODEL, :]))
        x1 = x_ref[...] + _rms(mix, gpost_ref[...])
        x1_ref[...] = x1
        h_ref[...] = _rms(x1, gpre_ref[...]).astype(bf16)
        acc_ref[...] = jnp.zeros_like(acc_ref)
        _paged_attn_init(lfn_ref, m_ref, l_ref, acca_ref, r_ref)

    nchunk = wup_ref.shape[2] // ff_chunk
    half = ff_chunk // 2
    groups = 4 * nchunk
    per = -(-pps // groups)
    pages = lambda g: _paged_attn_pages(qc_ref, k_refs[g * per:(g + 1) * per], v_refs[g * per:(g + 1) * per],
                                        sb_refs[g * per:(g + 1) * per], m_ref, l_ref, acca_ref, r_ref)
    relu2 = lambda u: jnp.square(jnp.maximum(u, 0.0)).astype(bf16)
    hmid = h_ref[...]
    acc = acc_ref[...]
    for c in range(nchunk):
        lo = slice(c * ff_chunk, c * ff_chunk + half)
        hi = slice(c * ff_chunk + half, (c + 1) * ff_chunk)
        up_lo = dot(hmid, wup_ref[j, :, lo])
        pages(4 * c)
        up_hi = dot(hmid, wup_ref[j, :, hi])
        pages(4 * c + 1)
        acc = acc + dot(relu2(up_lo), wdn_ref[j, lo, :])
        pages(4 * c + 2)
        acc = acc + dot(relu2(up_hi), wdn_ref[j, hi, :])
        pages(4 * c + 3)
    acc_ref[...] = acc

    @pl.when(j == last)
    def _():
        o_ref[...] = x1_ref[...] + _rms(acc_ref[...], gmpost_ref[...])
        _paged_attn_finish(qc_ref, knc_ref, vnc_ref, oa_ref, m_ref, l_ref, acca_ref)


def _out_mlp_attn(x, att, yssd, yconv, wo, gpost, gpre, gmpost, wup, wdn,
                  page_table, qc, knc, vnc, lfn, kt, vt, sb, layer, *, tm):
    M = x.shape[0]
    n, n_pages = page_table.shape
    pps = min(PAGES_PER_STEP, n_pages)
    nsub = n_pages // pps
    assert M // tm == n and n_pages % pps == 0 and D_FF % nsub == 0
    fs = D_FF // nsub
    wup = jnp.transpose(wup.reshape(D_MODEL, nsub, fs), (1, 0, 2))
    wdn = wdn.reshape(nsub, fs, D_MODEL)
    pt = page_table.reshape(-1)
    row = lambda width: pl.BlockSpec((tm, width), lambda i, j, pt: (i, 0))
    const = lambda shape: pl.BlockSpec(shape, lambda i, j, pt: (0,) * len(shape), pipeline_mode=pl.Buffered(1))
    seq = lambda rows: pl.BlockSpec((1, rows, LANES), lambda i, j, pt: (i, 0, 0))

    def page(i, j, pt, k):
        return pt[i * n_pages + n_pages - 1 - (j * pps + k)]

    kv_specs = [pl.BlockSpec((None, None, ATT_HEADS, HEAD_DIM, PAGE_SIZE),
                             lambda i, j, pt, k=k: (layer, page(i, j, pt, k), 0, 0, 0)) for k in range(pps)]
    sb_specs = [pl.BlockSpec((None, None, SUBLANES, 2 * LANES),
                             lambda i, j, pt, k=k: (layer, page(i, j, pt, k), 0, 0)) for k in range(pps)]
    return pl.pallas_call(
        functools.partial(_out_mlp_attn_body, pps=pps, ff_chunk=min(512, fs)),
        out_shape=(jax.ShapeDtypeStruct((M, D_MODEL), f32), jax.ShapeDtypeStruct((n, ATT_WIDTH, LANES), f32)),
        grid_spec=pltpu.PrefetchScalarGridSpec(
            num_scalar_prefetch=1,
            grid=(n, nsub),
            in_specs=[row(D_MODEL), row(ATT_WIDTH), row(SSD_WIDTH), row(CONV_WIDTH),
                      const((D_MODEL, D_MODEL)), const((1, D_MODEL)), const((1, D_MODEL)), const((1, D_MODEL)),
                      const((nsub, D_MODEL, fs)), const((nsub, fs, D_MODEL)),
                      seq(ATT_WIDTH), seq(ATT_WIDTH), seq(ATT_WIDTH), seq(SUBLANES)]
            + kv_specs + kv_specs + sb_specs,
            out_specs=(row(D_MODEL), seq(ATT_WIDTH)),
            scratch_shapes=[pltpu.VMEM((tm, D_MODEL), f32), pltpu.VMEM((tm, D_MODEL), bf16),
                            pltpu.VMEM((tm, D_MODEL), f32),
                            pltpu.VMEM((SUBLANES, LANES), f32), pltpu.VMEM((SUBLANES, LANES), f32),
                            pltpu.VMEM((ATT_HEADS, HEAD_DIM, LANES), f32), pltpu.VMEM((SUBLANES, LANES), f32)],
        ),
        compiler_params=pltpu.CompilerParams(dimension_semantics=("arbitrary", "arbitrary"),
                                             vmem_limit_bytes=VMEM_LIMIT_BYTES),
        name="out_mlp_sample_attn",
    )(pt, x, att, yssd, yconv, wo, gpost, gpre, gmpost, wup, wdn, qc, knc, vnc, lfn,
      *([kt] * pps), *([vt] * pps), *([sb] * pps))


def _sample_in_body(x_ref, g_ref, w_ref, bias_ref, cst_ref, cw_ref, cb_ref, aexp_ref,
                    q_ref, k_ref, v_ref, lf_ref, z_ref, gates_ref, xs_ref, bm_ref, cm_ref,
                    dtxt_ref, dect_ref, cnew_ref):
    n = x_ref.shape[0]
    h = _rms(x_ref[...], g_ref[...]).astype(bf16)
    proj = _dot_nt(h, w_ref[...])
    q_ref[...] = proj[:, OFF_Q:OFF_K] * (HEAD_DIM ** -0.5)
    k_ref[...] = proj[:, OFF_K:OFF_V]
    v_ref[...] = proj[:, OFF_V:OFF_Z]
    z_ref[...] = proj[:, OFF_Z:OFF_XBC]
    gates_ref[...] = proj[:, OFF_GB:OFF_SMALL]
    logf, dt = _softplus_parts(proj[:, OFF_SMALL:PROJ_WIDTH] + bias_ref[...])
    lf_ref[...] = logf

    xbc = proj[:, OFF_XBC:OFF_GB]
    xc = (cw_ref[0:1, :] * cst_ref[0] + cw_ref[1:2, :] * cst_ref[1] + cw_ref[2:3, :] * cst_ref[2]
          + cw_ref[3:4, :] * xbc + cb_ref[...])
    xc = _silu(xc)
    cnew_ref[0] = cst_ref[1]
    cnew_ref[1] = cst_ref[2]
    cnew_ref[2] = xbc
    xs = xc[:, 0:SSD_WIDTH]
    xs_ref[...] = xs
    bm_ref[...] = xc[:, SSD_WIDTH:SSD_WIDTH + SSD_GROUPS * D_STATE]
    cm_ref[...] = xc[:, SSD_WIDTH + SSD_GROUPS * D_STATE:XBC_WIDTH]

    head = lax.broadcasted_iota(jnp.int32, (n, SSD_WIDTH), 1) // SSD_HEAD_DIM
    dt_exp = jnp.zeros((n, SSD_WIDTH), f32)
    for hh in range(SSD_HEADS):
        dt_exp = jnp.where(head == hh, dt[:, COL_DT + hh:COL_DT + hh + 1], dt_exp)
    pad = jnp.zeros((LANES - n, SSD_WIDTH), f32)
    dtxt_ref[...] = jnp.concatenate([dt_exp * xs, pad], axis=0).T
    dect_ref[...] = jnp.concatenate([jnp.exp(dt_exp * aexp_ref[...]), pad], axis=0).T


def _sample_in(x, g, w, bias, cst, cw, cb, aexp):
    n = x.shape[0]
    s = lambda *shape: jax.ShapeDtypeStruct(shape, f32)
    return pl.pallas_call(
        _sample_in_body,
        out_shape=(s(n, ATT_WIDTH), s(n, ATT_WIDTH), s(n, ATT_WIDTH), s(n, LANES), s(n, SSD_WIDTH),
                   s(n, 3 * CONV_WIDTH), s(n, SSD_WIDTH), s(n, SSD_GROUPS * D_STATE), s(n, SSD_GROUPS * D_STATE),
                   s(SSD_WIDTH, LANES), s(SSD_WIDTH, LANES), s(SSD_CONV - 1, n, XBC_WIDTH)),
        compiler_params=pltpu.CompilerParams(vmem_limit_bytes=VMEM_LIMIT_BYTES),
        name="sample_in",
    )(x, g, w, bias, cst, cw, cb, aexp)


def _sample_ssd_body(s_ref, dtxt_ref, dect_ref, bm_ref, cm_ref, snew_ref, yt_ref):
    b = pl.program_id(0)
    rows = SSD_WIDTH
    half = rows // SSD_GROUPS
    lane = lax.broadcasted_iota(jnp.int32, (rows, LANES), 1)
    sel = lane == b
    dcol = jnp.sum(jnp.where(sel, dect_ref[...], 0.0), axis=1, keepdims=True)
    xcol = jnp.sum(jnp.where(sel, dtxt_ref[...], 0.0), axis=1, keepdims=True)

    def expand(ref):
        r = ref[pl.ds(b, 1), :]
        return jnp.concatenate([jnp.broadcast_to(r[:, g * D_STATE:(g + 1) * D_STATE], (half, D_STATE))
                                for g in range(SSD_GROUPS)], axis=0)

    s_old = s_ref[0].reshape(rows, D_STATE)
    s_new = dcol * s_old + xcol * expand(bm_ref)
    snew_ref[0] = s_new.reshape(SSD_HEADS, SSD_HEAD_DIM, D_STATE)
    ycol = jnp.sum(s_new * expand(cm_ref), axis=1, keepdims=True)

    @pl.when(b == 0)
    def _():
        yt_ref[...] = jnp.zeros_like(yt_ref)

    yt_ref[...] += jnp.where(sel, ycol, 0.0)


def _sample_ssd(state, dtxt, dect, bm, cm):
    n = state.shape[0]
    const = lambda shape: pl.BlockSpec(shape, lambda b: (0, 0))
    sblk = pl.BlockSpec((1, SSD_HEADS, SSD_HEAD_DIM, D_STATE), lambda b: (b, 0, 0, 0))
    return pl.pallas_call(
        _sample_ssd_body,
        out_shape=(jax.ShapeDtypeStruct(state.shape, f32), jax.ShapeDtypeStruct((SSD_WIDTH, LANES), f32)),
        grid=(n,),
        in_specs=[sblk, const((SSD_WIDTH, LANES)), const((SSD_WIDTH, LANES)),
                  const(bm.shape), const(cm.shape)],
        out_specs=(sblk, const((SSD_WIDTH, LANES))),
        compiler_params=pltpu.CompilerParams(dimension_semantics=("arbitrary",)),
        name="sample_ssd",
    )(state, dtxt, dect, bm, cm)


def _page_bias_body(lf_ref, upper_ref, o_ref):
    pb = lf_ref.shape[1]
    for hh in range(ATT_HEADS):
        x = lf_ref[hh]
        o_ref[:, hh, 0:LANES] = _exact_dot_rhs(x, upper_ref[...])
        o_ref[:, hh, LANES:2 * LANES] = jnp.broadcast_to(jnp.sum(x, axis=1, keepdims=True), x.shape)
    o_ref[:, ATT_HEADS:SUBLANES, :] = jnp.zeros((pb, SUBLANES - ATT_HEADS, 2 * LANES), f32)


def _page_bias(lft):
    depth, _, pool, _ = lft.shape
    pb = _pick_tile(pool, 256)
    pos = jnp.arange(PAGE_SIZE)
    upper = (pos[:, None] > pos[None, :]).astype(bf16)
    return pl.pallas_call(
        _page_bias_body,
        out_shape=jax.ShapeDtypeStruct((depth, pool, SUBLANES, 2 * LANES), f32),
        grid=(depth, pool // pb),
        in_specs=[pl.BlockSpec((None, ATT_HEADS, pb, PAGE_SIZE), lambda d, i: (d, 0, i, 0)),
                  pl.BlockSpec((PAGE_SIZE, PAGE_SIZE), lambda d, i: (0, 0))],
        out_specs=pl.BlockSpec((None, pb, SUBLANES, 2 * LANES), lambda d, i: (d, i, 0, 0)),
        compiler_params=pltpu.CompilerParams(dimension_semantics=("arbitrary", "arbitrary")),
        name="page_bias",
    )(lft, upper)


def _head_rows(hh):
    return slice(HEAD_DIM * hh, HEAD_DIM * (hh + 1))


def _paged_scores(qc_ref, k_of_head):
    hrow = lax.broadcasted_iota(jnp.int32, (SUBLANES, LANES), 0)
    s = jnp.zeros((SUBLANES, LANES), f32)
    for hh in range(ATT_HEADS):
        prod = qc_ref[0, _head_rows(hh), :] * k_of_head(hh)
        s = jnp.where(hrow == hh, jnp.sum(prod, axis=0, keepdims=True), s)
    return s


def _paged_attn_init(lfn_ref, m_ref, l_ref, acc_ref, r_ref):
    m_ref[...] = jnp.full(m_ref.shape, NEG_BIG, f32)
    l_ref[...] = jnp.zeros_like(l_ref)
    acc_ref[...] = jnp.zeros_like(acc_ref)
    r_ref[...] = lfn_ref[0]


def _paged_attn_pages(qc_ref, k_refs, v_refs, sb_refs, m_ref, l_ref, acc_ref, r_ref):
    for k_ref, v_ref, sb_ref in zip(k_refs, v_refs, sb_refs):
        sb = sb_ref[...]
        r = r_ref[...]
        s = _paged_scores(qc_ref, lambda hh: k_ref[hh]) + (r + sb[:, 0:LANES])
        r_ref[...] = r + sb[:, LANES:2 * LANES]
        m_old = m_ref[...]
        m_new = jnp.maximum(m_old, s)
        alpha = jnp.exp(m_old - m_new)
        p = jnp.exp(s - m_new)
        l_ref[...] = alpha * l_ref[...] + p
        m_ref[...] = m_new
        for hh in range(ATT_HEADS):
            acc_ref[hh] = alpha[hh:hh + 1, :] * acc_ref[hh] + p[hh:hh + 1, :] * v_ref[hh]


def _paged_attn_finish(qc_ref, knc_ref, vnc_ref, o_ref, m_ref, l_ref, acc_ref):
    m = m_ref[...]
    s_new = _paged_scores(qc_ref, lambda hh: knc_ref[0, _head_rows(hh), :])
    mx = jnp.maximum(jnp.max(m, axis=1, keepdims=True), s_new)
    w = jnp.exp(m - mx)
    w_new = jnp.exp(s_new - mx)
    denom = jnp.sum(l_ref[...] * w, axis=1, keepdims=True) + w_new
    for hh in range(ATT_HEADS):
        num = (jnp.sum(acc_ref[hh] * w[hh:hh + 1, :], axis=1, keepdims=True)
               + w_new[hh:hh + 1, :] * vnc_ref[0, _head_rows(hh), :])
        o_ref[0, _head_rows(hh), :] = num / denom[hh:hh + 1, :]


def _sample_out_body(x_ref, att_ref, yt_ref, xs_ref, z_ref, gates_ref, sst_ref, scw_ref, dsk_ref, ng_ref,
                     wo_ref, gpost_ref, gpre_ref, gmpost_ref, wup_ref, wdn_ref, o_ref, snew_ref, *, ff_chunk):
    n = x_ref.shape[0]
    y = yt_ref[...].T[0:n, :] + dsk_ref[...] * xs_ref[...]
    yssd = _rms(y * _silu(z_ref[...]), ng_ref[...])
    gates = gates_ref[...]
    u = gates[:, CONV_WIDTH:2 * CONV_WIDTH] * gates[:, 2 * CONV_WIDTH:3 * CONV_WIDTH]
    uc = scw_ref[0:1, :] * sst_ref[0] + scw_ref[1:2, :] * sst_ref[1] + scw_ref[2:3, :] * u
    yconv = gates[:, 0:CONV_WIDTH] * uc
    snew_ref[0] = sst_ref[1]
    snew_ref[1] = u
    dot = functools.partial(jnp.dot, preferred_element_type=f32)
    mix = (dot(att_ref[...].astype(bf16), wo_ref[0:ATT_WIDTH, :])
           + dot(yssd.astype(bf16), wo_ref[ATT_WIDTH:ATT_WIDTH + SSD_WIDTH, :])
           + dot(yconv.astype(bf16), wo_ref[ATT_WIDTH + SSD_WIDTH:D_MODEL, :]))
    o_ref[...] = _mlp_tail(x_ref[...], mix, gpost_ref[...], gpre_ref[...], gmpost_ref[...],
                           wup_ref, wdn_ref, ff_chunk)


def _sample_out(x, att, yt, xs, z, gates, sst, scw, dsk, ng, wo, gpost, gpre, gmpost, wup, wdn):
    n = x.shape[0]
    return pl.pallas_call(
        functools.partial(_sample_out_body, ff_chunk=512),
        out_shape=(jax.ShapeDtypeStruct((n, D_MODEL), f32),
                   jax.ShapeDtypeStruct((SCONV - 1, n, CONV_WIDTH), f32)),
        compiler_params=pltpu.CompilerParams(vmem_limit_bytes=VMEM_LIMIT_BYTES),
        name="sample_out_mlp",
    )(x, att, yt, xs, z, gates, sst, scw, dsk, ng, wo, gpost, gpre, gmpost, wup, wdn)


def _prep_w_in(wt):
    offs = [0]
    for s in IN_SIZES:
        offs.append(offs[-1] + s)
    q, k, v, f, z, xbc, dt, gb, gc, hc = [wt[offs[i]:offs[i + 1]] for i in range(len(IN_SIZES))]
    small = jnp.concatenate([f, dt, jnp.zeros((LANES - ATT_HEADS - SSD_HEADS, wt.shape[1]), wt.dtype)], axis=0)
    return jnp.concatenate([q, k, v, z, xbc, gb, gc, hc, small], axis=0).astype(bf16)


def kernel(x_prompt, x_sample, cache_k, cache_v, cache_logf, state_ssm, state_ssd_conv, state_sconv, page_table,
           w_in, b_f, ssd_conv_w, ssd_conv_b, dt_bias, a_log, d_skip, ssd_norm_g, sconv_w, w_out,
           g_mix_pre, g_mix_post, g_mlp_pre, g_mlp_post, w_mlp_up, w_mlp_down):
    depth = w_in.shape[0]
    B, L, _ = x_prompt.shape
    n = x_sample.shape[0]
    assert x_sample.shape[1] == 1 and n <= LANES
    tm = _pick_tile(L, 512)
    tq = _pick_tile(L, 256)
    tk = _pick_tile(L, 512)

    kt_cache = jnp.transpose(cache_k, (0, 1, 3, 4, 2))
    vt_cache = jnp.transpose(cache_v, (0, 1, 3, 4, 2))
    page_terms = _page_bias(jnp.transpose(cache_logf, (0, 3, 1, 2)))

    w_in_t = jnp.transpose(w_in, (0, 2, 1))

    xp = x_prompt
    xs = x_sample.reshape(n, D_MODEL)
    kv_all = None
    outs_p = [[] for _ in range(6)]
    outs_s = [[] for _ in range(6)]
    row = lambda a: a.reshape(1, -1)
    lanes_bcast = lambda a: jnp.broadcast_to(a[:, :, None], a.shape + (LANES,))

    for l in range(depth):
        w = _prep_w_in(w_in_t[l])
        wo = w_out[l].astype(bf16)
        wup = w_mlp_up[l].astype(bf16)
        wdn = w_mlp_down[l].astype(bf16)
        bias = jnp.concatenate([b_f[l], dt_bias[l], jnp.zeros((LANES - ATT_HEADS - SSD_HEADS,), f32)]).reshape(1, LANES)
        a_neg = -jnp.exp(a_log[l])
        a_row = jnp.concatenate([jnp.zeros((COL_DT,), f32), a_neg,
                                 jnp.zeros((LANES - COL_DT - SSD_HEADS,), f32)]).reshape(1, LANES)
        a_exp = jnp.repeat(a_neg, SSD_HEAD_DIM).reshape(1, SSD_WIDTH)
        dsk = jnp.repeat(d_skip[l], SSD_HEAD_DIM).reshape(1, SSD_WIDTH)
        ng = row(ssd_norm_g[l])
        cw, cb, scw = ssd_conv_w[l], row(ssd_conv_b[l]), sconv_w[l]
        gpre, gpost, gmpre, gmpost = row(g_mix_pre[l]), row(g_mix_post[l]), row(g_mlp_pre[l]), row(g_mlp_post[l])

        (qtp, kp, vtp, kt_all, vt_all, z, xbc, yconv, lfdt, lft, utail) = _prompt_in(
            xp, gpre, w, bias, scw, kv_all, l, depth, tm=tm, tq=tq, tk=tk)
        kv_all = (kt_all, vt_all)
        att = _prompt_attn(qtp, kp, vtp, tq=tq, tk=tk)
        yssd, s_fin = _prompt_ssd(xbc, lfdt, z, cw, cb, a_row, dsk, ng, tm=tm)

        cst = jnp.transpose(state_ssd_conv[l], (1, 0, 2))
        sst = jnp.transpose(state_sconv[l], (1, 0, 2))
        (q, k, v, lf, z_s, gates, xs_s, bm, cm, dtxt, dect, cnew) = _sample_in(xs, gpre, w, bias, cst, cw, cb, a_exp)
        s_new, yt = _sample_ssd(state_ssm[l], dtxt, dect, bm, cm)
        lf6 = lf[:, COL_LOGF:COL_LOGF + ATT_HEADS]
        lfn = lanes_bcast(jnp.pad(lf6, ((0, 0), (0, SUBLANES - ATT_HEADS))))

        xp, att_s = _out_mlp_attn(
            xp.reshape(B * L, D_MODEL), att.reshape(B * L, ATT_WIDTH), yssd.reshape(B * L, SSD_WIDTH),
            yconv.reshape(B * L, CONV_WIDTH), wo, gpost, gmpre, gmpost, wup, wdn,
            page_table, lanes_bcast(q), lanes_bcast(k), lanes_bcast(v), lfn, kt_cache, vt_cache, page_terms, l,
            tm=tm)
        xp = xp.reshape(B, L, D_MODEL)
        xs, snew = _sample_out(xs, att_s[:, :, 0], yt, xs_s, z_s, gates, sst, scw, dsk, ng,
                               wo, gpost, gmpre, gmpost, wup, wdn)

        outs_p[2].append(lft[:, 0:ATT_HEADS, :])
        outs_p[3].append(s_fin)
        outs_p[4].append(xbc[:, L - (SSD_CONV - 1):, :])
        outs_p[5].append(utail[:, SUBLANES - (SCONV - 1):, :])
        outs_s[0].append(k.reshape(n, 1, ATT_HEADS, HEAD_DIM))
        outs_s[1].append(v.reshape(n, 1, ATT_HEADS, HEAD_DIM))
        outs_s[2].append(lf6.reshape(n, 1, ATT_HEADS))
        outs_s[3].append(s_new)
        outs_s[4].append(jnp.transpose(cnew, (1, 0, 2)))
        outs_s[5].append(jnp.transpose(snew, (1, 0, 2)))

    heads_last = lambda a: jnp.transpose(a.reshape(depth, B, ATT_HEADS, HEAD_DIM, L), (0, 1, 4, 2, 3))
    k_p = heads_last(kv_all[0])
    v_p = heads_last(kv_all[1])
    lf_p = jnp.transpose(jnp.stack(outs_p[2]), (0, 1, 3, 2))
    rest_p = [jnp.stack(a) for a in outs_p[3:]]
    stacked_s = [jnp.stack(a) for a in outs_s]
    return (xp, xs.reshape(n, 1, D_MODEL), k_p, v_p, lf_p, *rest_p, *stacked_s)
```

```python
import functools

import jax
import jax.numpy as jnp
from jax import lax
from jax.experimental import pallas as pl
from jax.experimental.pallas import tpu as pltpu

f32 = jnp.float32
bf16 = jnp.bfloat16

D_MODEL = 1024
HEAD_DIM = 64
ATT_HEADS = 6
ATT_WIDTH = ATT_HEADS * HEAD_DIM
SSD_HEADS = 6
SSD_HEAD_DIM = 64
SSD_WIDTH = SSD_HEADS * SSD_HEAD_DIM
SSD_GROUPS = 2
D_STATE = 64
SSD_CONV = 4
XBC_WIDTH = SSD_WIDTH + 2 * SSD_GROUPS * D_STATE
CONV_WIDTH = 256
SCONV = 3
D_FF = 4 * D_MODEL
SSD_CHUNK = 128
PAGE_SIZE = 128
PAGES_PER_STEP = 16
RMS_EPS = 1e-6
IN_SIZES = (ATT_WIDTH, ATT_WIDTH, ATT_WIDTH, ATT_HEADS, SSD_WIDTH, XBC_WIDTH, SSD_HEADS,
            CONV_WIDTH, CONV_WIDTH, CONV_WIDTH)

LANES = 128
SUBLANES = 8
VMEM_LIMIT_BYTES = 56 * 1024 * 1024

OFF_Q = 0
OFF_K = OFF_Q + ATT_WIDTH
OFF_V = OFF_K + ATT_WIDTH
OFF_Z = OFF_V + ATT_WIDTH
OFF_XBC = OFF_Z + SSD_WIDTH
OFF_GB = OFF_XBC + XBC_WIDTH
OFF_GC = OFF_GB + CONV_WIDTH
OFF_HC = OFF_GC + CONV_WIDTH
OFF_SMALL = OFF_HC + CONV_WIDTH
PROJ_WIDTH = OFF_SMALL + LANES
COL_LOGF = 0
COL_DT = ATT_HEADS
PADDED_HEAD = LANES
NEG_BIG = -1e30
LOG2E = 1.4426950408889634
VALUE_ROWS = 80


def _rms(x, g):
    var = jnp.mean(x * x, axis=-1, keepdims=True)
    return (x * lax.rsqrt(var + RMS_EPS)) * g


def _silu(x):
    return x * (1.0 / (1.0 + jnp.exp(-x)))


def _split3(a):
    a1 = a.astype(bf16)
    r1 = a - a1.astype(f32)
    a2 = r1.astype(bf16)
    a3 = (r1 - a2.astype(f32)).astype(bf16)
    return a1, a2, a3


def _exact_dot(m01, a):
    a1, a2, a3 = _split3(a)
    d = functools.partial(jnp.dot, preferred_element_type=f32)
    return d(m01, a1) + d(m01, a2) + d(m01, a3)


def _exact_dot_rhs(a, m01):
    a1, a2, a3 = _split3(a)
    d = functools.partial(jnp.dot, preferred_element_type=f32)
    return d(a1, m01) + d(a2, m01) + d(a3, m01)


def _dot_nt(a, b):
    return lax.dot_general(a, b, (((1,), (1,)), ((), ())), preferred_element_type=f32)


def _softplus_parts(t):
    sp = jnp.log1p(jnp.exp(-jnp.abs(t)))
    return jnp.minimum(t, 0.0) - sp, jnp.maximum(t, 0.0) + sp


def _shift_rows(x, k, tail):
    row = lax.broadcasted_iota(jnp.int32, x.shape, 0)
    y = pltpu.roll(x, k, 0)
    for r in range(k):
        y = jnp.where(row == r, tail[SUBLANES - k + r:SUBLANES - k + r + 1, :], y)
    return y


def _pick_tile(n, pref):
    t = min(n, pref)
    while n % t:
        t -= SUBLANES
    assert t > 0
    return t


def _prompt_in_body(x_ref, g_ref, w_ref, bias_ref, tri_ref, scw_ref, *refs, tq, tk, carried):
    _prompt_in_compute(x_ref, g_ref, w_ref, bias_ref, tri_ref, scw_ref, *refs[carried:], tq=tq, tk=tk)


def _prompt_in_compute(x_ref, g_ref, w_ref, bias_ref, tri_ref, scw_ref,
                    qtp_ref, kp_ref, vtp_ref, kt32_ref, vt32_ref, z_ref, xbc_ref, yconv_ref,
                    lfdt_ref, lft_ref, utail_ref,
                    carry_ref, tail_ref, *, tq, tk):
    tm = x_ref.shape[1]

    @pl.when(pl.program_id(1) == 0)
    def _():
        carry_ref[...] = jnp.zeros_like(carry_ref)
        tail_ref[...] = jnp.zeros_like(tail_ref)

    h = _rms(x_ref[0], g_ref[...]).astype(bf16)
    proj = lambda lo, hi: _dot_nt(h, w_ref[lo:hi, :])

    t = proj(OFF_SMALL, PROJ_WIDTH) + bias_ref[...]
    logf, dt = _softplus_parts(t)
    col = lax.broadcasted_iota(jnp.int32, t.shape, 1)
    lfdt_ref[0] = jnp.where(col < COL_DT, logf, dt)
    lft_ref[0] = logf.T[0:SUBLANES, :]
    c = _exact_dot(tri_ref[...], logf) + carry_ref[...]
    carry_ref[...] = c[tm - 1:tm, :]
    c = c * LOG2E
    c1, c2, c3 = [p.astype(f32) for p in _split3(c)]
    ct1, ct2, ct3 = [p.astype(f32) for p in _split3(c.T[0:SUBLANES, :])]

    qkv_t = _dot_nt(w_ref[OFF_Q:OFF_Z, :], h)
    kt = qkv_t[ATT_WIDTH:2 * ATT_WIDTH]
    vt = qkv_t[2 * ATT_WIDTH:3 * ATT_WIDTH]
    kt32_ref[0] = kt
    vt32_ref[0] = vt
    k = kt.T
    lane = lax.broadcasted_iota(jnp.int32, (tm, PADDED_HEAD - HEAD_DIM), 1)
    row = lax.broadcasted_iota(jnp.int32, (PADDED_HEAD - HEAD_DIM, tm), 0)
    vrow = lax.broadcasted_iota(jnp.int32, (VALUE_ROWS - HEAD_DIM, tm), 0)
    ones_row = jnp.where(vrow == 0, 1.0, 0.0).astype(bf16)
    for hh in range(ATT_HEADS):
        src = slice(HEAD_DIM * hh, HEAD_DIM * (hh + 1))
        lo = slice(PADDED_HEAD * hh, PADDED_HEAD * hh + HEAD_DIM)
        hi = slice(PADDED_HEAD * hh + HEAD_DIM, PADDED_HEAD * (hh + 1))
        hc = slice(hh, hh + 1)
        kp_ref[0, :, lo] = k[:, src].astype(bf16)
        ek = jnp.where(lane < 3, 1.0, jnp.where(lane == 3, -c1[:, hc], jnp.where(
            lane == 4, -c2[:, hc], jnp.where(lane == 5, -c3[:, hc], 0.0))))
        kp_ref[0, :, hi] = ek.astype(bf16)
        eq = jnp.where(row == 0, ct1[hc, :], jnp.where(row == 1, ct2[hc, :], jnp.where(
            row == 2, ct3[hc, :], jnp.where(row < 6, 1.0, 0.0)))).astype(bf16)
        qh = (qkv_t[src] * (HEAD_DIM ** -0.5 * LOG2E)).astype(bf16)
        for i in range(tm // tq):
            cs = slice(i * tq, (i + 1) * tq)
            qtp_ref[0, i, lo, :] = qh[:, cs]
            qtp_ref[0, i, hi, :] = eq[:, cs]
        vh = vt[src].astype(bf16)
        for i in range(tm // tk):
            cs = slice(i * tk, (i + 1) * tk)
            vtp_ref[0, i, VALUE_ROWS * hh:VALUE_ROWS * hh + HEAD_DIM, :] = vh[:, cs]
            vtp_ref[0, i, VALUE_ROWS * hh + HEAD_DIM:VALUE_ROWS * (hh + 1), :] = ones_row[:, cs]

    z_ref[0] = proj(OFF_Z, OFF_XBC)
    xbc_ref[0] = proj(OFF_XBC, OFF_GB)

    gates = proj(OFF_GB, OFF_SMALL)
    gb = gates[:, 0:CONV_WIDTH]
    u = gates[:, CONV_WIDTH:2 * CONV_WIDTH] * gates[:, 2 * CONV_WIDTH:3 * CONV_WIDTH]
    tail = tail_ref[...]
    uc = (scw_ref[0:1, :] * _shift_rows(u, 2, tail) + scw_ref[1:2, :] * _shift_rows(u, 1, tail)
          + scw_ref[2:3, :] * u)
    yconv_ref[0] = (gb * uc).astype(bf16)
    tail_ref[...] = u[tm - SUBLANES:tm, :]
    utail_ref[0] = u[tm - SUBLANES:tm, :]


def _prompt_in(x, g, w, bias, scw, kv_all, layer, depth, *, tm, tq, tk):
    B, L, _ = x.shape
    nj = L // tm
    pw = ATT_HEADS * PADDED_HEAD
    tri = jnp.tril(jnp.ones((tm, tm), f32)).astype(bf16)
    row = lambda width: pl.BlockSpec((1, tm, width), lambda b, j: (b, j, 0))
    colb = lambda height: pl.BlockSpec((1, height, tm), lambda b, j: (b, 0, j))
    const = lambda shape: pl.BlockSpec(shape, lambda b, j: (0,) * len(shape))
    layer_colb = pl.BlockSpec((None, 1, ATT_WIDTH, tm), lambda b, j: (layer, b, 0, j))
    carried = () if kv_all is None else tuple(kv_all)
    out_shapes = (
        jax.ShapeDtypeStruct((B, L // tq, pw, tq), bf16),
        jax.ShapeDtypeStruct((B, L, pw), bf16),
        jax.ShapeDtypeStruct((B, L // tk, ATT_HEADS * VALUE_ROWS, tk), bf16),
        jax.ShapeDtypeStruct((depth, B, ATT_WIDTH, L), f32),
        jax.ShapeDtypeStruct((depth, B, ATT_WIDTH, L), f32),
        jax.ShapeDtypeStruct((B, L, SSD_WIDTH), f32),
        jax.ShapeDtypeStruct((B, L, XBC_WIDTH), f32),
        jax.ShapeDtypeStruct((B, L, CONV_WIDTH), bf16),
        jax.ShapeDtypeStruct((B, L, LANES), f32),
        jax.ShapeDtypeStruct((B, SUBLANES, L), f32),
        jax.ShapeDtypeStruct((B, SUBLANES, CONV_WIDTH), f32),
    )
    out_specs = (
        pl.BlockSpec((1, tm // tq, pw, tq), lambda b, j: (b, j, 0, 0)),
        row(pw),
        pl.BlockSpec((1, tm // tk, ATT_HEADS * VALUE_ROWS, tk), lambda b, j: (b, j, 0, 0)),
        layer_colb, layer_colb, row(SSD_WIDTH), row(XBC_WIDTH), row(CONV_WIDTH),
        row(LANES), colb(SUBLANES),
        pl.BlockSpec((1, SUBLANES, CONV_WIDTH), lambda b, j: (b, 0, 0)),
    )
    return pl.pallas_call(
        functools.partial(_prompt_in_body, tq=tq, tk=tk, carried=len(carried)),
        out_shape=out_shapes,
        grid=(B, nj),
        in_specs=[row(D_MODEL), const((1, D_MODEL)), const((PROJ_WIDTH, D_MODEL)), const((1, LANES)),
                  const((tm, tm)), const((SCONV, CONV_WIDTH))] + [pl.BlockSpec(memory_space=pl.ANY)] * len(carried),
        input_output_aliases={6 + i: 3 + i for i in range(len(carried))},
        out_specs=out_specs,
        scratch_shapes=[pltpu.VMEM((1, LANES), f32), pltpu.VMEM((SUBLANES, CONV_WIDTH), f32)],
        compiler_params=pltpu.CompilerParams(dimension_semantics=("arbitrary", "arbitrary"),
                                             vmem_limit_bytes=VMEM_LIMIT_BYTES),
        name="prompt_in",
    )(x, g, w, bias, tri, scw, *carried)


def _prompt_attn_body(qt_ref, kp_ref, vt_ref, o_ref, m_ref, acc_ref, st_ref, *, tq, tk):
    qi = pl.program_id(1)
    m_ref[...] = jnp.full(m_ref.shape, NEG_BIG, f32)
    acc_ref[...] = jnp.zeros_like(acc_ref)
    nfull = lax.div(qi * tq, tk)
    heads = [slice(PADDED_HEAD * hh, PADDED_HEAD * (hh + 1)) for hh in range(ATT_HEADS)]
    vrows = [slice(VALUE_ROWS * hh, VALUE_ROWS * (hh + 1)) for hh in range(ATT_HEADS)]

    def score(kj, slot, masked):
        ks = pl.multiple_of(kj * tk, tk)
        if masked:
            kpos = ks + lax.broadcasted_iota(jnp.int32, (tk, tq), 0)
            qpos = qi * tq + lax.broadcasted_iota(jnp.int32, (tk, tq), 1)
            valid = kpos <= qpos
        for hh, hs in enumerate(heads):
            st = jnp.dot(kp_ref[0, pl.ds(ks, tk), hs], qt_ref[0, 0, hs, :], preferred_element_type=f32)
            st_ref[slot, hh] = jnp.where(valid, st, NEG_BIG) if masked else st

    def absorb(kj, slot):
        probs = []
        for hh in range(ATT_HEADS):
            st = st_ref[slot, hh]
            m_old = m_ref[hh]
            m_new = jnp.maximum(m_old, jnp.max(st, axis=0, keepdims=True))
            m_ref[hh] = m_new
            probs.append((jnp.exp2(m_old[0:1] - m_new[0:1]), jnp.exp2(st - m_new[0:1]).astype(bf16)))
        for hh, (alpha, pt) in enumerate(probs):
            acc_ref[hh] = alpha * acc_ref[hh] + jnp.dot(vt_ref[0, kj, vrows[hh], :], pt,
                                                        preferred_element_type=f32)

    @pl.when(nfull > 0)
    def _():
        score(0, 0, False)

    def pair(pi, carry):
        k0 = 2 * pi
        score(k0 + 1, 1, False)
        absorb(k0, 0)
        score(k0 + 2, 0, False)
        absorb(k0 + 1, 1)
        return carry

    ntrip = lax.div(jnp.maximum(nfull - 1, 0), 2)
    lax.fori_loop(0, ntrip, pair, 0)
    done = 2 * ntrip
    rem = nfull - done

    @pl.when(rem == 2)
    def _():
        score(done + 1, 1, False)
        absorb(done, 0)
        score(done + 2, 0, True)
        absorb(done + 1, 1)
        absorb(done + 2, 0)

    @pl.when(rem == 1)
    def _():
        score(done + 1, 1, True)
        absorb(done, 0)
        absorb(done + 1, 1)

    @pl.when(rem == 0)
    def _():
        score(0, 0, True)
        absorb(0, 0)

    pad = jnp.zeros((PADDED_HEAD - VALUE_ROWS, tq), f32)
    for hh in range(ATT_HEADS):
        a = jnp.concatenate([acc_ref[hh], pad], axis=0).T
        o_ref[0, :, HEAD_DIM * hh:HEAD_DIM * (hh + 1)] = (
            a[:, 0:HEAD_DIM] / a[:, HEAD_DIM:HEAD_DIM + 1]).astype(bf16)


def _prompt_attn(qtp, kp, vtp, *, tq, tk):
    B, nq, W, _ = qtp.shape
    L = kp.shape[1]
    vw = ATT_HEADS * VALUE_ROWS
    return pl.pallas_call(
        functools.partial(_prompt_attn_body, tq=tq, tk=tk),
        out_shape=jax.ShapeDtypeStruct((B, L, ATT_WIDTH), bf16),
        grid=(B, nq),
        in_specs=[pl.BlockSpec((1, 1, W, tq), lambda b, i: (b, i, 0, 0)),
                  pl.BlockSpec((1, L, W), lambda b, i: (b, 0, 0)),
                  pl.BlockSpec((1, L // tk, vw, tk), lambda b, i: (b, 0, 0, 0))],
        out_specs=pl.BlockSpec((1, tq, ATT_WIDTH), lambda b, i: (b, i, 0)),
        scratch_shapes=[pltpu.VMEM((ATT_HEADS, SUBLANES, tq), f32),
                        pltpu.VMEM((ATT_HEADS, VALUE_ROWS, tq), f32),
                        pltpu.VMEM((2, ATT_HEADS, tk, tq), f32)],
        compiler_params=pltpu.CompilerParams(dimension_semantics=("arbitrary", "arbitrary"),
                                             vmem_limit_bytes=VMEM_LIMIT_BYTES),
        name="prompt_attn",
    )(qtp, kp, vtp)


def _prompt_ssd_body(xbc_ref, lfdt_ref, z_ref, cw_ref, cb_ref, a_ref, dsk_ref, ng_ref, tri_ref,
                     y_ref, sout_ref, s_ref, tail_ref):
    tm = xbc_ref.shape[1]
    Q = SSD_CHUNK

    @pl.when(pl.program_id(1) == 0)
    def _():
        s_ref[...] = jnp.zeros_like(s_ref)
        tail_ref[...] = jnp.zeros_like(tail_ref)

    x = xbc_ref[0]
    tail = tail_ref[...]
    xc = (cw_ref[0:1, :] * _shift_rows(x, 3, tail) + cw_ref[1:2, :] * _shift_rows(x, 2, tail)
          + cw_ref[2:3, :] * _shift_rows(x, 1, tail) + cw_ref[3:4, :] * x + cb_ref[...])
    xc = _silu(xc)
    tail_ref[...] = x[tm - SUBLANES:tm, :]

    dt_all = lfdt_ref[0]
    dta_all = dt_all * a_ref[...]
    row = lax.broadcasted_iota(jnp.int32, (Q, Q), 0)
    colm = lax.broadcasted_iota(jnp.int32, (Q, Q), 1)
    causal = row >= colm
    tri = tri_ref[...]
    group = lambda hh: slice((hh // (SSD_HEADS // SSD_GROUPS)) * D_STATE,
                             (hh // (SSD_HEADS // SSD_GROUPS) + 1) * D_STATE)
    head = lambda hh: slice(hh * SSD_HEAD_DIM, (hh + 1) * SSD_HEAD_DIM)

    states = [s_ref[hh] for hh in range(SSD_HEADS)]
    for c in range(tm // Q):
        rs = slice(c * Q, (c + 1) * Q)
        dt = dt_all[rs]
        acum = _exact_dot(tri, dta_all[rs])
        acum_t = acum.T
        dt_t = dt.T
        xs = xc[rs, 0:SSD_WIDTH]
        xs_t = xs.T.astype(bf16)
        xs_b = xs.astype(bf16)
        bm = xc[rs, SSD_WIDTH:SSD_WIDTH + SSD_GROUPS * D_STATE]
        cm = xc[rs, SSD_WIDTH + SSD_GROUPS * D_STATE:XBC_WIDTH].astype(bf16)
        cb = []
        for g in range(SSD_GROUPS):
            gs = slice(g * D_STATE, (g + 1) * D_STATE)
            cb.append(_dot_nt(cm[:, gs], bm[:, gs].astype(bf16)))
        terms = []
        for hh in range(SSD_HEADS):
            cc = COL_DT + hh
            a_col = acum[:, cc:cc + 1]
            a_row = acum_t[cc:cc + 1, :]
            a_last = acum[Q - 1:Q, cc:cc + 1]
            decay = jnp.exp(jnp.where(causal, a_col - a_row, -jnp.inf))
            mat = (cb[hh // (SSD_HEADS // SSD_GROUPS)] * decay * dt_t[cc:cc + 1, :]).astype(bf16)
            wcol = jnp.exp(a_last - a_col) * dt[:, cc:cc + 1]
            bw = (bm[:, group(hh)] * wcol).astype(bf16)
            terms.append((mat, bw, jnp.exp(a_col), jnp.exp(a_last)))
        ys = []
        for hh, (mat, bw, e_col, e_last) in enumerate(terms):
            s_in = states[hh]
            y_intra = jnp.dot(mat, xs_b[:, head(hh)], preferred_element_type=f32)
            y_inter = _dot_nt(cm[:, group(hh)], s_in.astype(bf16)) * e_col
            ys.append(y_intra + y_inter)
            states[hh] = e_last * s_in + jnp.dot(xs_t[head(hh), :], bw, preferred_element_type=f32)
        y = jnp.concatenate(ys, axis=1) + dsk_ref[...] * xs
        gated = y * _silu(z_ref[0, rs, :])
        y_ref[0, rs, :] = _rms(gated, ng_ref[...]).astype(bf16)
    for hh in range(SSD_HEADS):
        s_ref[hh] = states[hh]
    sout_ref[0] = jnp.stack(states)


def _prompt_ssd(xbc, lfdt, z, cw, cb, a_row, dsk, ng, *, tm):
    B, L, _ = xbc.shape
    tri = jnp.tril(jnp.ones((SSD_CHUNK, SSD_CHUNK), f32)).astype(bf16)
    row = lambda width: pl.BlockSpec((1, tm, width), lambda b, j: (b, j, 0))
    const = lambda shape: pl.BlockSpec(shape, lambda b, j: (0,) * len(shape))
    return pl.pallas_call(
        _prompt_ssd_body,
        out_shape=(jax.ShapeDtypeStruct((B, L, SSD_WIDTH), bf16),
                   jax.ShapeDtypeStruct((B, SSD_HEADS, SSD_HEAD_DIM, D_STATE), f32)),
        grid=(B, L // tm),
        in_specs=[row(XBC_WIDTH), row(LANES), row(SSD_WIDTH), const((SSD_CONV, XBC_WIDTH)),
                  const((1, XBC_WIDTH)), const((1, LANES)), const((1, SSD_WIDTH)), const((1, SSD_WIDTH)),
                  const((SSD_CHUNK, SSD_CHUNK))],
        out_specs=(row(SSD_WIDTH),
                   pl.BlockSpec((1, SSD_HEADS, SSD_HEAD_DIM, D_STATE), lambda b, j: (b, 0, 0, 0))),
        scratch_shapes=[pltpu.VMEM((SSD_HEADS, SSD_HEAD_DIM, D_STATE), f32),
                        pltpu.VMEM((SUBLANES, XBC_WIDTH), f32)],
        compiler_params=pltpu.CompilerParams(dimension_semantics=("arbitrary", "arbitrary"),
                                             vmem_limit_bytes=VMEM_LIMIT_BYTES),
        name="prompt_ssd",
    )(xbc, lfdt, z, cw, cb, a_row, dsk, ng, tri)


def _mlp_tail(x, mix, gpost, gpre, gmpost, wup_ref, wdn_ref, ff_chunk):
    x1 = x + _rms(mix, gpost)
    hmid = _rms(x1, gpre).astype(bf16)
    acc = jnp.zeros_like(x1)
    nsub, _, fs = wup_ref.shape
    for j in range(nsub):
        for c in range(fs // ff_chunk):
            cs = slice(c * ff_chunk, (c + 1) * ff_chunk)
            up = jnp.dot(hmid, wup_ref[j, :, cs], preferred_element_type=f32)
            act = jnp.square(jnp.maximum(up, 0.0)).astype(bf16)
            acc = acc + jnp.dot(act, wdn_ref[j, cs, :], preferred_element_type=f32)
    return x1 + _rms(acc, gmpost)


def _out_mlp_attn_body(pt_ref, x_ref, att_ref, yssd_ref, yconv_ref, wo_ref, gpost_ref, gpre_ref, gmpost_ref,
                       wup_ref, wdn_ref, qt_ref, knt_ref, vnt_ref, lfn_ref, *rest, pps, ff_chunk):
    k_refs = rest[0:pps]
    v_refs = rest[pps:2 * pps]
    sb_refs = rest[2 * pps:3 * pps]
    o_ref, oa_ref, x1_ref, h_ref, acc_ref, m_ref, l_ref, acca_ref, r_ref, qc_ref, knc_ref, vnc_ref = rest[3 * pps:]
    j = pl.program_id(1)
    last = pl.num_programs(1) - 1
    dot = functools.partial(jnp.dot, preferred_element_type=f32)

    @pl.when(j == 0)
    def _():
        mix = (dot(att_ref[...], wo_ref[0:ATT_WIDTH, :])
               + dot(yssd_ref[...], wo_ref[ATT_WIDTH:ATT_WIDTH + SSD_WIDTH, :])
               + dot(yconv_ref[...], wo_ref[ATT_WIDTH + SSD_WIDTH:D_MODEL, :]))
        x1 = x_ref[...] + _rms(mix, gpost_ref[...])
        x1_ref[...] = x1
        h_ref[...] = _rms(x1, gpre_ref[...]).astype(bf16)
        acc_ref[...] = jnp.zeros_like(acc_ref)
        _paged_attn_init(lfn_ref, m_ref, l_ref, acca_ref, r_ref)
        mine = lax.broadcasted_iota(jnp.int32, qt_ref.shape, 1) == pl.program_id(0)
        for src, dst in ((qt_ref, qc_ref), (knt_ref, knc_ref), (vnt_ref, vnc_ref)):
            col = jnp.sum(jnp.where(mine, src[...], 0.0), axis=1, keepdims=True)
            dst[...] = jnp.broadcast_to(col, dst.shape)

    nchunk = wup_ref.shape[2] // ff_chunk
    half = ff_chunk // 2
    groups = 4 * nchunk
    per = -(-pps // groups)
    pages = lambda g: _paged_attn_pages(qc_ref, k_refs[g * per:(g + 1) * per], v_refs[g * per:(g + 1) * per],
                                        sb_refs[g * per:(g + 1) * per], m_ref, l_ref, acca_ref, r_ref)
    relu2 = lambda u: jnp.square(jnp.maximum(u, 0.0)).astype(bf16)
    hmid = h_ref[...]
    acc = acc_ref[...]
    for c in range(nchunk):
        lo = slice(c * ff_chunk, c * ff_chunk + half)
        hi = slice(c * ff_chunk + half, (c + 1) * ff_chunk)
        up_lo = dot(hmid, wup_ref[j, :, lo])
        pages(4 * c)
        up_hi = dot(hmid, wup_ref[j, :, hi])
        pages(4 * c + 1)
        acc = acc + dot(relu2(up_lo), wdn_ref[j, lo, :])
        pages(4 * c + 2)
        acc = acc + dot(relu2(up_hi), wdn_ref[j, hi, :])
        pages(4 * c + 3)
    acc_ref[...] = acc

    @pl.when(j == last)
    def _():
        o_ref[...] = x1_ref[...] + _rms(acc_ref[...], gmpost_ref[...])
        _paged_attn_finish(qc_ref, knc_ref, vnc_ref, oa_ref, m_ref, l_ref, acca_ref)


def _out_mlp_attn(x, att, yssd, yconv, wo, gpost, gpre, gmpost, wup, wdn,
                  page_table, qt, knt, vnt, lfn, kt, vt, sb, layer, *, tm):
    M = x.shape[0]
    n, n_pages = page_table.shape
    pps = min(PAGES_PER_STEP, n_pages)
    nsub = n_pages // pps
    assert M // tm == n and n_pages % pps == 0 and D_FF % nsub == 0
    fs = D_FF // nsub
    assert wup.shape == (nsub, D_MODEL, fs) and wdn.shape == (nsub, fs, D_MODEL)
    pt = page_table.reshape(-1)
    row = lambda width: pl.BlockSpec((tm, width), lambda i, j, pt: (i, 0))
    const = lambda shape: pl.BlockSpec(shape, lambda i, j, pt: (0,) * len(shape), pipeline_mode=pl.Buffered(1))
    seq = lambda rows: pl.BlockSpec((1, rows, LANES), lambda i, j, pt: (i, 0, 0))

    def page(i, j, pt, k):
        return pt[i * n_pages + n_pages - 1 - (j * pps + k)]

    kv_specs = [pl.BlockSpec((None, None, ATT_HEADS, HEAD_DIM, PAGE_SIZE),
                             lambda i, j, pt, k=k: (layer, page(i, j, pt, k), 0, 0, 0)) for k in range(pps)]
    sb_specs = [pl.BlockSpec((None, None, SUBLANES, 2 * LANES),
                             lambda i, j, pt, k=k: (layer, page(i, j, pt, k), 0, 0)) for k in range(pps)]
    return pl.pallas_call(
        functools.partial(_out_mlp_attn_body, pps=pps, ff_chunk=min(512, fs)),
        out_shape=(jax.ShapeDtypeStruct((M, D_MODEL), f32), jax.ShapeDtypeStruct((n, ATT_WIDTH, LANES), f32)),
        grid_spec=pltpu.PrefetchScalarGridSpec(
            num_scalar_prefetch=1,
            grid=(n, nsub),
            in_specs=[row(D_MODEL), row(ATT_WIDTH), row(SSD_WIDTH), row(CONV_WIDTH),
                      const((D_MODEL, D_MODEL)), const((1, D_MODEL)), const((1, D_MODEL)), const((1, D_MODEL)),
                      const((nsub, D_MODEL, fs)), const((nsub, fs, D_MODEL)),
                      const((ATT_WIDTH, LANES)), const((ATT_WIDTH, LANES)), const((ATT_WIDTH, LANES)), seq(SUBLANES)]
            + kv_specs + kv_specs + sb_specs,
            out_specs=(row(D_MODEL), seq(ATT_WIDTH)),
            scratch_shapes=[pltpu.VMEM((tm, D_MODEL), f32), pltpu.VMEM((tm, D_MODEL), bf16),
                            pltpu.VMEM((tm, D_MODEL), f32),
                            pltpu.VMEM((SUBLANES, LANES), f32), pltpu.VMEM((SUBLANES, LANES), f32),
                            pltpu.VMEM((ATT_HEADS, HEAD_DIM, LANES), f32), pltpu.VMEM((SUBLANES, LANES), f32),
                            pltpu.VMEM((ATT_WIDTH, LANES), f32), pltpu.VMEM((ATT_WIDTH, LANES), f32),
                            pltpu.VMEM((ATT_WIDTH, LANES), f32)],
        ),
        compiler_params=pltpu.CompilerParams(dimension_semantics=("arbitrary", "arbitrary"),
                                             vmem_limit_bytes=VMEM_LIMIT_BYTES),
        name="out_mlp_sample_attn",
    )(pt, x, att, yssd, yconv, wo, gpost, gpre, gmpost, wup, wdn, qt, knt, vnt, lfn,
      *([kt] * pps), *([vt] * pps), *([sb] * pps))


def _sample_in_body(x_ref, g_ref, w_ref, bias_ref, cst_ref, cw_ref, cb_ref, aexp_ref,
                    q_ref, k_ref, v_ref, lf_ref, z_ref, gates_ref, xs_ref, bm_ref, cm_ref,
                    dtxt_ref, dect_ref, cnew_ref, qt_ref, kt_ref, vt_ref):
    n = x_ref.shape[0]
    h = _rms(x_ref[...], g_ref[...]).astype(bf16)
    proj = _dot_nt(h, w_ref[...])
    q_ref[...] = proj[:, OFF_Q:OFF_K] * (HEAD_DIM ** -0.5)
    k_ref[...] = proj[:, OFF_K:OFF_V]
    v_ref[...] = proj[:, OFF_V:OFF_Z]
    z_ref[...] = proj[:, OFF_Z:OFF_XBC]
    gates_ref[...] = proj[:, OFF_GB:OFF_SMALL]
    logf, dt = _softplus_parts(proj[:, OFF_SMALL:PROJ_WIDTH] + bias_ref[...])
    lf_ref[...] = logf

    xbc = proj[:, OFF_XBC:OFF_GB]
    xc = (cw_ref[0:1, :] * cst_ref[0] + cw_ref[1:2, :] * cst_ref[1] + cw_ref[2:3, :] * cst_ref[2]
          + cw_ref[3:4, :] * xbc + cb_ref[...])
    xc = _silu(xc)
    cnew_ref[0] = cst_ref[1]
    cnew_ref[1] = cst_ref[2]
    cnew_ref[2] = xbc
    xs = xc[:, 0:SSD_WIDTH]
    xs_ref[...] = xs
    bm_ref[...] = xc[:, SSD_WIDTH:SSD_WIDTH + SSD_GROUPS * D_STATE]
    cm_ref[...] = xc[:, SSD_WIDTH + SSD_GROUPS * D_STATE:XBC_WIDTH]

    head = lax.broadcasted_iota(jnp.int32, (n, SSD_WIDTH), 1) // SSD_HEAD_DIM
    dt_exp = jnp.zeros((n, SSD_WIDTH), f32)
    for hh in range(SSD_HEADS):
        dt_exp = jnp.where(head == hh, dt[:, COL_DT + hh:COL_DT + hh + 1], dt_exp)
    pad = jnp.zeros((LANES - n, SSD_WIDTH), f32)
    lanes_t = lambda a: jnp.concatenate([a, pad], axis=0).T
    dtxt_ref[...] = lanes_t(dt_exp * xs)
    dect_ref[...] = lanes_t(jnp.exp(dt_exp * aexp_ref[...]))
    qt_ref[...] = lanes_t(q_ref[...])
    kt_ref[...] = lanes_t(k_ref[...])
    vt_ref[...] = lanes_t(v_ref[...])


def _sample_in(x, g, w, bias, cst, cw, cb, aexp):
    n = x.shape[0]
    s = lambda *shape: jax.ShapeDtypeStruct(shape, f32)
    return pl.pallas_call(
        _sample_in_body,
        out_shape=(s(n, ATT_WIDTH), s(n, ATT_WIDTH), s(n, ATT_WIDTH), s(n, LANES), s(n, SSD_WIDTH),
                   s(n, 3 * CONV_WIDTH), s(n, SSD_WIDTH), s(n, SSD_GROUPS * D_STATE), s(n, SSD_GROUPS * D_STATE),
                   s(SSD_WIDTH, LANES), s(SSD_WIDTH, LANES), s(SSD_CONV - 1, n, XBC_WIDTH),
                   s(ATT_WIDTH, LANES), s(ATT_WIDTH, LANES), s(ATT_WIDTH, LANES)),
        compiler_params=pltpu.CompilerParams(vmem_limit_bytes=VMEM_LIMIT_BYTES),
        name="sample_in",
    )(x, g, w, bias, cst, cw, cb, aexp)


def _sample_ssd_body(s_ref, dtxt_ref, dect_ref, bm_ref, cm_ref, snew_ref, yt_ref):
    b = pl.program_id(0)
    rows = SSD_WIDTH
    half = rows // SSD_GROUPS
    lane = lax.broadcasted_iota(jnp.int32, (rows, LANES), 1)
    sel = lane == b
    dcol = jnp.sum(jnp.where(sel, dect_ref[...], 0.0), axis=1, keepdims=True)
    xcol = jnp.sum(jnp.where(sel, dtxt_ref[...], 0.0), axis=1, keepdims=True)

    def expand(ref):
        r = ref[pl.ds(b, 1), :]
        return jnp.concatenate([jnp.broadcast_to(r[:, g * D_STATE:(g + 1) * D_STATE], (half, D_STATE))
                                for g in range(SSD_GROUPS)], axis=0)

    s_old = s_ref[0].reshape(rows, D_STATE)
    s_new = dcol * s_old + xcol * expand(bm_ref)
    snew_ref[0] = s_new.reshape(SSD_HEADS, SSD_HEAD_DIM, D_STATE)
    ycol = jnp.sum(s_new * expand(cm_ref), axis=1, keepdims=True)

    @pl.when(b == 0)
    def _():
        yt_ref[...] = jnp.zeros_like(yt_ref)

    yt_ref[...] += jnp.where(sel, ycol, 0.0)


def _sample_ssd(state, dtxt, dect, bm, cm):
    n = state.shape[0]
    const = lambda shape: pl.BlockSpec(shape, lambda b: (0, 0))
    sblk = pl.BlockSpec((1, SSD_HEADS, SSD_HEAD_DIM, D_STATE), lambda b: (b, 0, 0, 0))
    return pl.pallas_call(
        _sample_ssd_body,
        out_shape=(jax.ShapeDtypeStruct(state.shape, f32), jax.ShapeDtypeStruct((SSD_WIDTH, LANES), f32)),
        grid=(n,),
        in_specs=[sblk, const((SSD_WIDTH, LANES)), const((SSD_WIDTH, LANES)),
                  const(bm.shape), const(cm.shape)],
        out_specs=(sblk, const((SSD_WIDTH, LANES))),
        compiler_params=pltpu.CompilerParams(dimension_semantics=("arbitrary",)),
        name="sample_ssd",
    )(state, dtxt, dect, bm, cm)


def _page_bias_body(lf_ref, upper_ref, o_ref):
    pb = lf_ref.shape[1]
    for hh in range(ATT_HEADS):
        x = lf_ref[hh]
        o_ref[:, hh, 0:LANES] = _exact_dot_rhs(x, upper_ref[...])
        o_ref[:, hh, LANES:2 * LANES] = jnp.broadcast_to(jnp.sum(x, axis=1, keepdims=True), x.shape)
    o_ref[:, ATT_HEADS:SUBLANES, :] = jnp.zeros((pb, SUBLANES - ATT_HEADS, 2 * LANES), f32)


def _page_bias(lft):
    depth, _, pool, _ = lft.shape
    pb = _pick_tile(pool, 256)
    pos = jnp.arange(PAGE_SIZE)
    upper = (pos[:, None] > pos[None, :]).astype(bf16)
    return pl.pallas_call(
        _page_bias_body,
        out_shape=jax.ShapeDtypeStruct((depth, pool, SUBLANES, 2 * LANES), f32),
        grid=(depth, pool // pb),
        in_specs=[pl.BlockSpec((None, ATT_HEADS, pb, PAGE_SIZE), lambda d, i: (d, 0, i, 0)),
                  pl.BlockSpec((PAGE_SIZE, PAGE_SIZE), lambda d, i: (0, 0))],
        out_specs=pl.BlockSpec((None, pb, SUBLANES, 2 * LANES), lambda d, i: (d, i, 0, 0)),
        compiler_params=pltpu.CompilerParams(dimension_semantics=("arbitrary", "arbitrary")),
        name="page_bias",
    )(lft, upper)


def _head_rows(hh):
    return slice(HEAD_DIM * hh, HEAD_DIM * (hh + 1))


def _paged_scores(qc_ref, k_of_head):
    hrow = lax.broadcasted_iota(jnp.int32, (SUBLANES, LANES), 0)
    s = jnp.zeros((SUBLANES, LANES), f32)
    for hh in range(ATT_HEADS):
        prod = qc_ref[_head_rows(hh), :] * k_of_head(hh)
        s = jnp.where(hrow == hh, jnp.sum(prod, axis=0, keepdims=True), s)
    return s


def _paged_attn_init(lfn_ref, m_ref, l_ref, acc_ref, r_ref):
    m_ref[...] = jnp.full(m_ref.shape, NEG_BIG, f32)
    l_ref[...] = jnp.zeros_like(l_ref)
    acc_ref[...] = jnp.zeros_like(acc_ref)
    r_ref[...] = lfn_ref[0]


def _paged_attn_pages(qc_ref, k_refs, v_refs, sb_refs, m_ref, l_ref, acc_ref, r_ref):
    for k_ref, v_ref, sb_ref in zip(k_refs, v_refs, sb_refs):
        sb = sb_ref[...]
        r = r_ref[...]
        s = _paged_scores(qc_ref, lambda hh: k_ref[hh]) + (r + sb[:, 0:LANES])
        r_ref[...] = r + sb[:, LANES:2 * LANES]
        m_old = m_ref[...]
        m_new = jnp.maximum(m_old, s)
        alpha = jnp.exp(m_old - m_new)
        p = jnp.exp(s - m_new)
        l_ref[...] = alpha * l_ref[...] + p
        m_ref[...] = m_new
        for hh in range(ATT_HEADS):
            acc_ref[hh] = alpha[hh:hh + 1, :] * acc_ref[hh] + p[hh:hh + 1, :] * v_ref[hh]


def _paged_attn_finish(qc_ref, knc_ref, vnc_ref, o_ref, m_ref, l_ref, acc_ref):
    m = m_ref[...]
    s_new = _paged_scores(qc_ref, lambda hh: knc_ref[_head_rows(hh), :])
    mx = jnp.maximum(jnp.max(m, axis=1, keepdims=True), s_new)
    w = jnp.exp(m - mx)
    w_new = jnp.exp(s_new - mx)
    denom = jnp.sum(l_ref[...] * w, axis=1, keepdims=True) + w_new
    for hh in range(ATT_HEADS):
        num = (jnp.sum(acc_ref[hh] * w[hh:hh + 1, :], axis=1, keepdims=True)
               + w_new[hh:hh + 1, :] * vnc_ref[_head_rows(hh), :])
        o_ref[0, _head_rows(hh), :] = num / denom[hh:hh + 1, :]


def _sample_out_body(x_ref, att_ref, yt_ref, xs_ref, z_ref, gates_ref, sst_ref, scw_ref, dsk_ref, ng_ref,
                     wo_ref, gpost_ref, gpre_ref, gmpost_ref, wup_ref, wdn_ref, o_ref, snew_ref, *, ff_chunk):
    n = x_ref.shape[0]
    y = yt_ref[...].T[0:n, :] + dsk_ref[...] * xs_ref[...]
    yssd = _rms(y * _silu(z_ref[...]), ng_ref[...])
    gates = gates_ref[...]
    u = gates[:, CONV_WIDTH:2 * CONV_WIDTH] * gates[:, 2 * CONV_WIDTH:3 * CONV_WIDTH]
    uc = scw_ref[0:1, :] * sst_ref[0] + scw_ref[1:2, :] * sst_ref[1] + scw_ref[2:3, :] * u
    yconv = gates[:, 0:CONV_WIDTH] * uc
    snew_ref[0] = sst_ref[1]
    snew_ref[1] = u
    dot = functools.partial(jnp.dot, preferred_element_type=f32)
    mix = (dot(att_ref[...].astype(bf16), wo_ref[0:ATT_WIDTH, :])
           + dot(yssd.astype(bf16), wo_ref[ATT_WIDTH:ATT_WIDTH + SSD_WIDTH, :])
           + dot(yconv.astype(bf16), wo_ref[ATT_WIDTH + SSD_WIDTH:D_MODEL, :]))
    o_ref[...] = _mlp_tail(x_ref[...], mix, gpost_ref[...], gpre_ref[...], gmpost_ref[...],
                           wup_ref, wdn_ref, ff_chunk)


def _sample_out(x, att, yt, xs, z, gates, sst, scw, dsk, ng, wo, gpost, gpre, gmpost, wup, wdn):
    n = x.shape[0]
    return pl.pallas_call(
        functools.partial(_sample_out_body, ff_chunk=min(512, wup.shape[2])),
        out_shape=(jax.ShapeDtypeStruct((n, D_MODEL), f32),
                   jax.ShapeDtypeStruct((SCONV - 1, n, CONV_WIDTH), f32)),
        compiler_params=pltpu.CompilerParams(vmem_limit_bytes=VMEM_LIMIT_BYTES),
        name="sample_out_mlp",
    )(x, att, yt, xs, z, gates, sst, scw, dsk, ng, wo, gpost, gpre, gmpost, wup, wdn)


def _prep_w_in(wt):
    offs = [0]
    for s in IN_SIZES:
        offs.append(offs[-1] + s)
    q, k, v, f, z, xbc, dt, gb, gc, hc = [wt[offs[i]:offs[i + 1]] for i in range(len(IN_SIZES))]
    small = jnp.concatenate([f, dt, jnp.zeros((LANES - ATT_HEADS - SSD_HEADS, wt.shape[1]), wt.dtype)], axis=0)
    return jnp.concatenate([q, k, v, z, xbc, gb, gc, hc, small], axis=0).astype(bf16)


def kernel(x_prompt, x_sample, cache_k, cache_v, cache_logf, state_ssm, state_ssd_conv, state_sconv, page_table,
           w_in, b_f, ssd_conv_w, ssd_conv_b, dt_bias, a_log, d_skip, ssd_norm_g, sconv_w, w_out,
           g_mix_pre, g_mix_post, g_mlp_pre, g_mlp_post, w_mlp_up, w_mlp_down):
    depth = w_in.shape[0]
    B, L, _ = x_prompt.shape
    n = x_sample.shape[0]
    assert x_sample.shape[1] == 1 and n <= LANES
    tm = _pick_tile(L, 512)
    tq = _pick_tile(L, 256)
    tk = _pick_tile(L, 512)

    kt_cache = jnp.transpose(cache_k, (0, 1, 3, 4, 2))
    vt_cache = jnp.transpose(cache_v, (0, 1, 3, 4, 2))
    page_terms = _page_bias(jnp.transpose(cache_logf, (0, 3, 1, 2)))

    w_in_t = jnp.transpose(w_in, (0, 2, 1))
    nsub = page_table.shape[1] // min(PAGES_PER_STEP, page_table.shape[1])
    fs = D_FF // nsub
    wup_all = jnp.transpose(w_mlp_up.reshape(depth, D_MODEL, nsub, fs), (0, 2, 1, 3)).astype(bf16)
    wdn_all = w_mlp_down.reshape(depth, nsub, fs, D_MODEL).astype(bf16)

    xp = x_prompt
    xs = x_sample.reshape(n, D_MODEL)
    kv_all = None
    outs_p = [[] for _ in range(6)]
    outs_s = [[] for _ in range(6)]
    row = lambda a: a.reshape(1, -1)
    lanes_bcast = lambda a: jnp.broadcast_to(a[:, :, None], a.shape + (LANES,))

    for l in range(depth):
        w = _prep_w_in(w_in_t[l])
        wo = w_out[l].astype(bf16)
        wup = wup_all[l]
        wdn = wdn_all[l]
        bias = jnp.concatenate([b_f[l], dt_bias[l], jnp.zeros((LANES - ATT_HEADS - SSD_HEADS,), f32)]).reshape(1, LANES)
        a_neg = -jnp.exp(a_log[l])
        a_row = jnp.concatenate([jnp.zeros((COL_DT,), f32), a_neg,
                                 jnp.zeros((LANES - COL_DT - SSD_HEADS,), f32)]).reshape(1, LANES)
        a_exp = jnp.repeat(a_neg, SSD_HEAD_DIM).reshape(1, SSD_WIDTH)
        dsk = jnp.repeat(d_skip[l], SSD_HEAD_DIM).reshape(1, SSD_WIDTH)
        ng = row(ssd_norm_g[l])
        cw, cb, scw = ssd_conv_w[l], row(ssd_conv_b[l]), sconv_w[l]
        gpre, gpost, gmpre, gmpost = row(g_mix_pre[l]), row(g_mix_post[l]), row(g_mlp_pre[l]), row(g_mlp_post[l])

        (qtp, kp, vtp, kt_all, vt_all, z, xbc, yconv, lfdt, lft, utail) = _prompt_in(
            xp, gpre, w, bias, scw, kv_all, l, depth, tm=tm, tq=tq, tk=tk)
        kv_all = (kt_all, vt_all)
        att = _prompt_attn(qtp, kp, vtp, tq=tq, tk=tk)
        yssd, s_fin = _prompt_ssd(xbc, lfdt, z, cw, cb, a_row, dsk, ng, tm=tm)

        cst = jnp.transpose(state_ssd_conv[l], (1, 0, 2))
        sst = jnp.transpose(state_sconv[l], (1, 0, 2))
        (q, k, v, lf, z_s, gates, xs_s, bm, cm, dtxt, dect, cnew, qt, knt, vnt) = _sample_in(
            xs, gpre, w, bias, cst, cw, cb, a_exp)
        s_new, yt = _sample_ssd(state_ssm[l], dtxt, dect, bm, cm)
        lf6 = lf[:, COL_LOGF:COL_LOGF + ATT_HEADS]
        lfn = lanes_bcast(jnp.pad(lf6, ((0, 0), (0, SUBLANES - ATT_HEADS))))

        xp, att_s = _out_mlp_attn(
            xp.reshape(B * L, D_MODEL), att.reshape(B * L, ATT_WIDTH), yssd.reshape(B * L, SSD_WIDTH),
            yconv.reshape(B * L, CONV_WIDTH), wo, gpost, gmpre, gmpost, wup, wdn,
            page_table, qt, knt, vnt, lfn, kt_cache, vt_cache, page_terms, l,
            tm=tm)
        xp = xp.reshape(B, L, D_MODEL)
        xs, snew = _sample_out(xs, att_s[:, :, 0], yt, xs_s, z_s, gates, sst, scw, dsk, ng,
                               wo, gpost, gmpre, gmpost, wup, wdn)

        outs_p[2].append(lft[:, 0:ATT_HEADS, :])
        outs_p[3].append(s_fin)
        outs_p[4].append(xbc[:, L - (SSD_CONV - 1):, :])
        outs_p[5].append(utail[:, SUBLANES - (SCONV - 1):, :])
        outs_s[0].append(k.reshape(n, 1, ATT_HEADS, HEAD_DIM))
        outs_s[1].append(v.reshape(n, 1, ATT_HEADS, HEAD_DIM))
        outs_s[2].append(lf6.reshape(n, 1, ATT_HEADS))
        outs_s[3].append(s_new)
        outs_s[4].append(jnp.transpose(cnew, (1, 0, 2)))
        outs_s[5].append(jnp.transpose(snew, (1, 0, 2)))

    heads_last = lambda a: jnp.transpose(a.reshape(depth, B, ATT_HEADS, HEAD_DIM, L), (0, 1, 4, 2, 3))
    k_p = heads_last(kv_all[0])
    v_p = heads_last(kv_all[1])
    lf_p = jnp.transpose(jnp.stack(outs_p[2]), (0, 1, 3, 2))
    rest_p = [jnp.stack(a) for a in outs_p[3:]]
    stacked_s = [jnp.stack(a) for a in outs_s]
    return (xp, xs.reshape(n, 1, D_MODEL), k_p, v_p, lf_p, *rest_p, *stacked_s)
```

```python
import functools

import jax
import jax.numpy as jnp
from jax import lax
from jax.experimental import pallas as pl
from jax.experimental.pallas import tpu as pltpu

f32 = jnp.float32
bf16 = jnp.bfloat16

D_MODEL = 1024
HEAD_DIM = 64
ATT_HEADS = 6
ATT_WIDTH = ATT_HEADS * HEAD_DIM
SSD_HEADS = 6
SSD_HEAD_DIM = 64
SSD_WIDTH = SSD_HEADS * SSD_HEAD_DIM
SSD_GROUPS = 2
D_STATE = 64
SSD_CONV = 4
XBC_WIDTH = SSD_WIDTH + 2 * SSD_GROUPS * D_STATE
CONV_WIDTH = 256
SCONV = 3
D_FF = 4 * D_MODEL
SSD_CHUNK = 128
PAGE_SIZE = 128
PAGES_PER_STEP = 16
RMS_EPS = 1e-6
IN_SIZES = (ATT_WIDTH, ATT_WIDTH, ATT_WIDTH, ATT_HEADS, SSD_WIDTH, XBC_WIDTH, SSD_HEADS,
            CONV_WIDTH, CONV_WIDTH, CONV_WIDTH)

LANES = 128
SUBLANES = 8
VMEM_LIMIT_BYTES = 56 * 1024 * 1024

OFF_Q = 0
OFF_K = OFF_Q + ATT_WIDTH
OFF_V = OFF_K + ATT_WIDTH
OFF_Z = OFF_V + ATT_WIDTH
OFF_XBC = OFF_Z + SSD_WIDTH
OFF_GB = OFF_XBC + XBC_WIDTH
OFF_GC = OFF_GB + CONV_WIDTH
OFF_HC = OFF_GC + CONV_WIDTH
OFF_SMALL = OFF_HC + CONV_WIDTH
PROJ_WIDTH = OFF_SMALL + LANES
COL_LOGF = 0
COL_DT = ATT_HEADS
PADDED_HEAD = LANES
NEG_BIG = -1e30
LOG2E = 1.4426950408889634
VALUE_ROWS = 80


def _rms(x, g):
    var = jnp.mean(x * x, axis=-1, keepdims=True)
    return (x * lax.rsqrt(var + RMS_EPS)) * g


def _silu(x):
    return x * (1.0 / (1.0 + jnp.exp(-x)))


def _split3(a):
    a1 = a.astype(bf16)
    r1 = a - a1.astype(f32)
    a2 = r1.astype(bf16)
    a3 = (r1 - a2.astype(f32)).astype(bf16)
    return a1, a2, a3


def _exact_dot(m01, a):
    a1, a2, a3 = _split3(a)
    d = functools.partial(jnp.dot, preferred_element_type=f32)
    return d(m01, a1) + d(m01, a2) + d(m01, a3)


def _exact_dot_rhs(a, m01):
    a1, a2, a3 = _split3(a)
    d = functools.partial(jnp.dot, preferred_element_type=f32)
    return d(a1, m01) + d(a2, m01) + d(a3, m01)


def _dot_nt(a, b):
    return lax.dot_general(a, b, (((1,), (1,)), ((), ())), preferred_element_type=f32)


def _softplus_parts(t):
    sp = jnp.log1p(jnp.exp(-jnp.abs(t)))
    return jnp.minimum(t, 0.0) - sp, jnp.maximum(t, 0.0) + sp


def _shift_rows(x, k, tail):
    row = lax.broadcasted_iota(jnp.int32, x.shape, 0)
    y = pltpu.roll(x, k, 0)
    for r in range(k):
        y = jnp.where(row == r, tail[SUBLANES - k + r:SUBLANES - k + r + 1, :], y)
    return y


def _pick_tile(n, pref):
    t = min(n, pref)
    while n % t:
        t -= SUBLANES
    assert t > 0
    return t


def _prompt_in_body(x_ref, g_ref, w_ref, bias_ref, tri_ref, scw_ref, *refs, tq, tk, carried):
    _prompt_in_compute(x_ref, g_ref, w_ref, bias_ref, tri_ref, scw_ref, *refs[carried:], tq=tq, tk=tk)


def _prompt_in_compute(x_ref, g_ref, w_ref, bias_ref, tri_ref, scw_ref,
                    qtp_ref, kp_ref, vtp_ref, kt32_ref, vt32_ref, z_ref, xbc_ref, yconv_ref,
                    lfdt_ref, lft_ref, utail_ref,
                    carry_ref, tail_ref, *, tq, tk):
    tm = x_ref.shape[1]

    @pl.when(pl.program_id(1) == 0)
    def _():
        carry_ref[...] = jnp.zeros_like(carry_ref)
        tail_ref[...] = jnp.zeros_like(tail_ref)

    h = _rms(x_ref[0], g_ref[...]).astype(bf16)
    proj = lambda lo, hi: _dot_nt(h, w_ref[lo:hi, :])

    t = proj(OFF_SMALL, PROJ_WIDTH) + bias_ref[...]
    logf, dt = _softplus_parts(t)
    col = lax.broadcasted_iota(jnp.int32, t.shape, 1)
    lfdt_ref[0] = jnp.where(col < COL_DT, logf, dt)
    lft_ref[0] = logf.T[0:SUBLANES, :]
    c = _exact_dot(tri_ref[...], logf) + carry_ref[...]
    carry_ref[...] = c[tm - 1:tm, :]
    c = c * LOG2E
    c1, c2, c3 = [p.astype(f32) for p in _split3(c)]
    ct1, ct2, ct3 = [p.astype(f32) for p in _split3(c.T[0:SUBLANES, :])]

    qkv_t = _dot_nt(w_ref[OFF_Q:OFF_Z, :], h)
    kt = qkv_t[ATT_WIDTH:2 * ATT_WIDTH]
    vt = qkv_t[2 * ATT_WIDTH:3 * ATT_WIDTH]
    kt32_ref[0] = kt
    vt32_ref[0] = vt
    k = kt.T
    lane = lax.broadcasted_iota(jnp.int32, (tm, PADDED_HEAD - HEAD_DIM), 1)
    row = lax.broadcasted_iota(jnp.int32, (PADDED_HEAD - HEAD_DIM, tm), 0)
    vrow = lax.broadcasted_iota(jnp.int32, (VALUE_ROWS - HEAD_DIM, tm), 0)
    ones_row = jnp.where(vrow == 0, 1.0, 0.0).astype(bf16)
    for hh in range(ATT_HEADS):
        src = slice(HEAD_DIM * hh, HEAD_DIM * (hh + 1))
        lo = slice(PADDED_HEAD * hh, PADDED_HEAD * hh + HEAD_DIM)
        hi = slice(PADDED_HEAD * hh + HEAD_DIM, PADDED_HEAD * (hh + 1))
        hc = slice(hh, hh + 1)
        kp_ref[0, :, lo] = k[:, src].astype(bf16)
        ek = jnp.where(lane < 3, 1.0, jnp.where(lane == 3, -c1[:, hc], jnp.where(
            lane == 4, -c2[:, hc], jnp.where(lane == 5, -c3[:, hc], 0.0))))
        kp_ref[0, :, hi] = ek.astype(bf16)
        eq = jnp.where(row == 0, ct1[hc, :], jnp.where(row == 1, ct2[hc, :], jnp.where(
            row == 2, ct3[hc, :], jnp.where(row < 6, 1.0, 0.0)))).astype(bf16)
        qh = (qkv_t[src] * (HEAD_DIM ** -0.5 * LOG2E)).astype(bf16)
        for i in range(tm // tq):
            cs = slice(i * tq, (i + 1) * tq)
            qtp_ref[0, i, lo, :] = qh[:, cs]
            qtp_ref[0, i, hi, :] = eq[:, cs]
        vh = vt[src].astype(bf16)
        for i in range(tm // tk):
            cs = slice(i * tk, (i + 1) * tk)
            vtp_ref[0, i, VALUE_ROWS * hh:VALUE_ROWS * hh + HEAD_DIM, :] = vh[:, cs]
            vtp_ref[0, i, VALUE_ROWS * hh + HEAD_DIM:VALUE_ROWS * (hh + 1), :] = ones_row[:, cs]

    z_ref[0] = proj(OFF_Z, OFF_XBC)
    xbc_ref[0] = proj(OFF_XBC, OFF_GB)

    gates = proj(OFF_GB, OFF_SMALL)
    gb = gates[:, 0:CONV_WIDTH]
    u = gates[:, CONV_WIDTH:2 * CONV_WIDTH] * gates[:, 2 * CONV_WIDTH:3 * CONV_WIDTH]
    tail = tail_ref[...]
    uc = (scw_ref[0:1, :] * _shift_rows(u, 2, tail) + scw_ref[1:2, :] * _shift_rows(u, 1, tail)
          + scw_ref[2:3, :] * u)
    yconv_ref[0] = (gb * uc).astype(bf16)
    tail_ref[...] = u[tm - SUBLANES:tm, :]
    utail_ref[0] = u[tm - SUBLANES:tm, :]


def _prompt_in(x, g, w, bias, scw, kv_all, layer, depth, *, tm, tq, tk):
    B, L, _ = x.shape
    nj = L // tm
    pw = ATT_HEADS * PADDED_HEAD
    tri = jnp.tril(jnp.ones((tm, tm), f32)).astype(bf16)
    row = lambda width: pl.BlockSpec((1, tm, width), lambda b, j: (b, j, 0))
    colb = lambda height: pl.BlockSpec((1, height, tm), lambda b, j: (b, 0, j))
    const = lambda shape: pl.BlockSpec(shape, lambda b, j: (0,) * len(shape))
    layer_colb = pl.BlockSpec((None, 1, ATT_WIDTH, tm), lambda b, j: (layer, b, 0, j))
    carried = () if kv_all is None else tuple(kv_all)
    out_shapes = (
        jax.ShapeDtypeStruct((B, L // tq, pw, tq), bf16),
        jax.ShapeDtypeStruct((B, L, pw), bf16),
        jax.ShapeDtypeStruct((B, L // tk, ATT_HEADS * VALUE_ROWS, tk), bf16),
        jax.ShapeDtypeStruct((depth, B, ATT_WIDTH, L), f32),
        jax.ShapeDtypeStruct((depth, B, ATT_WIDTH, L), f32),
        jax.ShapeDtypeStruct((B, L, SSD_WIDTH), f32),
        jax.ShapeDtypeStruct((B, L, XBC_WIDTH), f32),
        jax.ShapeDtypeStruct((B, L, CONV_WIDTH), bf16),
        jax.ShapeDtypeStruct((B, L, LANES), f32),
        jax.ShapeDtypeStruct((B, SUBLANES, L), f32),
        jax.ShapeDtypeStruct((B, SUBLANES, CONV_WIDTH), f32),
    )
    out_specs = (
        pl.BlockSpec((1, tm // tq, pw, tq), lambda b, j: (b, j, 0, 0)),
        row(pw),
        pl.BlockSpec((1, tm // tk, ATT_HEADS * VALUE_ROWS, tk), lambda b, j: (b, j, 0, 0)),
        layer_colb, layer_colb, row(SSD_WIDTH), row(XBC_WIDTH), row(CONV_WIDTH),
        row(LANES), colb(SUBLANES),
        pl.BlockSpec((1, SUBLANES, CONV_WIDTH), lambda b, j: (b, 0, 0)),
    )
    return pl.pallas_call(
        functools.partial(_prompt_in_body, tq=tq, tk=tk, carried=len(carried)),
        out_shape=out_shapes,
        grid=(B, nj),
        in_specs=[row(D_MODEL), const((1, D_MODEL)), const((PROJ_WIDTH, D_MODEL)), const((1, LANES)),
                  const((tm, tm)), const((SCONV, CONV_WIDTH))] + [pl.BlockSpec(memory_space=pl.ANY)] * len(carried),
        input_output_aliases={6 + i: 3 + i for i in range(len(carried))},
        out_specs=out_specs,
        scratch_shapes=[pltpu.VMEM((1, LANES), f32), pltpu.VMEM((SUBLANES, CONV_WIDTH), f32)],
        compiler_params=pltpu.CompilerParams(dimension_semantics=("arbitrary", "arbitrary"),
                                             vmem_limit_bytes=VMEM_LIMIT_BYTES),
        name="prompt_in",
    )(x, g, w, bias, tri, scw, *carried)


def _prompt_attn_body(qt_ref, kp_ref, vt_ref, o_ref, m_ref, acc_ref, st_ref, *, tq, tk):
    qi = pl.program_id(1)
    m_ref[...] = jnp.full(m_ref.shape, NEG_BIG, f32)
    acc_ref[...] = jnp.zeros_like(acc_ref)
    nfull = lax.div(qi * tq, tk)
    heads = [slice(PADDED_HEAD * hh, PADDED_HEAD * (hh + 1)) for hh in range(ATT_HEADS)]
    vrows = [slice(VALUE_ROWS * hh, VALUE_ROWS * (hh + 1)) for hh in range(ATT_HEADS)]

    def score(kj, slot, masked):
        ks = pl.multiple_of(kj * tk, tk)
        if masked:
            kpos = ks + lax.broadcasted_iota(jnp.int32, (tk, tq), 0)
            qpos = qi * tq + lax.broadcasted_iota(jnp.int32, (tk, tq), 1)
            valid = kpos <= qpos
        for hh, hs in enumerate(heads):
            st = jnp.dot(kp_ref[0, pl.ds(ks, tk), hs], qt_ref[0, 0, hs, :], preferred_element_type=f32)
            st_ref[slot, hh] = jnp.where(valid, st, NEG_BIG) if masked else st

    def absorb(kj, slot):
        probs = []
        for hh in range(ATT_HEADS):
            st = st_ref[slot, hh]
            m_old = m_ref[hh]
            m_new = jnp.maximum(m_old, jnp.max(st, axis=0, keepdims=True))
            m_ref[hh] = m_new
            probs.append((jnp.exp2(m_old[0:1] - m_new[0:1]), jnp.exp2(st - m_new[0:1]).astype(bf16)))
        for hh, (alpha, pt) in enumerate(probs):
            acc_ref[hh] = alpha * acc_ref[hh] + jnp.dot(vt_ref[0, kj, vrows[hh], :], pt,
                                                        preferred_element_type=f32)

    @pl.when(nfull > 0)
    def _():
        score(0, 0, False)

    def pair(pi, carry):
        k0 = 2 * pi
        score(k0 + 1, 1, False)
        absorb(k0, 0)
        score(k0 + 2, 0, False)
        absorb(k0 + 1, 1)
        return carry

    ntrip = lax.div(jnp.maximum(nfull - 1, 0), 2)
    lax.fori_loop(0, ntrip, pair, 0)
    done = 2 * ntrip
    rem = nfull - done

    @pl.when(rem == 2)
    def _():
        score(done + 1, 1, False)
        absorb(done, 0)
        score(done + 2, 0, True)
        absorb(done + 1, 1)
        absorb(done + 2, 0)

    @pl.when(rem == 1)
    def _():
        score(done + 1, 1, True)
        absorb(done, 0)
        absorb(done + 1, 1)

    @pl.when(rem == 0)
    def _():
        score(0, 0, True)
        absorb(0, 0)

    for hh in range(ATT_HEADS):
        a = acc_ref[hh]
        o_ref[0, HEAD_DIM * hh:HEAD_DIM * (hh + 1), :] = (a[0:HEAD_DIM] / a[HEAD_DIM:HEAD_DIM + 1]).astype(bf16)


def _prompt_attn(qtp, kp, vtp, *, tq, tk):
    B, nq, W, _ = qtp.shape
    L = kp.shape[1]
    vw = ATT_HEADS * VALUE_ROWS
    return pl.pallas_call(
        functools.partial(_prompt_attn_body, tq=tq, tk=tk),
        out_shape=jax.ShapeDtypeStruct((B, ATT_WIDTH, L), bf16),
        grid=(B, nq),
        in_specs=[pl.BlockSpec((1, 1, W, tq), lambda b, i: (b, i, 0, 0)),
                  pl.BlockSpec((1, L, W), lambda b, i: (b, 0, 0)),
                  pl.BlockSpec((1, L // tk, vw, tk), lambda b, i: (b, 0, 0, 0))],
        out_specs=pl.BlockSpec((1, ATT_WIDTH, tq), lambda b, i: (b, 0, i)),
        scratch_shapes=[pltpu.VMEM((ATT_HEADS, SUBLANES, tq), f32),
                        pltpu.VMEM((ATT_HEADS, VALUE_ROWS, tq), f32),
                        pltpu.VMEM((2, ATT_HEADS, tk, tq), f32)],
        compiler_params=pltpu.CompilerParams(dimension_semantics=("arbitrary", "arbitrary"),
                                             vmem_limit_bytes=VMEM_LIMIT_BYTES),
        name="prompt_attn",
    )(qtp, kp, vtp)


def _prompt_ssd_body(xbc_ref, lfdt_ref, z_ref, cw_ref, cb_ref, a_ref, dsk_ref, ng_ref, tri_ref,
                     y_ref, sout_ref, s_ref, tail_ref):
    tm = xbc_ref.shape[1]
    Q = SSD_CHUNK

    @pl.when(pl.program_id(1) == 0)
    def _():
        s_ref[...] = jnp.zeros_like(s_ref)
        tail_ref[...] = jnp.zeros_like(tail_ref)

    x = xbc_ref[0]
    tail = tail_ref[...]
    xc = (cw_ref[0:1, :] * _shift_rows(x, 3, tail) + cw_ref[1:2, :] * _shift_rows(x, 2, tail)
          + cw_ref[2:3, :] * _shift_rows(x, 1, tail) + cw_ref[3:4, :] * x + cb_ref[...])
    xc = _silu(xc)
    tail_ref[...] = x[tm - SUBLANES:tm, :]

    dt_all = lfdt_ref[0]
    dta_all = dt_all * a_ref[...]
    row = lax.broadcasted_iota(jnp.int32, (Q, Q), 0)
    colm = lax.broadcasted_iota(jnp.int32, (Q, Q), 1)
    causal = row >= colm
    tri = tri_ref[...]
    group = lambda hh: slice((hh // (SSD_HEADS // SSD_GROUPS)) * D_STATE,
                             (hh // (SSD_HEADS // SSD_GROUPS) + 1) * D_STATE)
    head = lambda hh: slice(hh * SSD_HEAD_DIM, (hh + 1) * SSD_HEAD_DIM)

    states = [s_ref[hh] for hh in range(SSD_HEADS)]
    for c in range(tm // Q):
        rs = slice(c * Q, (c + 1) * Q)
        dt = dt_all[rs]
        acum = _exact_dot(tri, dta_all[rs])
        acum_t = acum.T
        dt_t = dt.T
        xs = xc[rs, 0:SSD_WIDTH]
        xs_t = xs.T.astype(bf16)
        xs_b = xs.astype(bf16)
        bm = xc[rs, SSD_WIDTH:SSD_WIDTH + SSD_GROUPS * D_STATE]
        cm = xc[rs, SSD_WIDTH + SSD_GROUPS * D_STATE:XBC_WIDTH].astype(bf16)
        cb = []
        for g in range(SSD_GROUPS):
            gs = slice(g * D_STATE, (g + 1) * D_STATE)
            cb.append(_dot_nt(cm[:, gs], bm[:, gs].astype(bf16)))
        terms = []
        for hh in range(SSD_HEADS):
            cc = COL_DT + hh
            a_col = acum[:, cc:cc + 1]
            a_row = acum_t[cc:cc + 1, :]
            a_last = acum[Q - 1:Q, cc:cc + 1]
            decay = jnp.exp(jnp.where(causal, a_col - a_row, -jnp.inf))
            mat = (cb[hh // (SSD_HEADS // SSD_GROUPS)] * decay * dt_t[cc:cc + 1, :]).astype(bf16)
            wcol = jnp.exp(a_last - a_col) * dt[:, cc:cc + 1]
            bw = (bm[:, group(hh)] * wcol).astype(bf16)
            terms.append((mat, bw, jnp.exp(a_col), jnp.exp(a_last)))
        ys = []
        for hh, (mat, bw, e_col, e_last) in enumerate(terms):
            s_in = states[hh]
            y_intra = jnp.dot(mat, xs_b[:, head(hh)], preferred_element_type=f32)
            y_inter = _dot_nt(cm[:, group(hh)], s_in.astype(bf16)) * e_col
            ys.append(y_intra + y_inter)
            states[hh] = e_last * s_in + jnp.dot(xs_t[head(hh), :], bw, preferred_element_type=f32)
        y = jnp.concatenate(ys, axis=1) + dsk_ref[...] * xs
        gated = y * _silu(z_ref[0, rs, :])
        y_ref[0, rs, :] = _rms(gated, ng_ref[...]).astype(bf16)
    for hh in range(SSD_HEADS):
        s_ref[hh] = states[hh]
    sout_ref[0] = jnp.stack(states)


def _prompt_ssd(xbc, lfdt, z, cw, cb, a_row, dsk, ng, *, tm):
    B, L, _ = xbc.shape
    tri = jnp.tril(jnp.ones((SSD_CHUNK, SSD_CHUNK), f32)).astype(bf16)
    row = lambda width: pl.BlockSpec((1, tm, width), lambda b, j: (b, j, 0))
    const = lambda shape: pl.BlockSpec(shape, lambda b, j: (0,) * len(shape))
    return pl.pallas_call(
        _prompt_ssd_body,
        out_shape=(jax.ShapeDtypeStruct((B, L, SSD_WIDTH), bf16),
                   jax.ShapeDtypeStruct((B, SSD_HEADS, SSD_HEAD_DIM, D_STATE), f32)),
        grid=(B, L // tm),
        in_specs=[row(XBC_WIDTH), row(LANES), row(SSD_WIDTH), const((SSD_CONV, XBC_WIDTH)),
                  const((1, XBC_WIDTH)), const((1, LANES)), const((1, SSD_WIDTH)), const((1, SSD_WIDTH)),
                  const((SSD_CHUNK, SSD_CHUNK))],
        out_specs=(row(SSD_WIDTH),
                   pl.BlockSpec((1, SSD_HEADS, SSD_HEAD_DIM, D_STATE), lambda b, j: (b, 0, 0, 0))),
        scratch_shapes=[pltpu.VMEM((SSD_HEADS, SSD_HEAD_DIM, D_STATE), f32),
                        pltpu.VMEM((SUBLANES, XBC_WIDTH), f32)],
        compiler_params=pltpu.CompilerParams(dimension_semantics=("arbitrary", "arbitrary"),
                                             vmem_limit_bytes=VMEM_LIMIT_BYTES),
        name="prompt_ssd",
    )(xbc, lfdt, z, cw, cb, a_row, dsk, ng, tri)


def _mlp_tail(x, mix, gpost, gpre, gmpost, wup_ref, wdn_ref, ff_chunk):
    x1 = x + _rms(mix, gpost)
    hmid = _rms(x1, gpre).astype(bf16)
    acc = jnp.zeros_like(x1)
    nsub, _, fs = wup_ref.shape
    for j in range(nsub):
        for c in range(fs // ff_chunk):
            cs = slice(c * ff_chunk, (c + 1) * ff_chunk)
            up = jnp.dot(hmid, wup_ref[j, :, cs], preferred_element_type=f32)
            act = jnp.square(jnp.maximum(up, 0.0)).astype(bf16)
            acc = acc + jnp.dot(act, wdn_ref[j, cs, :], preferred_element_type=f32)
    return x1 + _rms(acc, gmpost)


def _out_mlp_attn_body(pt_ref, x_ref, att_ref, yssd_ref, yconv_ref, wo_ref, gpost_ref, gpre_ref, gmpost_ref,
                       wup_ref, wdn_ref, qt_ref, knt_ref, vnt_ref, lfn_ref, *rest, pps, ff_chunk):
    k_refs = rest[0:pps]
    v_refs = rest[pps:2 * pps]
    sb_refs = rest[2 * pps:3 * pps]
    o_ref, oa_ref, x1_ref, h_ref, acc_ref, m_ref, l_ref, acca_ref, r_ref, qc_ref, knc_ref, vnc_ref = rest[3 * pps:]
    j = pl.program_id(1)
    last = pl.num_programs(1) - 1
    dot = functools.partial(jnp.dot, preferred_element_type=f32)

    @pl.when(j == 0)
    def _():
        att_mix = lax.dot_general(att_ref[0], wo_ref[0:ATT_WIDTH, :], (((0,), (0,)), ((), ())),
                                  preferred_element_type=f32)
        mix = (att_mix
               + dot(yssd_ref[...], wo_ref[ATT_WIDTH:ATT_WIDTH + SSD_WIDTH, :])
               + dot(yconv_ref[...], wo_ref[ATT_WIDTH + SSD_WIDTH:D_MODEL, :]))
        x1 = x_ref[...] + _rms(mix, gpost_ref[...])
        x1_ref[...] = x1
        h_ref[...] = _rms(x1, gpre_ref[...]).astype(bf16)
        acc_ref[...] = jnp.zeros_like(acc_ref)
        _paged_attn_init(lfn_ref, m_ref, l_ref, acca_ref, r_ref)
        mine = lax.broadcasted_iota(jnp.int32, qt_ref.shape, 1) == pl.program_id(0)
        for src, dst in ((qt_ref, qc_ref), (knt_ref, knc_ref), (vnt_ref, vnc_ref)):
            col = jnp.sum(jnp.where(mine, src[...], 0.0), axis=1, keepdims=True)
            dst[...] = jnp.broadcast_to(col, dst.shape)

    nchunk = wup_ref.shape[2] // ff_chunk
    half = ff_chunk // 2
    groups = 4 * nchunk
    per = -(-pps // groups)
    pages = lambda g: _paged_attn_pages(qc_ref, k_refs[g * per:(g + 1) * per], v_refs[g * per:(g + 1) * per],
                                        sb_refs[g * per:(g + 1) * per], m_ref, l_ref, acca_ref, r_ref)
    relu2 = lambda u: jnp.square(jnp.maximum(u, 0.0)).astype(bf16)
    hmid = h_ref[...]
    acc = acc_ref[...]
    for c in range(nchunk):
        lo = slice(c * ff_chunk, c * ff_chunk + half)
        hi = slice(c * ff_chunk + half, (c + 1) * ff_chunk)
        up_lo = dot(hmid, wup_ref[j, :, lo])
        pages(4 * c)
        up_hi = dot(hmid, wup_ref[j, :, hi])
        pages(4 * c + 1)
        acc = acc + dot(relu2(up_lo), wdn_ref[j, lo, :])
        pages(4 * c + 2)
        acc = acc + dot(relu2(up_hi), wdn_ref[j, hi, :])
        pages(4 * c + 3)
    acc_ref[...] = acc

    @pl.when(j == last)
    def _():
        o_ref[...] = x1_ref[...] + _rms(acc_ref[...], gmpost_ref[...])
        _paged_attn_finish(qc_ref, knc_ref, vnc_ref, oa_ref, m_ref, l_ref, acca_ref)


def _out_mlp_attn(x, att, yssd, yconv, wo, gpost, gpre, gmpost, wup, wdn,
                  page_table, qt, knt, vnt, lfn, kt, vt, sb, layer, *, tm):
    M = x.shape[0]
    n, n_pages = page_table.shape
    pps = min(PAGES_PER_STEP, n_pages)
    nsub = n_pages // pps
    assert M // tm == n and n_pages % pps == 0 and D_FF % nsub == 0
    fs = D_FF // nsub
    assert wup.shape == (nsub, D_MODEL, fs) and wdn.shape == (nsub, fs, D_MODEL)
    pt = page_table.reshape(-1)
    row = lambda width: pl.BlockSpec((tm, width), lambda i, j, pt: (i, 0))
    nlb = att.shape[2] // tm
    att_spec = pl.BlockSpec((1, ATT_WIDTH, tm), lambda i, j, pt: (i // nlb, 0, i % nlb))
    const = lambda shape: pl.BlockSpec(shape, lambda i, j, pt: (0,) * len(shape), pipeline_mode=pl.Buffered(1))
    seq = lambda rows: pl.BlockSpec((1, rows, LANES), lambda i, j, pt: (i, 0, 0))

    def page(i, j, pt, k):
        return pt[i * n_pages + n_pages - 1 - (j * pps + k)]

    kv_specs = [pl.BlockSpec((None, None, ATT_HEADS, HEAD_DIM, PAGE_SIZE),
                             lambda i, j, pt, k=k: (layer, page(i, j, pt, k), 0, 0, 0)) for k in range(pps)]
    sb_specs = [pl.BlockSpec((None, None, SUBLANES, 2 * LANES),
                             lambda i, j, pt, k=k: (layer, page(i, j, pt, k), 0, 0)) for k in range(pps)]
    return pl.pallas_call(
        functools.partial(_out_mlp_attn_body, pps=pps, ff_chunk=min(512, fs)),
        out_shape=(jax.ShapeDtypeStruct((M, D_MODEL), f32), jax.ShapeDtypeStruct((n, ATT_WIDTH, LANES), f32)),
        grid_spec=pltpu.PrefetchScalarGridSpec(
            num_scalar_prefetch=1,
            grid=(n, nsub),
            in_specs=[row(D_MODEL), att_spec, row(SSD_WIDTH), row(CONV_WIDTH),
                      const((D_MODEL, D_MODEL)), const((1, D_MODEL)), const((1, D_MODEL)), const((1, D_MODEL)),
                      const((nsub, D_MODEL, fs)), const((nsub, fs, D_MODEL)),
                      const((ATT_WIDTH, LANES)), const((ATT_WIDTH, LANES)), const((ATT_WIDTH, LANES)), seq(SUBLANES)]
            + kv_specs + kv_specs + sb_specs,
            out_specs=(row(D_MODEL), seq(ATT_WIDTH)),
            scratch_shapes=[pltpu.VMEM((tm, D_MODEL), f32), pltpu.VMEM((tm, D_MODEL), bf16),
                            pltpu.VMEM((tm, D_MODEL), f32),
                            pltpu.VMEM((SUBLANES, LANES), f32), pltpu.VMEM((SUBLANES, LANES), f32),
                            pltpu.VMEM((ATT_HEADS, HEAD_DIM, LANES), f32), pltpu.VMEM((SUBLANES, LANES), f32),
                            pltpu.VMEM((ATT_WIDTH, LANES), f32), pltpu.VMEM((ATT_WIDTH, LANES), f32),
                            pltpu.VMEM((ATT_WIDTH, LANES), f32)],
        ),
        compiler_params=pltpu.CompilerParams(dimension_semantics=("arbitrary", "arbitrary"),
                                             vmem_limit_bytes=VMEM_LIMIT_BYTES),
        name="out_mlp_sample_attn",
    )(pt, x, att, yssd, yconv, wo, gpost, gpre, gmpost, wup, wdn, qt, knt, vnt, lfn,
      *([kt] * pps), *([vt] * pps), *([sb] * pps))


def _sample_in_body(x_ref, g_ref, w_ref, bias_ref, cst_ref, cw_ref, cb_ref, aexp_ref,
                    q_ref, k_ref, v_ref, lf_ref, z_ref, gates_ref, xs_ref, bm_ref, cm_ref,
                    dtxt_ref, dect_ref, cnew_ref, qt_ref, kt_ref, vt_ref):
    n = x_ref.shape[0]
    h = _rms(x_ref[...], g_ref[...]).astype(bf16)
    proj = _dot_nt(h, w_ref[...])
    q_ref[...] = proj[:, OFF_Q:OFF_K] * (HEAD_DIM ** -0.5)
    k_ref[...] = proj[:, OFF_K:OFF_V]
    v_ref[...] = proj[:, OFF_V:OFF_Z]
    z_ref[...] = proj[:, OFF_Z:OFF_XBC]
    gates_ref[...] = proj[:, OFF_GB:OFF_SMALL]
    logf, dt = _softplus_parts(proj[:, OFF_SMALL:PROJ_WIDTH] + bias_ref[...])
    lf_ref[...] = logf

    xbc = proj[:, OFF_XBC:OFF_GB]
    xc = (cw_ref[0:1, :] * cst_ref[0] + cw_ref[1:2, :] * cst_ref[1] + cw_ref[2:3, :] * cst_ref[2]
          + cw_ref[3:4, :] * xbc + cb_ref[...])
    xc = _silu(xc)
    cnew_ref[0] = cst_ref[1]
    cnew_ref[1] = cst_ref[2]
    cnew_ref[2] = xbc
    xs = xc[:, 0:SSD_WIDTH]
    xs_ref[...] = xs
    bm_ref[...] = xc[:, SSD_WIDTH:SSD_WIDTH + SSD_GROUPS * D_STATE]
    cm_ref[...] = xc[:, SSD_WIDTH + SSD_GROUPS * D_STATE:XBC_WIDTH]

    head = lax.broadcasted_iota(jnp.int32, (n, SSD_WIDTH), 1) // SSD_HEAD_DIM
    dt_exp = jnp.zeros((n, SSD_WIDTH), f32)
    for hh in range(SSD_HEADS):
        dt_exp = jnp.where(head == hh, dt[:, COL_DT + hh:COL_DT + hh + 1], dt_exp)
    pad = jnp.zeros((LANES - n, SSD_WIDTH), f32)
    lanes_t = lambda a: jnp.concatenate([a, pad], axis=0).T
    dtxt_ref[...] = lanes_t(dt_exp * xs)
    dect_ref[...] = lanes_t(jnp.exp(dt_exp * aexp_ref[...]))
    qt_ref[...] = lanes_t(q_ref[...])
    kt_ref[...] = lanes_t(k_ref[...])
    vt_ref[...] = lanes_t(v_ref[...])


def _sample_in(x, g, w, bias, cst, cw, cb, aexp):
    n = x.shape[0]
    s = lambda *shape: jax.ShapeDtypeStruct(shape, f32)
    return pl.pallas_call(
        _sample_in_body,
        out_shape=(s(n, ATT_WIDTH), s(n, ATT_WIDTH), s(n, ATT_WIDTH), s(n, LANES), s(n, SSD_WIDTH),
                   s(n, 3 * CONV_WIDTH), s(n, SSD_WIDTH), s(n, SSD_GROUPS * D_STATE), s(n, SSD_GROUPS * D_STATE),
                   s(SSD_WIDTH, LANES), s(SSD_WIDTH, LANES), s(SSD_CONV - 1, n, XBC_WIDTH),
                   s(ATT_WIDTH, LANES), s(ATT_WIDTH, LANES), s(ATT_WIDTH, LANES)),
        compiler_params=pltpu.CompilerParams(vmem_limit_bytes=VMEM_LIMIT_BYTES),
        name="sample_in",
    )(x, g, w, bias, cst, cw, cb, aexp)


def _sample_ssd_body(s_ref, dtxt_ref, dect_ref, bm_ref, cm_ref, snew_ref, yt_ref):
    b = pl.program_id(0)
    rows = SSD_WIDTH
    half = rows // SSD_GROUPS
    lane = lax.broadcasted_iota(jnp.int32, (rows, LANES), 1)
    sel = lane == b
    dcol = jnp.sum(jnp.where(sel, dect_ref[...], 0.0), axis=1, keepdims=True)
    xcol = jnp.sum(jnp.where(sel, dtxt_ref[...], 0.0), axis=1, keepdims=True)

    def expand(ref):
        r = ref[pl.ds(b, 1), :]
        return jnp.concatenate([jnp.broadcast_to(r[:, g * D_STATE:(g + 1) * D_STATE], (half, D_STATE))
                                for g in range(SSD_GROUPS)], axis=0)

    s_old = s_ref[0].reshape(rows, D_STATE)
    s_new = dcol * s_old + xcol * expand(bm_ref)
    snew_ref[0] = s_new.reshape(SSD_HEADS, SSD_HEAD_DIM, D_STATE)
    ycol = jnp.sum(s_new * expand(cm_ref), axis=1, keepdims=True)

    @pl.when(b == 0)
    def _():
        yt_ref[...] = jnp.zeros_like(yt_ref)

    yt_ref[...] += jnp.where(sel, ycol, 0.0)


def _sample_ssd(state, dtxt, dect, bm, cm):
    n = state.shape[0]
    const = lambda shape: pl.BlockSpec(shape, lambda b: (0, 0))
    sblk = pl.BlockSpec((1, SSD_HEADS, SSD_HEAD_DIM, D_STATE), lambda b: (b, 0, 0, 0))
    return pl.pallas_call(
        _sample_ssd_body,
        out_shape=(jax.ShapeDtypeStruct(state.shape, f32), jax.ShapeDtypeStruct((SSD_WIDTH, LANES), f32)),
        grid=(n,),
        in_specs=[sblk, const((SSD_WIDTH, LANES)), const((SSD_WIDTH, LANES)),
                  const(bm.shape), const(cm.shape)],
        out_specs=(sblk, const((SSD_WIDTH, LANES))),
        compiler_params=pltpu.CompilerParams(dimension_semantics=("arbitrary",)),
        name="sample_ssd",
    )(state, dtxt, dect, bm, cm)


def _page_bias_body(lf_ref, upper_ref, o_ref):
    pb = lf_ref.shape[1]
    for hh in range(ATT_HEADS):
        x = lf_ref[hh]
        o_ref[:, hh, 0:LANES] = _exact_dot_rhs(x, upper_ref[...])
        o_ref[:, hh, LANES:2 * LANES] = jnp.broadcast_to(jnp.sum(x, axis=1, keepdims=True), x.shape)
    o_ref[:, ATT_HEADS:SUBLANES, :] = jnp.zeros((pb, SUBLANES - ATT_HEADS, 2 * LANES), f32)


def _page_bias(lft):
    depth, _, pool, _ = lft.shape
    pb = _pick_tile(pool, 256)
    pos = jnp.arange(PAGE_SIZE)
    upper = (pos[:, None] > pos[None, :]).astype(bf16)
    return pl.pallas_call(
        _page_bias_body,
        out_shape=jax.ShapeDtypeStruct((depth, pool, SUBLANES, 2 * LANES), f32),
        grid=(depth, pool // pb),
        in_specs=[pl.BlockSpec((None, ATT_HEADS, pb, PAGE_SIZE), lambda d, i: (d, 0, i, 0)),
                  pl.BlockSpec((PAGE_SIZE, PAGE_SIZE), lambda d, i: (0, 0))],
        out_specs=pl.BlockSpec((None, pb, SUBLANES, 2 * LANES), lambda d, i: (d, i, 0, 0)),
        compiler_params=pltpu.CompilerParams(dimension_semantics=("arbitrary", "arbitrary")),
        name="page_bias",
    )(lft, upper)


def _head_rows(hh):
    return slice(HEAD_DIM * hh, HEAD_DIM * (hh + 1))


def _paged_scores(qc_ref, k_of_head):
    hrow = lax.broadcasted_iota(jnp.int32, (SUBLANES, LANES), 0)
    s = jnp.zeros((SUBLANES, LANES), f32)
    for hh in range(ATT_HEADS):
        prod = qc_ref[_head_rows(hh), :] * k_of_head(hh)
        s = jnp.where(hrow == hh, jnp.sum(prod, axis=0, keepdims=True), s)
    return s


def _paged_attn_init(lfn_ref, m_ref, l_ref, acc_ref, r_ref):
    m_ref[...] = jnp.full(m_ref.shape, NEG_BIG, f32)
    l_ref[...] = jnp.zeros_like(l_ref)
    acc_ref[...] = jnp.zeros_like(acc_ref)
    r_ref[...] = lfn_ref[0]


def _paged_attn_pages(qc_ref, k_refs, v_refs, sb_refs, m_ref, l_ref, acc_ref, r_ref):
    for k_ref, v_ref, sb_ref in zip(k_refs, v_refs, sb_refs):
        sb = sb_ref[...]
        r = r_ref[...]
        s = _paged_scores(qc_ref, lambda hh: k_ref[hh]) + (r + sb[:, 0:LANES])
        r_ref[...] = r + sb[:, LANES:2 * LANES]
        m_old = m_ref[...]
        m_new = jnp.maximum(m_old, s)
        alpha = jnp.exp(m_old - m_new)
        p = jnp.exp(s - m_new)
        l_ref[...] = alpha * l_ref[...] + p
        m_ref[...] = m_new
        for hh in range(ATT_HEADS):
            acc_ref[hh] = alpha[hh:hh + 1, :] * acc_ref[hh] + p[hh:hh + 1, :] * v_ref[hh]


def _paged_attn_finish(qc_ref, knc_ref, vnc_ref, o_ref, m_ref, l_ref, acc_ref):
    m = m_ref[...]
    s_new = _paged_scores(qc_ref, lambda hh: knc_ref[_head_rows(hh), :])
    mx = jnp.maximum(jnp.max(m, axis=1, keepdims=True), s_new)
    w = jnp.exp(m - mx)
    w_new = jnp.exp(s_new - mx)
    denom = jnp.sum(l_ref[...] * w, axis=1, keepdims=True) + w_new
    for hh in range(ATT_HEADS):
        num = (jnp.sum(acc_ref[hh] * w[hh:hh + 1, :], axis=1, keepdims=True)
               + w_new[hh:hh + 1, :] * vnc_ref[_head_rows(hh), :])
        o_ref[0, _head_rows(hh), :] = num / denom[hh:hh + 1, :]


def _sample_out_body(x_ref, att_ref, yt_ref, xs_ref, z_ref, gates_ref, sst_ref, scw_ref, dsk_ref, ng_ref,
                     wo_ref, gpost_ref, gpre_ref, gmpost_ref, wup_ref, wdn_ref, o_ref, snew_ref, *, ff_chunk):
    n = x_ref.shape[0]
    y = yt_ref[...].T[0:n, :] + dsk_ref[...] * xs_ref[...]
    yssd = _rms(y * _silu(z_ref[...]), ng_ref[...])
    gates = gates_ref[...]
    u = gates[:, CONV_WIDTH:2 * CONV_WIDTH] * gates[:, 2 * CONV_WIDTH:3 * CONV_WIDTH]
    uc = scw_ref[0:1, :] * sst_ref[0] + scw_ref[1:2, :] * sst_ref[1] + scw_ref[2:3, :] * u
    yconv = gates[:, 0:CONV_WIDTH] * uc
    snew_ref[0] = sst_ref[1]
    snew_ref[1] = u
    dot = functools.partial(jnp.dot, preferred_element_type=f32)
    mix = (dot(att_ref[...].astype(bf16), wo_ref[0:ATT_WIDTH, :])
           + dot(yssd.astype(bf16), wo_ref[ATT_WIDTH:ATT_WIDTH + SSD_WIDTH, :])
           + dot(yconv.astype(bf16), wo_ref[ATT_WIDTH + SSD_WIDTH:D_MODEL, :]))
    o_ref[...] = _mlp_tail(x_ref[...], mix, gpost_ref[...], gpre_ref[...], gmpost_ref[...],
                           wup_ref, wdn_ref, ff_chunk)


def _sample_out(x, att, yt, xs, z, gates, sst, scw, dsk, ng, wo, gpost, gpre, gmpost, wup, wdn):
    n = x.shape[0]
    return pl.pallas_call(
        functools.partial(_sample_out_body, ff_chunk=min(512, wup.shape[2])),
        out_shape=(jax.ShapeDtypeStruct((n, D_MODEL), f32),
                   jax.ShapeDtypeStruct((SCONV - 1, n, CONV_WIDTH), f32)),
        compiler_params=pltpu.CompilerParams(vmem_limit_bytes=VMEM_LIMIT_BYTES),
        name="sample_out_mlp",
    )(x, att, yt, xs, z, gates, sst, scw, dsk, ng, wo, gpost, gpre, gmpost, wup, wdn)


def _prep_w_in(wt):
    offs = [0]
    for s in IN_SIZES:
        offs.append(offs[-1] + s)
    q, k, v, f, z, xbc, dt, gb, gc, hc = [wt[offs[i]:offs[i + 1]] for i in range(len(IN_SIZES))]
    small = jnp.concatenate([f, dt, jnp.zeros((LANES - ATT_HEADS - SSD_HEADS, wt.shape[1]), wt.dtype)], axis=0)
    return jnp.concatenate([q, k, v, z, xbc, gb, gc, hc, small], axis=0).astype(bf16)


def kernel(x_prompt, x_sample, cache_k, cache_v, cache_logf, state_ssm, state_ssd_conv, state_sconv, page_table,
           w_in, b_f, ssd_conv_w, ssd_conv_b, dt_bias, a_log, d_skip, ssd_norm_g, sconv_w, w_out,
           g_mix_pre, g_mix_post, g_mlp_pre, g_mlp_post, w_mlp_up, w_mlp_down):
    depth = w_in.shape[0]
    B, L, _ = x_prompt.shape
    n = x_sample.shape[0]
    assert x_sample.shape[1] == 1 and n <= LANES
    tm = _pick_tile(L, 512)
    tq = _pick_tile(L, 512)
    tk = _pick_tile(L, 512)

    kt_cache = jnp.transpose(cache_k, (0, 1, 3, 4, 2))
    vt_cache = jnp.transpose(cache_v, (0, 1, 3, 4, 2))
    page_terms = _page_bias(jnp.transpose(cache_logf, (0, 3, 1, 2)))

    w_in_t = jnp.transpose(w_in, (0, 2, 1))
    nsub = page_table.shape[1] // min(PAGES_PER_STEP, page_table.shape[1])
    fs = D_FF // nsub
    wup_all = jnp.transpose(w_mlp_up.reshape(depth, D_MODEL, nsub, fs), (0, 2, 1, 3)).astype(bf16)
    wdn_all = w_mlp_down.reshape(depth, nsub, fs, D_MODEL).astype(bf16)

    xp = x_prompt
    xs = x_sample.reshape(n, D_MODEL)
    kv_all = None
    outs_p = [[] for _ in range(6)]
    outs_s = [[] for _ in range(6)]
    row = lambda a: a.reshape(1, -1)
    lanes_bcast = lambda a: jnp.broadcast_to(a[:, :, None], a.shape + (LANES,))

    for l in range(depth):
        w = _prep_w_in(w_in_t[l])
        wo = w_out[l].astype(bf16)
        wup = wup_all[l]
        wdn = wdn_all[l]
        bias = jnp.concatenate([b_f[l], dt_bias[l], jnp.zeros((LANES - ATT_HEADS - SSD_HEADS,), f32)]).reshape(1, LANES)
        a_neg = -jnp.exp(a_log[l])
        a_row = jnp.concatenate([jnp.zeros((COL_DT,), f32), a_neg,
                                 jnp.zeros((LANES - COL_DT - SSD_HEADS,), f32)]).reshape(1, LANES)
        a_exp = jnp.repeat(a_neg, SSD_HEAD_DIM).reshape(1, SSD_WIDTH)
        dsk = jnp.repeat(d_skip[l], SSD_HEAD_DIM).reshape(1, SSD_WIDTH)
        ng = row(ssd_norm_g[l])
        cw, cb, scw = ssd_conv_w[l], row(ssd_conv_b[l]), sconv_w[l]
        gpre, gpost, gmpre, gmpost = row(g_mix_pre[l]), row(g_mix_post[l]), row(g_mlp_pre[l]), row(g_mlp_post[l])

        (qtp, kp, vtp, kt_all, vt_all, z, xbc, yconv, lfdt, lft, utail) = _prompt_in(
            xp, gpre, w, bias, scw, kv_all, l, depth, tm=tm, tq=tq, tk=tk)
        kv_all = (kt_all, vt_all)
        att = _prompt_attn(qtp, kp, vtp, tq=tq, tk=tk)
        yssd, s_fin = _prompt_ssd(xbc, lfdt, z, cw, cb, a_row, dsk, ng, tm=tm)

        cst = jnp.transpose(state_ssd_conv[l], (1, 0, 2))
        sst = jnp.transpose(state_sconv[l], (1, 0, 2))
        (q, k, v, lf, z_s, gates, xs_s, bm, cm, dtxt, dect, cnew, qt, knt, vnt) = _sample_in(
            xs, gpre, w, bias, cst, cw, cb, a_exp)
        s_new, yt = _sample_ssd(state_ssm[l], dtxt, dect, bm, cm)
        lf6 = lf[:, COL_LOGF:COL_LOGF + ATT_HEADS]
        lfn = lanes_bcast(jnp.pad(lf6, ((0, 0), (0, SUBLANES - ATT_HEADS))))

        xp, att_s = _out_mlp_attn(
            xp.reshape(B * L, D_MODEL), att, yssd.reshape(B * L, SSD_WIDTH),
            yconv.reshape(B * L, CONV_WIDTH), wo, gpost, gmpre, gmpost, wup, wdn,
            page_table, qt, knt, vnt, lfn, kt_cache, vt_cache, page_terms, l,
            tm=tm)
        xp = xp.reshape(B, L, D_MODEL)
        xs, snew = _sample_out(xs, att_s[:, :, 0], yt, xs_s, z_s, gates, sst, scw, dsk, ng,
                               wo, gpost, gmpre, gmpost, wup, wdn)

        outs_p[2].append(lft[:, 0:ATT_HEADS, :])
        outs_p[3].append(s_fin)
        outs_p[4].append(xbc[:, L - (SSD_CONV - 1):, :])
        outs_p[5].append(utail[:, SUBLANES - (SCONV - 1):, :])
        outs_s[0].append(k.reshape(n, 1, ATT_HEADS, HEAD_DIM))
        outs_s[1].append(v.reshape(n, 1, ATT_HEADS, HEAD_DIM))
        outs_s[2].append(lf6.reshape(n, 1, ATT_HEADS))
        outs_s[3].append(s_new)
        outs_s[4].append(jnp.transpose(cnew, (1, 0, 2)))
        outs_s[5].append(jnp.transpose(snew, (1, 0, 2)))

    heads_last = lambda a: jnp.transpose(a.reshape(depth, B, ATT_HEADS, HEAD_DIM, L), (0, 1, 4, 2, 3))
    k_p = heads_last(kv_all[0])
    v_p = heads_last(kv_all[1])
    lf_p = jnp.transpose(jnp.stack(outs_p[2]), (0, 1, 3, 2))
    rest_p = [jnp.stack(a) for a in outs_p[3:]]
    stacked_s = [jnp.stack(a) for a in outs_s]
    return (xp, xs.reshape(n, 1, D_MODEL), k_p, v_p, lf_p, *rest_p, *stacked_s)
```

```python
import functools

import jax
import jax.numpy as jnp
from jax import lax
from jax.experimental import pallas as pl
from jax.experimental.pallas import tpu as pltpu

f32 = jnp.float32
bf16 = jnp.bfloat16

D_MODEL = 1024
HEAD_DIM = 64
ATT_HEADS = 6
ATT_WIDTH = ATT_HEADS * HEAD_DIM
SSD_HEADS = 6
SSD_HEAD_DIM = 64
SSD_WIDTH = SSD_HEADS * SSD_HEAD_DIM
SSD_GROUPS = 2
D_STATE = 64
SSD_CONV = 4
XBC_WIDTH = SSD_WIDTH + 2 * SSD_GROUPS * D_STATE
CONV_WIDTH = 256
SCONV = 3
D_FF = 4 * D_MODEL
SSD_CHUNK = 128
PAGE_SIZE = 128
PAGES_PER_STEP = 16
RMS_EPS = 1e-6
IN_SIZES = (ATT_WIDTH, ATT_WIDTH, ATT_WIDTH, ATT_HEADS, SSD_WIDTH, XBC_WIDTH, SSD_HEADS,
            CONV_WIDTH, CONV_WIDTH, CONV_WIDTH)

LANES = 128
SUBLANES = 8
VMEM_LIMIT_BYTES = 56 * 1024 * 1024

OFF_Q = 0
OFF_K = OFF_Q + ATT_WIDTH
OFF_V = OFF_K + ATT_WIDTH
OFF_Z = OFF_V + ATT_WIDTH
OFF_XBC = OFF_Z + SSD_WIDTH
OFF_GB = OFF_XBC + XBC_WIDTH
OFF_GC = OFF_GB + CONV_WIDTH
OFF_HC = OFF_GC + CONV_WIDTH
OFF_SMALL = OFF_HC + CONV_WIDTH
PROJ_WIDTH = OFF_SMALL + LANES
COL_LOGF = 0
COL_DT = ATT_HEADS
PADDED_HEAD = LANES
NEG_BIG = -1e30
LOG2E = 1.4426950408889634
VALUE_ROWS = 80


def _rms(x, g):
    var = jnp.mean(x * x, axis=-1, keepdims=True)
    return (x * lax.rsqrt(var + RMS_EPS)) * g


def _silu(x):
    return x * (1.0 / (1.0 + jnp.exp(-x)))


def _split3(a):
    a1 = a.astype(bf16)
    r1 = a - a1.astype(f32)
    a2 = r1.astype(bf16)
    a3 = (r1 - a2.astype(f32)).astype(bf16)
    return a1, a2, a3


def _exact_dot(m01, a):
    a1, a2, a3 = _split3(a)
    d = functools.partial(jnp.dot, preferred_element_type=f32)
    return d(m01, a1) + d(m01, a2) + d(m01, a3)


def _exact_dot_rhs(a, m01):
    a1, a2, a3 = _split3(a)
    d = functools.partial(jnp.dot, preferred_element_type=f32)
    return d(a1, m01) + d(a2, m01) + d(a3, m01)


def _dot_nt(a, b):
    return lax.dot_general(a, b, (((1,), (1,)), ((), ())), preferred_element_type=f32)


def _softplus_parts(t):
    sp = jnp.log1p(jnp.exp(-jnp.abs(t)))
    return jnp.minimum(t, 0.0) - sp, jnp.maximum(t, 0.0) + sp


def _shift_rows(x, k, tail):
    row = lax.broadcasted_iota(jnp.int32, x.shape, 0)
    y = pltpu.roll(x, k, 0)
    for r in range(k):
        y = jnp.where(row == r, tail[SUBLANES - k + r:SUBLANES - k + r + 1, :], y)
    return y


def _pick_tile(n, pref):
    t = min(n, pref)
    while n % t:
        t -= SUBLANES
    assert t > 0
    return t


def _prompt_in_body(x_ref, g_ref, w_ref, bias_ref, tri_ref, scw_ref, *refs, tq, tk, carried):
    _prompt_in_compute(x_ref, g_ref, w_ref, bias_ref, tri_ref, scw_ref, *refs[carried:], tq=tq, tk=tk)


def _prompt_in_compute(x_ref, g_ref, w_ref, bias_ref, tri_ref, scw_ref,
                    qtp_ref, kp_ref, vtp_ref, kt32_ref, vt32_ref, z_ref, xbc_ref, yconv_ref,
                    lfdt_ref, lft_ref, utail_ref,
                    carry_ref, tail_ref, *, tq, tk):
    tm = x_ref.shape[1]

    @pl.when(pl.program_id(1) == 0)
    def _():
        carry_ref[...] = jnp.zeros_like(carry_ref)
        tail_ref[...] = jnp.zeros_like(tail_ref)

    h = _rms(x_ref[0], g_ref[...]).astype(bf16)
    proj = lambda lo, hi: _dot_nt(h, w_ref[lo:hi, :])

    t = proj(OFF_SMALL, PROJ_WIDTH) + bias_ref[...]
    logf, dt = _softplus_parts(t)
    col = lax.broadcasted_iota(jnp.int32, t.shape, 1)
    lfdt_ref[0] = jnp.where(col < COL_DT, logf, dt)
    lft_ref[0] = logf.T[0:SUBLANES, :]
    carry = carry_ref[...]
    blocks = []
    for r0 in range(0, tm, SSD_CHUNK):
        cb = _exact_dot(tri_ref[...], logf[r0:r0 + SSD_CHUNK]) + carry
        carry = cb[SSD_CHUNK - 1:SSD_CHUNK, :]
        blocks.append(cb)
    carry_ref[...] = carry
    c = jnp.concatenate(blocks, axis=0) * LOG2E
    c1, c2, c3 = [p.astype(f32) for p in _split3(c)]
    ct1, ct2, ct3 = [p.astype(f32) for p in _split3(c.T[0:SUBLANES, :])]

    qkv_t = _dot_nt(w_ref[OFF_Q:OFF_Z, :], h)
    kt = qkv_t[ATT_WIDTH:2 * ATT_WIDTH]
    vt = qkv_t[2 * ATT_WIDTH:3 * ATT_WIDTH]
    kt32_ref[0] = kt
    vt32_ref[0] = vt
    k = kt.T
    lane = lax.broadcasted_iota(jnp.int32, (tm, PADDED_HEAD - HEAD_DIM), 1)
    row = lax.broadcasted_iota(jnp.int32, (PADDED_HEAD - HEAD_DIM, tm), 0)
    vrow = lax.broadcasted_iota(jnp.int32, (VALUE_ROWS - HEAD_DIM, tm), 0)
    ones_row = jnp.where(vrow == 0, 1.0, 0.0).astype(bf16)
    for hh in range(ATT_HEADS):
        src = slice(HEAD_DIM * hh, HEAD_DIM * (hh + 1))
        lo = slice(PADDED_HEAD * hh, PADDED_HEAD * hh + HEAD_DIM)
        hi = slice(PADDED_HEAD * hh + HEAD_DIM, PADDED_HEAD * (hh + 1))
        hc = slice(hh, hh + 1)
        kp_ref[0, :, lo] = k[:, src].astype(bf16)
        ek = jnp.where(lane < 3, 1.0, jnp.where(lane == 3, -c1[:, hc], jnp.where(
            lane == 4, -c2[:, hc], jnp.where(lane == 5, -c3[:, hc], 0.0))))
        kp_ref[0, :, hi] = ek.astype(bf16)
        eq = jnp.where(row == 0, ct1[hc, :], jnp.where(row == 1, ct2[hc, :], jnp.where(
            row == 2, ct3[hc, :], jnp.where(row < 6, 1.0, 0.0)))).astype(bf16)
        qh = (qkv_t[src] * (HEAD_DIM ** -0.5 * LOG2E)).astype(bf16)
        for i in range(tm // tq):
            cs = slice(i * tq, (i + 1) * tq)
            qtp_ref[0, i, lo, :] = qh[:, cs]
            qtp_ref[0, i, hi, :] = eq[:, cs]
        vh = vt[src].astype(bf16)
        for i in range(tm // tk):
            cs = slice(i * tk, (i + 1) * tk)
            vtp_ref[0, i, VALUE_ROWS * hh:VALUE_ROWS * hh + HEAD_DIM, :] = vh[:, cs]
            vtp_ref[0, i, VALUE_ROWS * hh + HEAD_DIM:VALUE_ROWS * (hh + 1), :] = ones_row[:, cs]

    z_ref[0] = proj(OFF_Z, OFF_XBC)
    xbc_ref[0] = proj(OFF_XBC, OFF_GB)

    gates = proj(OFF_GB, OFF_SMALL)
    gb = gates[:, 0:CONV_WIDTH]
    u = gates[:, CONV_WIDTH:2 * CONV_WIDTH] * gates[:, 2 * CONV_WIDTH:3 * CONV_WIDTH]
    tail = tail_ref[...]
    uc = (scw_ref[0:1, :] * _shift_rows(u, 2, tail) + scw_ref[1:2, :] * _shift_rows(u, 1, tail)
          + scw_ref[2:3, :] * u)
    yconv_ref[0] = (gb * uc).astype(bf16)
    tail_ref[...] = u[tm - SUBLANES:tm, :]
    utail_ref[0] = u[tm - SUBLANES:tm, :]


def _prompt_in(x, g, w, bias, scw, kv_all, layer, depth, *, tm, tq, tk):
    B, L, _ = x.shape
    nj = L // tm
    pw = ATT_HEADS * PADDED_HEAD
    tri = jnp.tril(jnp.ones((SSD_CHUNK, SSD_CHUNK), f32)).astype(bf16)
    row = lambda width: pl.BlockSpec((1, tm, width), lambda b, j: (b, j, 0))
    colb = lambda height: pl.BlockSpec((1, height, tm), lambda b, j: (b, 0, j))
    const = lambda shape: pl.BlockSpec(shape, lambda b, j: (0,) * len(shape))
    layer_colb = pl.BlockSpec((None, 1, ATT_WIDTH, tm), lambda b, j: (layer, b, 0, j))
    carried = () if kv_all is None else tuple(kv_all)
    out_shapes = (
        jax.ShapeDtypeStruct((B, L // tq, pw, tq), bf16),
        jax.ShapeDtypeStruct((B, L, pw), bf16),
        jax.ShapeDtypeStruct((B, L // tk, ATT_HEADS * VALUE_ROWS, tk), bf16),
        jax.ShapeDtypeStruct((depth, B, ATT_WIDTH, L), f32),
        jax.ShapeDtypeStruct((depth, B, ATT_WIDTH, L), f32),
        jax.ShapeDtypeStruct((B, L, SSD_WIDTH), f32),
        jax.ShapeDtypeStruct((B, L, XBC_WIDTH), f32),
        jax.ShapeDtypeStruct((B, L, CONV_WIDTH), bf16),
        jax.ShapeDtypeStruct((B, L, LANES), f32),
        jax.ShapeDtypeStruct((B, SUBLANES, L), f32),
        jax.ShapeDtypeStruct((B, SUBLANES, CONV_WIDTH), f32),
    )
    out_specs = (
        pl.BlockSpec((1, tm // tq, pw, tq), lambda b, j: (b, j, 0, 0)),
        row(pw),
        pl.BlockSpec((1, tm // tk, ATT_HEADS * VALUE_ROWS, tk), lambda b, j: (b, j, 0, 0)),
        layer_colb, layer_colb, row(SSD_WIDTH), row(XBC_WIDTH), row(CONV_WIDTH),
        row(LANES), colb(SUBLANES),
        pl.BlockSpec((1, SUBLANES, CONV_WIDTH), lambda b, j: (b, 0, 0)),
    )
    return pl.pallas_call(
        functools.partial(_prompt_in_body, tq=tq, tk=tk, carried=len(carried)),
        out_shape=out_shapes,
        grid=(B, nj),
        in_specs=[row(D_MODEL), const((1, D_MODEL)), const((PROJ_WIDTH, D_MODEL)), const((1, LANES)),
                  const((SSD_CHUNK, SSD_CHUNK)), const((SCONV, CONV_WIDTH))]
        + [pl.BlockSpec(memory_space=pl.ANY)] * len(carried),
        input_output_aliases={6 + i: 3 + i for i in range(len(carried))},
        out_specs=out_specs,
        scratch_shapes=[pltpu.VMEM((1, LANES), f32), pltpu.VMEM((SUBLANES, CONV_WIDTH), f32)],
        compiler_params=pltpu.CompilerParams(dimension_semantics=("arbitrary", "arbitrary"),
                                             vmem_limit_bytes=VMEM_LIMIT_BYTES),
        name="prompt_in",
    )(x, g, w, bias, tri, scw, *carried)


def _prompt_attn_body(qt_ref, kp_ref, vt_ref, o_ref, m_ref, acc_ref, st_ref, *, tq, tk):
    qi = pl.program_id(1)
    m_ref[...] = jnp.full(m_ref.shape, NEG_BIG, f32)
    acc_ref[...] = jnp.zeros_like(acc_ref)
    nfull = lax.div(qi * tq, tk)
    heads = [slice(PADDED_HEAD * hh, PADDED_HEAD * (hh + 1)) for hh in range(ATT_HEADS)]
    vrows = [slice(VALUE_ROWS * hh, VALUE_ROWS * (hh + 1)) for hh in range(ATT_HEADS)]

    def score(kj, slot, masked):
        ks = pl.multiple_of(kj * tk, tk)
        if masked:
            kpos = ks + lax.broadcasted_iota(jnp.int32, (tk, tq), 0)
            qpos = qi * tq + lax.broadcasted_iota(jnp.int32, (tk, tq), 1)
            valid = kpos <= qpos
        for hh, hs in enumerate(heads):
            st = jnp.dot(kp_ref[0, pl.ds(ks, tk), hs], qt_ref[0, 0, hs, :], preferred_element_type=f32)
            st_ref[slot, hh] = jnp.where(valid, st, NEG_BIG) if masked else st

    def absorb(kj, slot):
        probs = []
        for hh in range(ATT_HEADS):
            st = st_ref[slot, hh]
            m_old = m_ref[hh]
            m_new = jnp.maximum(m_old, jnp.max(st, axis=0, keepdims=True))
            m_ref[hh] = m_new
            probs.append((jnp.exp2(m_old[0:1] - m_new[0:1]), jnp.exp2(st - m_new[0:1]).astype(bf16)))
        for hh, (alpha, pt) in enumerate(probs):
            acc_ref[hh] = alpha * acc_ref[hh] + jnp.dot(vt_ref[0, kj, vrows[hh], :], pt,
                                                        preferred_element_type=f32)

    @pl.when(nfull > 0)
    def _():
        score(0, 0, False)

    def pair(pi, carry):
        k0 = 2 * pi
        score(k0 + 1, 1, False)
        absorb(k0, 0)
        score(k0 + 2, 0, False)
        absorb(k0 + 1, 1)
        return carry

    ntrip = lax.div(jnp.maximum(nfull - 1, 0), 2)
    lax.fori_loop(0, ntrip, pair, 0)
    done = 2 * ntrip
    rem = nfull - done

    @pl.when(rem == 2)
    def _():
        score(done + 1, 1, False)
        absorb(done, 0)
        score(done + 2, 0, True)
        absorb(done + 1, 1)
        absorb(done + 2, 0)

    @pl.when(rem == 1)
    def _():
        score(done + 1, 1, True)
        absorb(done, 0)
        absorb(done + 1, 1)

    @pl.when(rem == 0)
    def _():
        score(0, 0, True)
        absorb(0, 0)

    for hh in range(ATT_HEADS):
        a = acc_ref[hh]
        o_ref[0, HEAD_DIM * hh:HEAD_DIM * (hh + 1), :] = (a[0:HEAD_DIM] / a[HEAD_DIM:HEAD_DIM + 1]).astype(bf16)


def _prompt_attn(qtp, kp, vtp, *, tq, tk):
    B, nq, W, _ = qtp.shape
    L = kp.shape[1]
    vw = ATT_HEADS * VALUE_ROWS
    return pl.pallas_call(
        functools.partial(_prompt_attn_body, tq=tq, tk=tk),
        out_shape=jax.ShapeDtypeStruct((B, ATT_WIDTH, L), bf16),
        grid=(B, nq),
        in_specs=[pl.BlockSpec((1, 1, W, tq), lambda b, i: (b, i, 0, 0)),
                  pl.BlockSpec((1, L, W), lambda b, i: (b, 0, 0)),
                  pl.BlockSpec((1, L // tk, vw, tk), lambda b, i: (b, 0, 0, 0))],
        out_specs=pl.BlockSpec((1, ATT_WIDTH, tq), lambda b, i: (b, 0, i)),
        scratch_shapes=[pltpu.VMEM((ATT_HEADS, SUBLANES, tq), f32),
                        pltpu.VMEM((ATT_HEADS, VALUE_ROWS, tq), f32),
                        pltpu.VMEM((2, ATT_HEADS, tk, tq), f32)],
        compiler_params=pltpu.CompilerParams(dimension_semantics=("arbitrary", "arbitrary"),
                                             vmem_limit_bytes=VMEM_LIMIT_BYTES),
        name="prompt_attn",
    )(qtp, kp, vtp)


def _prompt_ssd_body(xbc_ref, lfdt_ref, z_ref, cw_ref, cb_ref, a_ref, dsk_ref, ng_ref, tri_ref,
                     y_ref, sout_ref, s_ref, tail_ref):
    tm = xbc_ref.shape[1]
    Q = SSD_CHUNK

    @pl.when(pl.program_id(1) == 0)
    def _():
        s_ref[...] = jnp.zeros_like(s_ref)
        tail_ref[...] = jnp.zeros_like(tail_ref)

    x = xbc_ref[0]
    tail = tail_ref[...]
    xc = (cw_ref[0:1, :] * _shift_rows(x, 3, tail) + cw_ref[1:2, :] * _shift_rows(x, 2, tail)
          + cw_ref[2:3, :] * _shift_rows(x, 1, tail) + cw_ref[3:4, :] * x + cb_ref[...])
    xc = _silu(xc)
    tail_ref[...] = x[tm - SUBLANES:tm, :]

    dt_all = lfdt_ref[0]
    dta_all = dt_all * a_ref[...]
    row = lax.broadcasted_iota(jnp.int32, (Q, Q), 0)
    colm = lax.broadcasted_iota(jnp.int32, (Q, Q), 1)
    causal = row >= colm
    tri = tri_ref[...]
    group = lambda hh: slice((hh // (SSD_HEADS // SSD_GROUPS)) * D_STATE,
                             (hh // (SSD_HEADS // SSD_GROUPS) + 1) * D_STATE)
    head = lambda hh: slice(hh * SSD_HEAD_DIM, (hh + 1) * SSD_HEAD_DIM)

    states = [s_ref[hh] for hh in range(SSD_HEADS)]
    for c in range(tm // Q):
        rs = slice(c * Q, (c + 1) * Q)
        dt = dt_all[rs]
        acum = _exact_dot(tri, dta_all[rs])
        acum_t = acum.T
        dt_t = dt.T
        xs = xc[rs, 0:SSD_WIDTH]
        xs_t = xs.T.astype(bf16)
        xs_b = xs.astype(bf16)
        bm = xc[rs, SSD_WIDTH:SSD_WIDTH + SSD_GROUPS * D_STATE]
        cm = xc[rs, SSD_WIDTH + SSD_GROUPS * D_STATE:XBC_WIDTH].astype(bf16)
        cb = []
        for g in range(SSD_GROUPS):
            gs = slice(g * D_STATE, (g + 1) * D_STATE)
            cb.append(_dot_nt(cm[:, gs], bm[:, gs].astype(bf16)))
        terms = []
        for hh in range(SSD_HEADS):
            cc = COL_DT + hh
            a_col = acum[:, cc:cc + 1]
            a_row = acum_t[cc:cc + 1, :]
            a_last = acum[Q - 1:Q, cc:cc + 1]
            decay = jnp.exp(jnp.where(causal, a_col - a_row, -jnp.inf))
            mat = (cb[hh // (SSD_HEADS // SSD_GROUPS)] * decay * dt_t[cc:cc + 1, :]).astype(bf16)
            wcol = jnp.exp(a_last - a_col) * dt[:, cc:cc + 1]
            bw = (bm[:, group(hh)] * wcol).astype(bf16)
            terms.append((mat, bw, jnp.exp(a_col), jnp.exp(a_last)))
        ys = []
        for hh, (mat, bw, e_col, e_last) in enumerate(terms):
            s_in = states[hh]
            y_intra = jnp.dot(mat, xs_b[:, head(hh)], preferred_element_type=f32)
            y_inter = _dot_nt(cm[:, group(hh)], s_in.astype(bf16)) * e_col
            ys.append(y_intra + y_inter)
            states[hh] = e_last * s_in + jnp.dot(xs_t[head(hh), :], bw, preferred_element_type=f32)
        y = jnp.concatenate(ys, axis=1) + dsk_ref[...] * xs
        gated = y * _silu(z_ref[0, rs, :])
        y_ref[0, rs, :] = _rms(gated, ng_ref[...]).astype(bf16)
    for hh in range(SSD_HEADS):
        s_ref[hh] = states[hh]
    sout_ref[0] = jnp.stack(states)


def _prompt_ssd(xbc, lfdt, z, cw, cb, a_row, dsk, ng, *, tm):
    B, L, _ = xbc.shape
    tri = jnp.tril(jnp.ones((SSD_CHUNK, SSD_CHUNK), f32)).astype(bf16)
    row = lambda width: pl.BlockSpec((1, tm, width), lambda b, j: (b, j, 0))
    const = lambda shape: pl.BlockSpec(shape, lambda b, j: (0,) * len(shape))
    return pl.pallas_call(
        _prompt_ssd_body,
        out_shape=(jax.ShapeDtypeStruct((B, L, SSD_WIDTH), bf16),
                   jax.ShapeDtypeStruct((B, SSD_HEADS, SSD_HEAD_DIM, D_STATE), f32)),
        grid=(B, L // tm),
        in_specs=[row(XBC_WIDTH), row(LANES), row(SSD_WIDTH), const((SSD_CONV, XBC_WIDTH)),
                  const((1, XBC_WIDTH)), const((1, LANES)), const((1, SSD_WIDTH)), const((1, SSD_WIDTH)),
                  const((SSD_CHUNK, SSD_CHUNK))],
        out_specs=(row(SSD_WIDTH),
                   pl.BlockSpec((1, SSD_HEADS, SSD_HEAD_DIM, D_STATE), lambda b, j: (b, 0, 0, 0))),
        scratch_shapes=[pltpu.VMEM((SSD_HEADS, SSD_HEAD_DIM, D_STATE), f32),
                        pltpu.VMEM((SUBLANES, XBC_WIDTH), f32)],
        compiler_params=pltpu.CompilerParams(dimension_semantics=("arbitrary", "arbitrary"),
                                             vmem_limit_bytes=VMEM_LIMIT_BYTES),
        name="prompt_ssd",
    )(xbc, lfdt, z, cw, cb, a_row, dsk, ng, tri)


def _mlp_tail(x, mix, gpost, gpre, gmpost, wup_ref, wdn_ref, ff_chunk):
    x1 = x + _rms(mix, gpost)
    hmid = _rms(x1, gpre).astype(bf16)
    acc = jnp.zeros_like(x1)
    nsub, _, fs = wup_ref.shape
    for j in range(nsub):
        for c in range(fs // ff_chunk):
            cs = slice(c * ff_chunk, (c + 1) * ff_chunk)
            up = jnp.dot(hmid, wup_ref[j, :, cs], preferred_element_type=f32)
            act = jnp.square(jnp.maximum(up, 0.0)).astype(bf16)
            acc = acc + jnp.dot(act, wdn_ref[j, cs, :], preferred_element_type=f32)
    return x1 + _rms(acc, gmpost)


def _out_mlp_attn_body(pt_ref, x_ref, att_ref, yssd_ref, yconv_ref, wo_ref, gpost_ref, gpre_ref, gmpost_ref,
                       wup_ref, wdn_ref, qt_ref, knt_ref, vnt_ref, lfn_ref, *rest, pps, ff_chunk):
    k_refs = rest[0:pps]
    v_refs = rest[pps:2 * pps]
    sb_refs = rest[2 * pps:3 * pps]
    o_ref, oa_ref, x1_ref, h_ref, acc_ref, m_ref, l_ref, acca_ref, r_ref, qc_ref, knc_ref, vnc_ref = rest[3 * pps:]
    j = pl.program_id(1)
    last = pl.num_programs(1) - 1
    dot = functools.partial(jnp.dot, preferred_element_type=f32)

    @pl.when(j == 0)
    def _():
        att_mix = lax.dot_general(att_ref[0], wo_ref[0:ATT_WIDTH, :], (((0,), (0,)), ((), ())),
                                  preferred_element_type=f32)
        mix = (att_mix
               + dot(yssd_ref[...], wo_ref[ATT_WIDTH:ATT_WIDTH + SSD_WIDTH, :])
               + dot(yconv_ref[...], wo_ref[ATT_WIDTH + SSD_WIDTH:D_MODEL, :]))
        x1 = x_ref[...] + _rms(mix, gpost_ref[...])
        x1_ref[...] = x1
        h_ref[...] = _rms(x1, gpre_ref[...]).astype(bf16)
        acc_ref[...] = jnp.zeros_like(acc_ref)
        _paged_attn_init(lfn_ref, m_ref, l_ref, acca_ref, r_ref)
        mine = lax.broadcasted_iota(jnp.int32, qt_ref.shape, 1) == pl.program_id(0)
        for src, dst in ((qt_ref, qc_ref), (knt_ref, knc_ref), (vnt_ref, vnc_ref)):
            col = jnp.sum(jnp.where(mine, src[...], 0.0), axis=1, keepdims=True)
            dst[...] = jnp.broadcast_to(col, dst.shape)

    nchunk = wup_ref.shape[2] // ff_chunk
    half = ff_chunk // 2
    groups = 4 * nchunk
    per = -(-pps // groups)
    pages = lambda g: _paged_attn_pages(qc_ref, k_refs[g * per:(g + 1) * per], v_refs[g * per:(g + 1) * per],
                                        sb_refs[g * per:(g + 1) * per], m_ref, l_ref, acca_ref, r_ref)
    relu2 = lambda u: jnp.square(jnp.maximum(u, 0.0)).astype(bf16)
    hmid = h_ref[...]
    acc = acc_ref[...]
    for c in range(nchunk):
        lo = slice(c * ff_chunk, c * ff_chunk + half)
        hi = slice(c * ff_chunk + half, (c + 1) * ff_chunk)
        up_lo = dot(hmid, wup_ref[j, :, lo])
        pages(4 * c)
        up_hi = dot(hmid, wup_ref[j, :, hi])
        pages(4 * c + 1)
        acc = acc + dot(relu2(up_lo), wdn_ref[j, lo, :])
        pages(4 * c + 2)
        acc = acc + dot(relu2(up_hi), wdn_ref[j, hi, :])
        pages(4 * c + 3)
    acc_ref[...] = acc

    @pl.when(j == last)
    def _():
        o_ref[...] = x1_ref[...] + _rms(acc_ref[...], gmpost_ref[...])
        _paged_attn_finish(qc_ref, knc_ref, vnc_ref, oa_ref, m_ref, l_ref, acca_ref)


def _out_mlp_attn(x, att, yssd, yconv, wo, gpost, gpre, gmpost, wup, wdn,
                  page_table, qt, knt, vnt, lfn, kt, vt, sb, layer, *, tm):
    M = x.shape[0]
    n, n_pages = page_table.shape
    pps = min(PAGES_PER_STEP, n_pages)
    nsub = n_pages // pps
    assert M // tm == n and n_pages % pps == 0 and D_FF % nsub == 0
    fs = D_FF // nsub
    assert wup.shape == (nsub, D_MODEL, fs) and wdn.shape == (nsub, fs, D_MODEL)
    pt = page_table.reshape(-1)
    row = lambda width: pl.BlockSpec((tm, width), lambda i, j, pt: (i, 0))
    nlb = att.shape[2] // tm
    att_spec = pl.BlockSpec((1, ATT_WIDTH, tm), lambda i, j, pt: (i // nlb, 0, i % nlb))
    const = lambda shape: pl.BlockSpec(shape, lambda i, j, pt: (0,) * len(shape), pipeline_mode=pl.Buffered(1))
    seq = lambda rows: pl.BlockSpec((1, rows, LANES), lambda i, j, pt: (i, 0, 0))

    def page(i, j, pt, k):
        return pt[i * n_pages + n_pages - 1 - (j * pps + k)]

    kv_specs = [pl.BlockSpec((None, None, ATT_HEADS, HEAD_DIM, PAGE_SIZE),
                             lambda i, j, pt, k=k: (layer, page(i, j, pt, k), 0, 0, 0)) for k in range(pps)]
    sb_specs = [pl.BlockSpec((None, None, SUBLANES, 2 * LANES),
                             lambda i, j, pt, k=k: (layer, page(i, j, pt, k), 0, 0)) for k in range(pps)]
    return pl.pallas_call(
        functools.partial(_out_mlp_attn_body, pps=pps, ff_chunk=min(512, fs)),
        out_shape=(jax.ShapeDtypeStruct((M, D_MODEL), f32), jax.ShapeDtypeStruct((n, ATT_WIDTH, LANES), f32)),
        grid_spec=pltpu.PrefetchScalarGridSpec(
            num_scalar_prefetch=1,
            grid=(n, nsub),
            in_specs=[row(D_MODEL), att_spec, row(SSD_WIDTH), row(CONV_WIDTH),
                      const((D_MODEL, D_MODEL)), const((1, D_MODEL)), const((1, D_MODEL)), const((1, D_MODEL)),
                      const((nsub, D_MODEL, fs)), const((nsub, fs, D_MODEL)),
                      const((ATT_WIDTH, LANES)), const((ATT_WIDTH, LANES)), const((ATT_WIDTH, LANES)), seq(SUBLANES)]
            + kv_specs + kv_specs + sb_specs,
            out_specs=(row(D_MODEL), seq(ATT_WIDTH)),
            scratch_shapes=[pltpu.VMEM((tm, D_MODEL), f32), pltpu.VMEM((tm, D_MODEL), bf16),
                            pltpu.VMEM((tm, D_MODEL), f32),
                            pltpu.VMEM((SUBLANES, LANES), f32), pltpu.VMEM((SUBLANES, LANES), f32),
                            pltpu.VMEM((ATT_HEADS, HEAD_DIM, LANES), f32), pltpu.VMEM((SUBLANES, LANES), f32),
                            pltpu.VMEM((ATT_WIDTH, LANES), f32), pltpu.VMEM((ATT_WIDTH, LANES), f32),
                            pltpu.VMEM((ATT_WIDTH, LANES), f32)],
        ),
        compiler_params=pltpu.CompilerParams(dimension_semantics=("arbitrary", "arbitrary"),
                                             vmem_limit_bytes=VMEM_LIMIT_BYTES),
        name="out_mlp_sample_attn",
    )(pt, x, att, yssd, yconv, wo, gpost, gpre, gmpost, wup, wdn, qt, knt, vnt, lfn,
      *([kt] * pps), *([vt] * pps), *([sb] * pps))


def _sample_in_body(x_ref, g_ref, w_ref, bias_ref, cst_ref, cw_ref, cb_ref, aexp_ref,
                    q_ref, k_ref, v_ref, lf_ref, z_ref, gates_ref, xs_ref, bm_ref, cm_ref,
                    dtxt_ref, dect_ref, cnew_ref, qt_ref, kt_ref, vt_ref):
    n = x_ref.shape[0]
    h = _rms(x_ref[...], g_ref[...]).astype(bf16)
    proj = _dot_nt(h, w_ref[...])
    q_ref[...] = proj[:, OFF_Q:OFF_K] * (HEAD_DIM ** -0.5)
    k_ref[...] = proj[:, OFF_K:OFF_V]
    v_ref[...] = proj[:, OFF_V:OFF_Z]
    z_ref[...] = proj[:, OFF_Z:OFF_XBC]
    gates_ref[...] = proj[:, OFF_GB:OFF_SMALL]
    logf, dt = _softplus_parts(proj[:, OFF_SMALL:PROJ_WIDTH] + bias_ref[...])
    lf_ref[...] = logf

    xbc = proj[:, OFF_XBC:OFF_GB]
    xc = (cw_ref[0:1, :] * cst_ref[0] + cw_ref[1:2, :] * cst_ref[1] + cw_ref[2:3, :] * cst_ref[2]
          + cw_ref[3:4, :] * xbc + cb_ref[...])
    xc = _silu(xc)
    cnew_ref[0] = cst_ref[1]
    cnew_ref[1] = cst_ref[2]
    cnew_ref[2] = xbc
    xs = xc[:, 0:SSD_WIDTH]
    xs_ref[...] = xs
    bm_ref[...] = xc[:, SSD_WIDTH:SSD_WIDTH + SSD_GROUPS * D_STATE]
    cm_ref[...] = xc[:, SSD_WIDTH + SSD_GROUPS * D_STATE:XBC_WIDTH]

    head = lax.broadcasted_iota(jnp.int32, (n, SSD_WIDTH), 1) // SSD_HEAD_DIM
    dt_exp = jnp.zeros((n, SSD_WIDTH), f32)
    for hh in range(SSD_HEADS):
        dt_exp = jnp.where(head == hh, dt[:, COL_DT + hh:COL_DT + hh + 1], dt_exp)
    pad = jnp.zeros((LANES - n, SSD_WIDTH), f32)
    lanes_t = lambda a: jnp.concatenate([a, pad], axis=0).T
    dtxt_ref[...] = lanes_t(dt_exp * xs)
    dect_ref[...] = lanes_t(jnp.exp(dt_exp * aexp_ref[...]))
    qt_ref[...] = lanes_t(q_ref[...])
    kt_ref[...] = lanes_t(k_ref[...])
    vt_ref[...] = lanes_t(v_ref[...])


def _sample_in(x, g, w, bias, cst, cw, cb, aexp):
    n = x.shape[0]
    s = lambda *shape: jax.ShapeDtypeStruct(shape, f32)
    return pl.pallas_call(
        _sample_in_body,
        out_shape=(s(n, ATT_WIDTH), s(n, ATT_WIDTH), s(n, ATT_WIDTH), s(n, LANES), s(n, SSD_WIDTH),
                   s(n, 3 * CONV_WIDTH), s(n, SSD_WIDTH), s(n, SSD_GROUPS * D_STATE), s(n, SSD_GROUPS * D_STATE),
                   s(SSD_WIDTH, LANES), s(SSD_WIDTH, LANES), s(SSD_CONV - 1, n, XBC_WIDTH),
                   s(ATT_WIDTH, LANES), s(ATT_WIDTH, LANES), s(ATT_WIDTH, LANES)),
        compiler_params=pltpu.CompilerParams(vmem_limit_bytes=VMEM_LIMIT_BYTES),
        name="sample_in",
    )(x, g, w, bias, cst, cw, cb, aexp)


def _sample_ssd_body(s_ref, dtxt_ref, dect_ref, bm_ref, cm_ref, snew_ref, yt_ref, *, per_step):
    rows = SSD_WIDTH
    half = rows // SSD_GROUPS
    lane = lax.broadcasted_iota(jnp.int32, (rows, LANES), 1)

    def expand(ref, b):
        r = ref[pl.ds(b, 1), :]
        return jnp.concatenate([jnp.broadcast_to(r[:, g * D_STATE:(g + 1) * D_STATE], (half, D_STATE))
                                for g in range(SSD_GROUPS)], axis=0)

    @pl.when(pl.program_id(0) == 0)
    def _():
        yt_ref[...] = jnp.zeros_like(yt_ref)

    y_add = jnp.zeros((rows, LANES), f32)
    for r in range(per_step):
        b = pl.program_id(0) * per_step + r
        sel = lane == b
        dcol = jnp.sum(jnp.where(sel, dect_ref[...], 0.0), axis=1, keepdims=True)
        xcol = jnp.sum(jnp.where(sel, dtxt_ref[...], 0.0), axis=1, keepdims=True)
        s_old = s_ref[r].reshape(rows, D_STATE)
        s_new = dcol * s_old + xcol * expand(bm_ref, b)
        snew_ref[r] = s_new.reshape(SSD_HEADS, SSD_HEAD_DIM, D_STATE)
        ycol = jnp.sum(s_new * expand(cm_ref, b), axis=1, keepdims=True)
        y_add = jnp.where(sel, ycol, y_add)
    yt_ref[...] += y_add


def _sample_ssd(state, dtxt, dect, bm, cm):
    n = state.shape[0]
    per_step = 4 if n % 4 == 0 else 1
    const = lambda shape: pl.BlockSpec(shape, lambda b: (0, 0))
    sblk = pl.BlockSpec((per_step, SSD_HEADS, SSD_HEAD_DIM, D_STATE), lambda b: (b, 0, 0, 0))
    return pl.pallas_call(
        functools.partial(_sample_ssd_body, per_step=per_step),
        out_shape=(jax.ShapeDtypeStruct(state.shape, f32), jax.ShapeDtypeStruct((SSD_WIDTH, LANES), f32)),
        grid=(n // per_step,),
        in_specs=[sblk, const((SSD_WIDTH, LANES)), const((SSD_WIDTH, LANES)),
                  const(bm.shape), const(cm.shape)],
        out_specs=(sblk, const((SSD_WIDTH, LANES))),
        compiler_params=pltpu.CompilerParams(dimension_semantics=("arbitrary",)),
        name="sample_ssd",
    )(state, dtxt, dect, bm, cm)


def _page_bias_body(lf_ref, upper_ref, o_ref):
    pb = lf_ref.shape[1]
    for hh in range(ATT_HEADS):
        x = lf_ref[hh]
        o_ref[:, hh, 0:LANES] = _exact_dot_rhs(x, upper_ref[...])
        o_ref[:, hh, LANES:2 * LANES] = jnp.broadcast_to(jnp.sum(x, axis=1, keepdims=True), x.shape)
    o_ref[:, ATT_HEADS:SUBLANES, :] = jnp.zeros((pb, SUBLANES - ATT_HEADS, 2 * LANES), f32)


def _page_bias(lft):
    depth, _, pool, _ = lft.shape
    pb = _pick_tile(pool, 256)
    pos = jnp.arange(PAGE_SIZE)
    upper = (pos[:, None] > pos[None, :]).astype(bf16)
    return pl.pallas_call(
        _page_bias_body,
        out_shape=jax.ShapeDtypeStruct((depth, pool, SUBLANES, 2 * LANES), f32),
        grid=(depth, pool // pb),
        in_specs=[pl.BlockSpec((None, ATT_HEADS, pb, PAGE_SIZE), lambda d, i: (d, 0, i, 0)),
                  pl.BlockSpec((PAGE_SIZE, PAGE_SIZE), lambda d, i: (0, 0))],
        out_specs=pl.BlockSpec((None, pb, SUBLANES, 2 * LANES), lambda d, i: (d, i, 0, 0)),
        compiler_params=pltpu.CompilerParams(dimension_semantics=("arbitrary", "arbitrary")),
        name="page_bias",
    )(lft, upper)


def _head_rows(hh):
    return slice(HEAD_DIM * hh, HEAD_DIM * (hh + 1))


def _paged_scores(qc_ref, k_of_head):
    hrow = lax.broadcasted_iota(jnp.int32, (SUBLANES, LANES), 0)
    s = jnp.zeros((SUBLANES, LANES), f32)
    for hh in range(ATT_HEADS):
        prod = qc_ref[_head_rows(hh), :] * k_of_head(hh)
        s = jnp.where(hrow == hh, jnp.sum(prod, axis=0, keepdims=True), s)
    return s


def _paged_attn_init(lfn_ref, m_ref, l_ref, acc_ref, r_ref):
    m_ref[...] = jnp.full(m_ref.shape, NEG_BIG, f32)
    l_ref[...] = jnp.zeros_like(l_ref)
    acc_ref[...] = jnp.zeros_like(acc_ref)
    r_ref[...] = lfn_ref[0]


def _paged_attn_pages(qc_ref, k_refs, v_refs, sb_refs, m_ref, l_ref, acc_ref, r_ref):
    for k_ref, v_ref, sb_ref in zip(k_refs, v_refs, sb_refs):
        sb = sb_ref[...]
        r = r_ref[...]
        s = _paged_scores(qc_ref, lambda hh: k_ref[hh]) + (r + sb[:, 0:LANES])
        r_ref[...] = r + sb[:, LANES:2 * LANES]
        m_old = m_ref[...]
        m_new = jnp.maximum(m_old, s)
        alpha = jnp.exp(m_old - m_new)
        p = jnp.exp(s - m_new)
        l_ref[...] = alpha * l_ref[...] + p
        m_ref[...] = m_new
        for hh in range(ATT_HEADS):
            acc_ref[hh] = alpha[hh:hh + 1, :] * acc_ref[hh] + p[hh:hh + 1, :] * v_ref[hh]


def _paged_attn_finish(qc_ref, knc_ref, vnc_ref, o_ref, m_ref, l_ref, acc_ref):
    m = m_ref[...]
    s_new = _paged_scores(qc_ref, lambda hh: knc_ref[_head_rows(hh), :])
    mx = jnp.maximum(jnp.max(m, axis=1, keepdims=True), s_new)
    w = jnp.exp(m - mx)
    w_new = jnp.exp(s_new - mx)
    denom = jnp.sum(l_ref[...] * w, axis=1, keepdims=True) + w_new
    for hh in range(ATT_HEADS):
        num = (jnp.sum(acc_ref[hh] * w[hh:hh + 1, :], axis=1, keepdims=True)
               + w_new[hh:hh + 1, :] * vnc_ref[_head_rows(hh), :])
        o_ref[0, _head_rows(hh), :] = num / denom[hh:hh + 1, :]


def _sample_out_body(x_ref, att_ref, yt_ref, xs_ref, z_ref, gates_ref, sst_ref, scw_ref, dsk_ref, ng_ref,
                     wo_ref, gpost_ref, gpre_ref, gmpost_ref, wup_ref, wdn_ref, o_ref, snew_ref, *, ff_chunk):
    n = x_ref.shape[0]
    y = yt_ref[...].T[0:n, :] + dsk_ref[...] * xs_ref[...]
    yssd = _rms(y * _silu(z_ref[...]), ng_ref[...])
    gates = gates_ref[...]
    u = gates[:, CONV_WIDTH:2 * CONV_WIDTH] * gates[:, 2 * CONV_WIDTH:3 * CONV_WIDTH]
    uc = scw_ref[0:1, :] * sst_ref[0] + scw_ref[1:2, :] * sst_ref[1] + scw_ref[2:3, :] * u
    yconv = gates[:, 0:CONV_WIDTH] * uc
    snew_ref[0] = sst_ref[1]
    snew_ref[1] = u
    dot = functools.partial(jnp.dot, preferred_element_type=f32)
    mix = (dot(att_ref[...].astype(bf16), wo_ref[0:ATT_WIDTH, :])
           + dot(yssd.astype(bf16), wo_ref[ATT_WIDTH:ATT_WIDTH + SSD_WIDTH, :])
           + dot(yconv.astype(bf16), wo_ref[ATT_WIDTH + SSD_WIDTH:D_MODEL, :]))
    o_ref[...] = _mlp_tail(x_ref[...], mix, gpost_ref[...], gpre_ref[...], gmpost_ref[...],
                           wup_ref, wdn_ref, ff_chunk)


def _sample_out(x, att, yt, xs, z, gates, sst, scw, dsk, ng, wo, gpost, gpre, gmpost, wup, wdn):
    n = x.shape[0]
    return pl.pallas_call(
        functools.partial(_sample_out_body, ff_chunk=min(512, wup.shape[2])),
        out_shape=(jax.ShapeDtypeStruct((n, D_MODEL), f32),
                   jax.ShapeDtypeStruct((SCONV - 1, n, CONV_WIDTH), f32)),
        compiler_params=pltpu.CompilerParams(vmem_limit_bytes=VMEM_LIMIT_BYTES),
        name="sample_out_mlp",
    )(x, att, yt, xs, z, gates, sst, scw, dsk, ng, wo, gpost, gpre, gmpost, wup, wdn)


def _prep_w_in(wt):
    offs = [0]
    for s in IN_SIZES:
        offs.append(offs[-1] + s)
    q, k, v, f, z, xbc, dt, gb, gc, hc = [wt[:, offs[i]:offs[i + 1]] for i in range(len(IN_SIZES))]
    pad = jnp.zeros((wt.shape[0], LANES - ATT_HEADS - SSD_HEADS, wt.shape[2]), wt.dtype)
    return jnp.concatenate([q, k, v, z, xbc, gb, gc, hc, f, dt, pad], axis=1).astype(bf16)


def kernel(x_prompt, x_sample, cache_k, cache_v, cache_logf, state_ssm, state_ssd_conv, state_sconv, page_table,
           w_in, b_f, ssd_conv_w, ssd_conv_b, dt_bias, a_log, d_skip, ssd_norm_g, sconv_w, w_out,
           g_mix_pre, g_mix_post, g_mlp_pre, g_mlp_post, w_mlp_up, w_mlp_down):
    depth = w_in.shape[0]
    B, L, _ = x_prompt.shape
    n = x_sample.shape[0]
    assert x_sample.shape[1] == 1 and n <= LANES
    tm = _pick_tile(L, 512)
    tq = _pick_tile(L, 512)
    tk = _pick_tile(L, 512)

    kt_cache = jnp.transpose(cache_k, (0, 1, 3, 4, 2))
    vt_cache = jnp.transpose(cache_v, (0, 1, 3, 4, 2))
    page_terms = _page_bias(jnp.transpose(cache_logf, (0, 3, 1, 2)))

    w_all = _prep_w_in(jnp.transpose(w_in, (0, 2, 1)))
    wo_all = w_out.astype(bf16)
    nsub = page_table.shape[1] // min(PAGES_PER_STEP, page_table.shape[1])
    fs = D_FF // nsub
    wup_all = jnp.transpose(w_mlp_up.reshape(depth, D_MODEL, nsub, fs), (0, 2, 1, 3)).astype(bf16)
    wdn_all = w_mlp_down.reshape(depth, nsub, fs, D_MODEL).astype(bf16)

    xp = x_prompt
    xs = x_sample.reshape(n, D_MODEL)
    kv_all = None
    outs_p = [[] for _ in range(6)]
    outs_s = [[] for _ in range(6)]
    row = lambda a: a.reshape(1, -1)
    lanes_bcast = lambda a: jnp.broadcast_to(a[:, :, None], a.shape + (LANES,))

    for l in range(depth):
        w = w_all[l]
        wo = wo_all[l]
        wup = wup_all[l]
        wdn = wdn_all[l]
        bias = jnp.concatenate([b_f[l], dt_bias[l], jnp.zeros((LANES - ATT_HEADS - SSD_HEADS,), f32)]).reshape(1, LANES)
        a_neg = -jnp.exp(a_log[l])
        a_row = jnp.concatenate([jnp.zeros((COL_DT,), f32), a_neg,
                                 jnp.zeros((LANES - COL_DT - SSD_HEADS,), f32)]).reshape(1, LANES)
        a_exp = jnp.repeat(a_neg, SSD_HEAD_DIM).reshape(1, SSD_WIDTH)
        dsk = jnp.repeat(d_skip[l], SSD_HEAD_DIM).reshape(1, SSD_WIDTH)
        ng = row(ssd_norm_g[l])
        cw, cb, scw = ssd_conv_w[l], row(ssd_conv_b[l]), sconv_w[l]
        gpre, gpost, gmpre, gmpost = row(g_mix_pre[l]), row(g_mix_post[l]), row(g_mlp_pre[l]), row(g_mlp_post[l])

        (qtp, kp, vtp, kt_all, vt_all, z, xbc, yconv, lfdt, lft, utail) = _prompt_in(
            xp, gpre, w, bias, scw, kv_all, l, depth, tm=tm, tq=tq, tk=tk)
        kv_all = (kt_all, vt_all)
        att = _prompt_attn(qtp, kp, vtp, tq=tq, tk=tk)
        yssd, s_fin = _prompt_ssd(xbc, lfdt, z, cw, cb, a_row, dsk, ng, tm=tm)

        cst = jnp.transpose(state_ssd_conv[l], (1, 0, 2))
        sst = jnp.transpose(state_sconv[l], (1, 0, 2))
        (q, k, v, lf, z_s, gates, xs_s, bm, cm, dtxt, dect, cnew, qt, knt, vnt) = _sample_in(
            xs, gpre, w, bias, cst, cw, cb, a_exp)
        s_new, yt = _sample_ssd(state_ssm[l], dtxt, dect, bm, cm)
        lf6 = lf[:, COL_LOGF:COL_LOGF + ATT_HEADS]
        lfn = lanes_bcast(jnp.pad(lf6, ((0, 0), (0, SUBLANES - ATT_HEADS))))

        xp, att_s = _out_mlp_attn(
            xp.reshape(B * L, D_MODEL), att, yssd.reshape(B * L, SSD_WIDTH),
            yconv.reshape(B * L, CONV_WIDTH), wo, gpost, gmpre, gmpost, wup, wdn,
            page_table, qt, knt, vnt, lfn, kt_cache, vt_cache, page_terms, l,
            tm=tm)
        xp = xp.reshape(B, L, D_MODEL)
        xs, snew = _sample_out(xs, att_s[:, :, 0], yt, xs_s, z_s, gates, sst, scw, dsk, ng,
                               wo, gpost, gmpre, gmpost, wup, wdn)

        outs_p[2].append(lft[:, 0:ATT_HEADS, :])
        outs_p[3].append(s_fin)
        outs_p[4].append(xbc[:, L - (SSD_CONV - 1):, :])
        outs_p[5].append(utail[:, SUBLANES - (SCONV - 1):, :])
        outs_s[0].append(k.reshape(n, 1, ATT_HEADS, HEAD_DIM))
        outs_s[1].append(v.reshape(n, 1, ATT_HEADS, HEAD_DIM))
        outs_s[2].append(lf6.reshape(n, 1, ATT_HEADS))
        outs_s[3].append(s_new)
        outs_s[4].append(jnp.transpose(cnew, (1, 0, 2)))
        outs_s[5].append(jnp.transpose(snew, (1, 0, 2)))

    heads_last = lambda a: jnp.transpose(a.reshape(depth, B, ATT_HEADS, HEAD_DIM, L), (0, 1, 4, 2, 3))
    k_p = heads_last(kv_all[0])
    v_p = heads_last(kv_all[1])
    lf_p = jnp.transpose(jnp.stack(outs_p[2]), (0, 1, 3, 2))
    rest_p = [jnp.stack(a) for a in outs_p[3:]]
    stacked_s = [jnp.stack(a) for a in outs_s]
    return (xp, xs.reshape(n, 1, D_MODEL), k_p, v_p, lf_p, *rest_p, *stacked_s)
```

```python
import functools

import jax
import jax.numpy as jnp
from jax import lax
from jax.experimental import pallas as pl
from jax.experimental.pallas import tpu as pltpu

f32 = jnp.float32
bf16 = jnp.bfloat16

D_MODEL = 1024
HEAD_DIM = 64
ATT_HEADS = 6
ATT_WIDTH = ATT_HEADS * HEAD_DIM
SSD_HEADS = 6
SSD_HEAD_DIM = 64
SSD_WIDTH = SSD_HEADS * SSD_HEAD_DIM
SSD_GROUPS = 2
D_STATE = 64
SSD_CONV = 4
XBC_WIDTH = SSD_WIDTH + 2 * SSD_GROUPS * D_STATE
CONV_WIDTH = 256
SCONV = 3
D_FF = 4 * D_MODEL
SSD_CHUNK = 128
PAGE_SIZE = 128
PAGES_PER_STEP = 16
RMS_EPS = 1e-6
IN_SIZES = (ATT_WIDTH, ATT_WIDTH, ATT_WIDTH, ATT_HEADS, SSD_WIDTH, XBC_WIDTH, SSD_HEADS,
            CONV_WIDTH, CONV_WIDTH, CONV_WIDTH)

LANES = 128
SUBLANES = 8
VMEM_LIMIT_BYTES = 56 * 1024 * 1024

OFF_Q = 0
OFF_K = OFF_Q + ATT_WIDTH
OFF_V = OFF_K + ATT_WIDTH
OFF_Z = OFF_V + ATT_WIDTH
OFF_XBC = OFF_Z + SSD_WIDTH
OFF_GB = OFF_XBC + XBC_WIDTH
OFF_GC = OFF_GB + CONV_WIDTH
OFF_HC = OFF_GC + CONV_WIDTH
OFF_SMALL = OFF_HC + CONV_WIDTH
PROJ_WIDTH = OFF_SMALL + LANES
COL_LOGF = 0
COL_DT = ATT_HEADS
PADDED_HEAD = LANES
NEG_BIG = -1e30
LOG2E = 1.4426950408889634
VALUE_ROWS = 80


def _rms(x, g):
    var = jnp.mean(x * x, axis=-1, keepdims=True)
    return (x * lax.rsqrt(var + RMS_EPS)) * g


def _silu(x):
    return x * (1.0 / (1.0 + jnp.exp(-x)))


def _split3(a):
    a1 = a.astype(bf16)
    r1 = a - a1.astype(f32)
    a2 = r1.astype(bf16)
    a3 = (r1 - a2.astype(f32)).astype(bf16)
    return a1, a2, a3


def _exact_dot(m01, a):
    a1, a2, a3 = _split3(a)
    d = functools.partial(jnp.dot, preferred_element_type=f32)
    return d(m01, a1) + d(m01, a2) + d(m01, a3)


def _exact_dot_rhs(a, m01):
    a1, a2, a3 = _split3(a)
    d = functools.partial(jnp.dot, preferred_element_type=f32)
    return d(a1, m01) + d(a2, m01) + d(a3, m01)


def _dot_nt(a, b):
    return lax.dot_general(a, b, (((1,), (1,)), ((), ())), preferred_element_type=f32)


def _softplus_parts(t):
    sp = jnp.log1p(jnp.exp(-jnp.abs(t)))
    return jnp.minimum(t, 0.0) - sp, jnp.maximum(t, 0.0) + sp


def _shift_rows(x, k, tail):
    row = lax.broadcasted_iota(jnp.int32, x.shape, 0)
    y = pltpu.roll(x, k, 0)
    for r in range(k):
        y = jnp.where(row == r, tail[SUBLANES - k + r:SUBLANES - k + r + 1, :], y)
    return y


def _pick_tile(n, pref):
    t = min(n, pref)
    while n % t:
        t -= SUBLANES
    assert t > 0
    return t


def _prompt_in_body(x_ref, g_ref, w_ref, bias_ref, tri_ref, scw_ref, *refs, tq, tk, carried):
    _prompt_in_compute(x_ref, g_ref, w_ref, bias_ref, tri_ref, scw_ref, *refs[carried:], tq=tq, tk=tk)


def _prompt_in_compute(x_ref, g_ref, w_ref, bias_ref, tri_ref, scw_ref,
                    qtp_ref, kp_ref, vtp_ref, kt32_ref, vt32_ref, z_ref, xbc_ref, yconv_ref,
                    lfdt_ref, lft_ref, utail_ref,
                    carry_ref, tail_ref, *, tq, tk):
    tm = x_ref.shape[1]

    @pl.when(pl.program_id(1) == 0)
    def _():
        carry_ref[...] = jnp.zeros_like(carry_ref)
        tail_ref[...] = jnp.zeros_like(tail_ref)

    h = _rms(x_ref[0], g_ref[...]).astype(bf16)
    proj = lambda lo, hi: _dot_nt(h, w_ref[lo:hi, :])

    t = proj(OFF_SMALL, PROJ_WIDTH) + bias_ref[...]
    logf, dt = _softplus_parts(t)
    col = lax.broadcasted_iota(jnp.int32, t.shape, 1)
    lfdt_ref[0] = jnp.where(col < COL_DT, logf, dt)
    lft_ref[0] = logf.T[0:SUBLANES, :]
    carry = carry_ref[...]
    blocks = []
    for r0 in range(0, tm, SSD_CHUNK):
        cb = _exact_dot(tri_ref[...], logf[r0:r0 + SSD_CHUNK]) + carry
        carry = cb[SSD_CHUNK - 1:SSD_CHUNK, :]
        blocks.append(cb)
    carry_ref[...] = carry
    c = jnp.concatenate(blocks, axis=0) * LOG2E
    c1, c2, c3 = [p.astype(f32) for p in _split3(c)]
    ct1, ct2, ct3 = [p.astype(f32) for p in _split3(c.T[0:SUBLANES, :])]

    qkv_t = _dot_nt(w_ref[OFF_Q:OFF_Z, :], h)
    kt = qkv_t[ATT_WIDTH:2 * ATT_WIDTH]
    vt = qkv_t[2 * ATT_WIDTH:3 * ATT_WIDTH]
    kt32_ref[0] = kt
    vt32_ref[0] = vt
    k = kt.T
    lane = lax.broadcasted_iota(jnp.int32, (tm, PADDED_HEAD - HEAD_DIM), 1)
    row = lax.broadcasted_iota(jnp.int32, (PADDED_HEAD - HEAD_DIM, tm), 0)
    vrow = lax.broadcasted_iota(jnp.int32, (VALUE_ROWS - HEAD_DIM, tm), 0)
    ones_row = jnp.where(vrow == 0, 1.0, 0.0).astype(bf16)
    for hh in range(ATT_HEADS):
        src = slice(HEAD_DIM * hh, HEAD_DIM * (hh + 1))
        lo = slice(PADDED_HEAD * hh, PADDED_HEAD * hh + HEAD_DIM)
        hi = slice(PADDED_HEAD * hh + HEAD_DIM, PADDED_HEAD * (hh + 1))
        hc = slice(hh, hh + 1)
        kp_ref[0, :, lo] = k[:, src].astype(bf16)
        ek = jnp.where(lane < 3, 1.0, jnp.where(lane == 3, -c1[:, hc], jnp.where(
            lane == 4, -c2[:, hc], jnp.where(lane == 5, -c3[:, hc], 0.0))))
        kp_ref[0, :, hi] = ek.astype(bf16)
        eq = jnp.where(row == 0, ct1[hc, :], jnp.where(row == 1, ct2[hc, :], jnp.where(
            row == 2, ct3[hc, :], jnp.where(row < 6, 1.0, 0.0)))).astype(bf16)
        qh = (qkv_t[src] * (HEAD_DIM ** -0.5 * LOG2E)).astype(bf16)
        for i in range(tm // tq):
            cs = slice(i * tq, (i + 1) * tq)
            qtp_ref[0, i, lo, :] = qh[:, cs]
            qtp_ref[0, i, hi, :] = eq[:, cs]
        vh = vt[src].astype(bf16)
        for i in range(tm // tk):
            cs = slice(i * tk, (i + 1) * tk)
            vtp_ref[0, i, VALUE_ROWS * hh:VALUE_ROWS * hh + HEAD_DIM, :] = vh[:, cs]
            vtp_ref[0, i, VALUE_ROWS * hh + HEAD_DIM:VALUE_ROWS * (hh + 1), :] = ones_row[:, cs]

    z_ref[0] = proj(OFF_Z, OFF_XBC)
    xbc_ref[0] = proj(OFF_XBC, OFF_GB)

    gates = proj(OFF_GB, OFF_SMALL)
    gb = gates[:, 0:CONV_WIDTH]
    u = gates[:, CONV_WIDTH:2 * CONV_WIDTH] * gates[:, 2 * CONV_WIDTH:3 * CONV_WIDTH]
    tail = tail_ref[...]
    uc = (scw_ref[0:1, :] * _shift_rows(u, 2, tail) + scw_ref[1:2, :] * _shift_rows(u, 1, tail)
          + scw_ref[2:3, :] * u)
    yconv_ref[0] = (gb * uc).astype(bf16)
    tail_ref[...] = u[tm - SUBLANES:tm, :]
    utail_ref[0] = u[tm - SUBLANES:tm, :]


def _prompt_in(x, g, w, bias, scw, kv_all, layer, depth, *, tm, tq, tk):
    B, L, _ = x.shape
    nj = L // tm
    pw = ATT_HEADS * PADDED_HEAD
    tri = jnp.tril(jnp.ones((SSD_CHUNK, SSD_CHUNK), f32)).astype(bf16)
    row = lambda width: pl.BlockSpec((1, tm, width), lambda b, j: (b, j, 0))
    colb = lambda height: pl.BlockSpec((1, height, tm), lambda b, j: (b, 0, j))
    const = lambda shape: pl.BlockSpec(shape, lambda b, j: (0,) * len(shape))
    layer_colb = pl.BlockSpec((None, 1, ATT_WIDTH, tm), lambda b, j: (layer, b, 0, j))
    carried = () if kv_all is None else tuple(kv_all)
    out_shapes = (
        jax.ShapeDtypeStruct((B, L // tq, pw, tq), bf16),
        jax.ShapeDtypeStruct((B, L, pw), bf16),
        jax.ShapeDtypeStruct((B, L // tk, ATT_HEADS * VALUE_ROWS, tk), bf16),
        jax.ShapeDtypeStruct((depth, B, ATT_WIDTH, L), f32),
        jax.ShapeDtypeStruct((depth, B, ATT_WIDTH, L), f32),
        jax.ShapeDtypeStruct((B, L, SSD_WIDTH), f32),
        jax.ShapeDtypeStruct((B, L, XBC_WIDTH), f32),
        jax.ShapeDtypeStruct((B, L, CONV_WIDTH), bf16),
        jax.ShapeDtypeStruct((B, L, LANES), f32),
        jax.ShapeDtypeStruct((B, SUBLANES, L), f32),
        jax.ShapeDtypeStruct((B, SUBLANES, CONV_WIDTH), f32),
    )
    out_specs = (
        pl.BlockSpec((1, tm // tq, pw, tq), lambda b, j: (b, j, 0, 0)),
        row(pw),
        pl.BlockSpec((1, tm // tk, ATT_HEADS * VALUE_ROWS, tk), lambda b, j: (b, j, 0, 0)),
        layer_colb, layer_colb, row(SSD_WIDTH), row(XBC_WIDTH), row(CONV_WIDTH),
        row(LANES), colb(SUBLANES),
        pl.BlockSpec((1, SUBLANES, CONV_WIDTH), lambda b, j: (b, 0, 0)),
    )
    return pl.pallas_call(
        functools.partial(_prompt_in_body, tq=tq, tk=tk, carried=len(carried)),
        out_shape=out_shapes,
        grid=(B, nj),
        in_specs=[row(D_MODEL), const((1, D_MODEL)), const((PROJ_WIDTH, D_MODEL)), const((1, LANES)),
                  const((SSD_CHUNK, SSD_CHUNK)), const((SCONV, CONV_WIDTH))]
        + [pl.BlockSpec(memory_space=pl.ANY)] * len(carried),
        input_output_aliases={6 + i: 3 + i for i in range(len(carried))},
        out_specs=out_specs,
        scratch_shapes=[pltpu.VMEM((1, LANES), f32), pltpu.VMEM((SUBLANES, CONV_WIDTH), f32)],
        compiler_params=pltpu.CompilerParams(dimension_semantics=("arbitrary", "arbitrary"),
                                             vmem_limit_bytes=VMEM_LIMIT_BYTES),
        name="prompt_in",
    )(x, g, w, bias, tri, scw, *carried)


def _prompt_attn_body(qt_ref, kp_ref, vt_ref, o_ref, m_ref, acc_ref, st_ref, *, tq, tk):
    qi = pl.program_id(1)
    m_ref[...] = jnp.full(m_ref.shape, NEG_BIG, f32)
    acc_ref[...] = jnp.zeros_like(acc_ref)
    nfull = lax.div(qi * tq, tk)
    heads = [slice(PADDED_HEAD * hh, PADDED_HEAD * (hh + 1)) for hh in range(ATT_HEADS)]
    vrows = [slice(VALUE_ROWS * hh, VALUE_ROWS * (hh + 1)) for hh in range(ATT_HEADS)]

    def score(kj, slot, masked):
        ks = pl.multiple_of(kj * tk, tk)
        if masked:
            kpos = ks + lax.broadcasted_iota(jnp.int32, (tk, tq), 0)
            qpos = qi * tq + lax.broadcasted_iota(jnp.int32, (tk, tq), 1)
            valid = kpos <= qpos
        for hh, hs in enumerate(heads):
            st = jnp.dot(kp_ref[0, pl.ds(ks, tk), hs], qt_ref[0, 0, hs, :], preferred_element_type=f32)
            st_ref[slot, hh] = jnp.where(valid, st, NEG_BIG) if masked else st

    def absorb(kj, slot):
        probs = []
        for hh in range(ATT_HEADS):
            st = st_ref[slot, hh]
            m_old = m_ref[hh]
            m_new = jnp.maximum(m_old, jnp.max(st, axis=0, keepdims=True))
            m_ref[hh] = m_new
            probs.append((jnp.exp2(m_old[0:1] - m_new[0:1]), jnp.exp2(st - m_new[0:1]).astype(bf16)))
        for hh, (alpha, pt) in enumerate(probs):
            acc_ref[hh] = alpha * acc_ref[hh] + jnp.dot(vt_ref[0, kj, vrows[hh], :], pt,
                                                        preferred_element_type=f32)

    @pl.when(nfull > 0)
    def _():
        score(0, 0, False)

    def pair(pi, carry):
        k0 = 2 * pi
        score(k0 + 1, 1, False)
        absorb(k0, 0)
        score(k0 + 2, 0, False)
        absorb(k0 + 1, 1)
        return carry

    ntrip = lax.div(jnp.maximum(nfull - 1, 0), 2)
    lax.fori_loop(0, ntrip, pair, 0)
    done = 2 * ntrip
    rem = nfull - done

    @pl.when(rem == 2)
    def _():
        score(done + 1, 1, False)
        absorb(done, 0)
        score(done + 2, 0, True)
        absorb(done + 1, 1)
        absorb(done + 2, 0)

    @pl.when(rem == 1)
    def _():
        score(done + 1, 1, True)
        absorb(done, 0)
        absorb(done + 1, 1)

    @pl.when(rem == 0)
    def _():
        score(0, 0, True)
        absorb(0, 0)

    for hh in range(ATT_HEADS):
        a = acc_ref[hh]
        o_ref[0, HEAD_DIM * hh:HEAD_DIM * (hh + 1), :] = (a[0:HEAD_DIM] / a[HEAD_DIM:HEAD_DIM + 1]).astype(bf16)


def _prompt_attn(qtp, kp, vtp, *, tq, tk):
    B, nq, W, _ = qtp.shape
    L = kp.shape[1]
    vw = ATT_HEADS * VALUE_ROWS
    return pl.pallas_call(
        functools.partial(_prompt_attn_body, tq=tq, tk=tk),
        out_shape=jax.ShapeDtypeStruct((B, ATT_WIDTH, L), bf16),
        grid=(B, nq),
        in_specs=[pl.BlockSpec((1, 1, W, tq), lambda b, i: (b, i, 0, 0)),
                  pl.BlockSpec((1, L, W), lambda b, i: (b, 0, 0)),
                  pl.BlockSpec((1, L // tk, vw, tk), lambda b, i: (b, 0, 0, 0))],
        out_specs=pl.BlockSpec((1, ATT_WIDTH, tq), lambda b, i: (b, 0, i)),
        scratch_shapes=[pltpu.VMEM((ATT_HEADS, SUBLANES, tq), f32),
                        pltpu.VMEM((ATT_HEADS, VALUE_ROWS, tq), f32),
                        pltpu.VMEM((2, ATT_HEADS, tk, tq), f32)],
        compiler_params=pltpu.CompilerParams(dimension_semantics=("arbitrary", "arbitrary"),
                                             vmem_limit_bytes=VMEM_LIMIT_BYTES),
        name="prompt_attn",
    )(qtp, kp, vtp)


def _prompt_ssd_body(xbc_ref, lfdt_ref, z_ref, cw_ref, cb_ref, a_ref, dsk_ref, ng_ref, tri_ref,
                     y_ref, sout_ref, s_ref, tail_ref):
    tm = xbc_ref.shape[1]
    Q = SSD_CHUNK

    @pl.when(pl.program_id(1) == 0)
    def _():
        s_ref[...] = jnp.zeros_like(s_ref)
        tail_ref[...] = jnp.zeros_like(tail_ref)

    x = xbc_ref[0]
    tail = tail_ref[...]
    xc = (cw_ref[0:1, :] * _shift_rows(x, 3, tail) + cw_ref[1:2, :] * _shift_rows(x, 2, tail)
          + cw_ref[2:3, :] * _shift_rows(x, 1, tail) + cw_ref[3:4, :] * x + cb_ref[...])
    xc = _silu(xc)
    tail_ref[...] = x[tm - SUBLANES:tm, :]

    dt_all = lfdt_ref[0]
    dta_all = dt_all * a_ref[...]
    row = lax.broadcasted_iota(jnp.int32, (Q, Q), 0)
    colm = lax.broadcasted_iota(jnp.int32, (Q, Q), 1)
    causal = row >= colm
    tri = tri_ref[...]
    group = lambda hh: slice((hh // (SSD_HEADS // SSD_GROUPS)) * D_STATE,
                             (hh // (SSD_HEADS // SSD_GROUPS) + 1) * D_STATE)
    head = lambda hh: slice(hh * SSD_HEAD_DIM, (hh + 1) * SSD_HEAD_DIM)

    states = [s_ref[hh] for hh in range(SSD_HEADS)]
    for c in range(tm // Q):
        rs = slice(c * Q, (c + 1) * Q)
        dt = dt_all[rs]
        acum = _exact_dot(tri, dta_all[rs])
        acum_t = acum.T
        dt_t = dt.T
        xs = xc[rs, 0:SSD_WIDTH]
        xs_t = xs.T.astype(bf16)
        xs_b = xs.astype(bf16)
        bm = xc[rs, SSD_WIDTH:SSD_WIDTH + SSD_GROUPS * D_STATE]
        cm = xc[rs, SSD_WIDTH + SSD_GROUPS * D_STATE:XBC_WIDTH].astype(bf16)
        cb = []
        for g in range(SSD_GROUPS):
            gs = slice(g * D_STATE, (g + 1) * D_STATE)
            cb.append(_dot_nt(cm[:, gs], bm[:, gs].astype(bf16)))
        terms = []
        for hh in range(SSD_HEADS):
            cc = COL_DT + hh
            a_col = acum[:, cc:cc + 1]
            a_row = acum_t[cc:cc + 1, :]
            a_last = acum[Q - 1:Q, cc:cc + 1]
            decay = jnp.exp(jnp.where(causal, a_col - a_row, -jnp.inf))
            mat = (cb[hh // (SSD_HEADS // SSD_GROUPS)] * decay * dt_t[cc:cc + 1, :]).astype(bf16)
            wcol = jnp.exp(a_last - a_col) * dt[:, cc:cc + 1]
            bw = (bm[:, group(hh)] * wcol).astype(bf16)
            terms.append((mat, bw, jnp.exp(a_col), jnp.exp(a_last)))
        ys = []
        for hh, (mat, bw, e_col, e_last) in enumerate(terms):
            s_in = states[hh]
            y_intra = jnp.dot(mat, xs_b[:, head(hh)], preferred_element_type=f32)
            y_inter = _dot_nt(cm[:, group(hh)], s_in.astype(bf16)) * e_col
            ys.append(y_intra + y_inter)
            states[hh] = e_last * s_in + jnp.dot(xs_t[head(hh), :], bw, preferred_element_type=f32)
        y = jnp.concatenate(ys, axis=1) + dsk_ref[...] * xs
        gated = y * _silu(z_ref[0, rs, :])
        y_ref[0, rs, :] = _rms(gated, ng_ref[...]).astype(bf16)
    for hh in range(SSD_HEADS):
        s_ref[hh] = states[hh]
    sout_ref[0] = jnp.stack(states)


def _prompt_ssd(xbc, lfdt, z, cw, cb, a_row, dsk, ng, *, tm):
    B, L, _ = xbc.shape
    tri = jnp.tril(jnp.ones((SSD_CHUNK, SSD_CHUNK), f32)).astype(bf16)
    row = lambda width: pl.BlockSpec((1, tm, width), lambda b, j: (b, j, 0))
    const = lambda shape: pl.BlockSpec(shape, lambda b, j: (0,) * len(shape))
    return pl.pallas_call(
        _prompt_ssd_body,
        out_shape=(jax.ShapeDtypeStruct((B, L, SSD_WIDTH), bf16),
                   jax.ShapeDtypeStruct((B, SSD_HEADS, SSD_HEAD_DIM, D_STATE), f32)),
        grid=(B, L // tm),
        in_specs=[row(XBC_WIDTH), row(LANES), row(SSD_WIDTH), const((SSD_CONV, XBC_WIDTH)),
                  const((1, XBC_WIDTH)), const((1, LANES)), const((1, SSD_WIDTH)), const((1, SSD_WIDTH)),
                  const((SSD_CHUNK, SSD_CHUNK))],
        out_specs=(row(SSD_WIDTH),
                   pl.BlockSpec((1, SSD_HEADS, SSD_HEAD_DIM, D_STATE), lambda b, j: (b, 0, 0, 0))),
        scratch_shapes=[pltpu.VMEM((SSD_HEADS, SSD_HEAD_DIM, D_STATE), f32),
                        pltpu.VMEM((SUBLANES, XBC_WIDTH), f32)],
        compiler_params=pltpu.CompilerParams(dimension_semantics=("arbitrary", "arbitrary"),
                                             vmem_limit_bytes=VMEM_LIMIT_BYTES),
        name="prompt_ssd",
    )(xbc, lfdt, z, cw, cb, a_row, dsk, ng, tri)


def _mlp_tail(x, mix, gpost, gpre, gmpost, wup_ref, wdn_ref, ff_chunk):
    x1 = x + _rms(mix, gpost)
    hmid = _rms(x1, gpre).astype(bf16)
    acc = jnp.zeros_like(x1)
    nsub, _, fs = wup_ref.shape
    for j in range(nsub):
        for c in range(fs // ff_chunk):
            cs = slice(c * ff_chunk, (c + 1) * ff_chunk)
            up = jnp.dot(hmid, wup_ref[j, :, cs], preferred_element_type=f32)
            act = jnp.square(jnp.maximum(up, 0.0)).astype(bf16)
            acc = acc + jnp.dot(act, wdn_ref[j, cs, :], preferred_element_type=f32)
    return x1 + _rms(acc, gmpost)


def _out_mlp_attn_body(pt_ref, x_ref, att_ref, yssd_ref, yconv_ref, wo_ref, gpost_ref, gpre_ref, gmpost_ref,
                       wup_ref, wdn_ref, qt_ref, knt_ref, vnt_ref, lfn_ref, *rest, pps, ff_chunk):
    k_refs = rest[0:pps]
    v_refs = rest[pps:2 * pps]
    sb_refs = rest[2 * pps:3 * pps]
    o_ref, oa_ref, x1_ref, h_ref, acc_ref, m_ref, l_ref, acca_ref, r_ref, qc_ref, knc_ref, vnc_ref = rest[3 * pps:]
    j = pl.program_id(1)
    last = pl.num_programs(1) - 1
    dot = functools.partial(jnp.dot, preferred_element_type=f32)

    @pl.when(j == 0)
    def _():
        att_mix = lax.dot_general(att_ref[0], wo_ref[0:ATT_WIDTH, :], (((0,), (0,)), ((), ())),
                                  preferred_element_type=f32)
        mix = (att_mix
               + dot(yssd_ref[...], wo_ref[ATT_WIDTH:ATT_WIDTH + SSD_WIDTH, :])
               + dot(yconv_ref[...], wo_ref[ATT_WIDTH + SSD_WIDTH:D_MODEL, :]))
        x1 = x_ref[...] + _rms(mix, gpost_ref[...])
        x1_ref[...] = x1
        h_ref[...] = _rms(x1, gpre_ref[...]).astype(bf16)
        acc_ref[...] = jnp.zeros_like(acc_ref)
        _paged_attn_init(lfn_ref, m_ref, l_ref, acca_ref, r_ref)
        mine = lax.broadcasted_iota(jnp.int32, qt_ref.shape, 1) == pl.program_id(0)
        for src, dst in ((qt_ref, qc_ref), (knt_ref, knc_ref), (vnt_ref, vnc_ref)):
            col = jnp.sum(jnp.where(mine, src[...], 0.0), axis=1, keepdims=True)
            dst[...] = jnp.broadcast_to(col, dst.shape)

    nchunk = wup_ref.shape[2] // ff_chunk
    half = ff_chunk // 2
    groups = 4 * nchunk
    per = -(-pps // groups)
    pages = lambda g: _paged_attn_pages(qc_ref, k_refs[g * per:(g + 1) * per], v_refs[g * per:(g + 1) * per],
                                        sb_refs[g * per:(g + 1) * per], m_ref, l_ref, acca_ref, r_ref)
    relu2 = lambda u: jnp.square(jnp.maximum(u, 0.0)).astype(bf16)
    hmid = h_ref[...]
    acc = acc_ref[...]
    for c in range(nchunk):
        lo = slice(c * ff_chunk, c * ff_chunk + half)
        hi = slice(c * ff_chunk + half, (c + 1) * ff_chunk)
        up_lo = dot(hmid, wup_ref[j, :, lo])
        pages(4 * c)
        up_hi = dot(hmid, wup_ref[j, :, hi])
        pages(4 * c + 1)
        acc = acc + dot(relu2(up_lo), wdn_ref[j, lo, :])
        pages(4 * c + 2)
        acc = acc + dot(relu2(up_hi), wdn_ref[j, hi, :])
        pages(4 * c + 3)
    acc_ref[...] = acc

    @pl.when(j == last)
    def _():
        o_ref[...] = x1_ref[...] + _rms(acc_ref[...], gmpost_ref[...])
        _paged_attn_finish(qc_ref, knc_ref, vnc_ref, oa_ref, m_ref, l_ref, acca_ref)


def _out_mlp_attn(x, att, yssd, yconv, wo, gpost, gpre, gmpost, wup, wdn,
                  page_table, qt, knt, vnt, lfn, kt, vt, sb, layer, *, tm):
    M = x.shape[0]
    n, n_pages = page_table.shape
    pps = min(PAGES_PER_STEP, n_pages)
    nsub = n_pages // pps
    assert M // tm == n and n_pages % pps == 0 and D_FF % nsub == 0
    fs = D_FF // nsub
    assert wup.shape == (nsub, D_MODEL, fs) and wdn.shape == (nsub, fs, D_MODEL)
    pt = page_table.reshape(-1)
    row = lambda width: pl.BlockSpec((tm, width), lambda i, j, pt: (i, 0))
    nlb = att.shape[2] // tm
    att_spec = pl.BlockSpec((1, ATT_WIDTH, tm), lambda i, j, pt: (i // nlb, 0, i % nlb))
    const = lambda shape: pl.BlockSpec(shape, lambda i, j, pt: (0,) * len(shape), pipeline_mode=pl.Buffered(1))
    seq = lambda rows: pl.BlockSpec((1, rows, LANES), lambda i, j, pt: (i, 0, 0))

    def page(i, j, pt, k):
        return pt[i * n_pages + n_pages - 1 - (j * pps + k)]

    kv_specs = [pl.BlockSpec((None, None, ATT_HEADS, HEAD_DIM, PAGE_SIZE),
                             lambda i, j, pt, k=k: (layer, page(i, j, pt, k), 0, 0, 0)) for k in range(pps)]
    sb_specs = [pl.BlockSpec((None, None, SUBLANES, 2 * LANES),
                             lambda i, j, pt, k=k: (layer, page(i, j, pt, k), 0, 0)) for k in range(pps)]
    return pl.pallas_call(
        functools.partial(_out_mlp_attn_body, pps=pps, ff_chunk=min(512, fs)),
        out_shape=(jax.ShapeDtypeStruct((M, D_MODEL), f32), jax.ShapeDtypeStruct((n, ATT_WIDTH, LANES), f32)),
        grid_spec=pltpu.PrefetchScalarGridSpec(
            num_scalar_prefetch=1,
            grid=(n, nsub),
            in_specs=[row(D_MODEL), att_spec, row(SSD_WIDTH), row(CONV_WIDTH),
                      const((D_MODEL, D_MODEL)), const((1, D_MODEL)), const((1, D_MODEL)), const((1, D_MODEL)),
                      const((nsub, D_MODEL, fs)), const((nsub, fs, D_MODEL)),
                      const((ATT_WIDTH, LANES)), const((ATT_WIDTH, LANES)), const((ATT_WIDTH, LANES)), seq(SUBLANES)]
            + kv_specs + kv_specs + sb_specs,
            out_specs=(row(D_MODEL), seq(ATT_WIDTH)),
            scratch_shapes=[pltpu.VMEM((tm, D_MODEL), f32), pltpu.VMEM((tm, D_MODEL), bf16),
                            pltpu.VMEM((tm, D_MODEL), f32),
                            pltpu.VMEM((SUBLANES, LANES), f32), pltpu.VMEM((SUBLANES, LANES), f32),
                            pltpu.VMEM((ATT_HEADS, HEAD_DIM, LANES), f32), pltpu.VMEM((SUBLANES, LANES), f32),
                            pltpu.VMEM((ATT_WIDTH, LANES), f32), pltpu.VMEM((ATT_WIDTH, LANES), f32),
                            pltpu.VMEM((ATT_WIDTH, LANES), f32)],
        ),
        compiler_params=pltpu.CompilerParams(dimension_semantics=("arbitrary", "arbitrary"),
                                             vmem_limit_bytes=VMEM_LIMIT_BYTES),
        name="out_mlp_sample_attn",
    )(pt, x, att, yssd, yconv, wo, gpost, gpre, gmpost, wup, wdn, qt, knt, vnt, lfn,
      *([kt] * pps), *([vt] * pps), *([sb] * pps))


def _sample_in_body(x_ref, g_ref, w_ref, bias_ref, cst_ref, cw_ref, cb_ref, aexp_ref,
                    q_ref, k_ref, v_ref, lf_ref, z_ref, gates_ref, xs_ref, bm_ref, cm_ref,
                    dtxt_ref, dect_ref, cnew_ref, qt_ref, kt_ref, vt_ref):
    n = x_ref.shape[0]
    h = _rms(x_ref[...], g_ref[...]).astype(bf16)
    proj = _dot_nt(h, w_ref[...])
    q_ref[...] = proj[:, OFF_Q:OFF_K] * (HEAD_DIM ** -0.5)
    k_ref[...] = proj[:, OFF_K:OFF_V]
    v_ref[...] = proj[:, OFF_V:OFF_Z]
    z_ref[...] = proj[:, OFF_Z:OFF_XBC]
    gates_ref[...] = proj[:, OFF_GB:OFF_SMALL]
    logf, dt = _softplus_parts(proj[:, OFF_SMALL:PROJ_WIDTH] + bias_ref[...])
    lf_ref[...] = logf

    xbc = proj[:, OFF_XBC:OFF_GB]
    xc = (cw_ref[0:1, :] * cst_ref[0] + cw_ref[1:2, :] * cst_ref[1] + cw_ref[2:3, :] * cst_ref[2]
          + cw_ref[3:4, :] * xbc + cb_ref[...])
    xc = _silu(xc)
    cnew_ref[0] = cst_ref[1]
    cnew_ref[1] = cst_ref[2]
    cnew_ref[2] = xbc
    xs = xc[:, 0:SSD_WIDTH]
    xs_ref[...] = xs
    bm_ref[...] = xc[:, SSD_WIDTH:SSD_WIDTH + SSD_GROUPS * D_STATE]
    cm_ref[...] = xc[:, SSD_WIDTH + SSD_GROUPS * D_STATE:XBC_WIDTH]

    head = lax.broadcasted_iota(jnp.int32, (n, SSD_WIDTH), 1) // SSD_HEAD_DIM
    dt_exp = jnp.zeros((n, SSD_WIDTH), f32)
    for hh in range(SSD_HEADS):
        dt_exp = jnp.where(head == hh, dt[:, COL_DT + hh:COL_DT + hh + 1], dt_exp)
    pad = jnp.zeros((LANES - n, SSD_WIDTH), f32)
    lanes_t = lambda a: jnp.concatenate([a, pad], axis=0).T
    dtxt_ref[...] = lanes_t(dt_exp * xs)
    dect_ref[...] = lanes_t(jnp.exp(dt_exp * aexp_ref[...]))
    qt_ref[...] = lanes_t(q_ref[...])
    kt_ref[...] = lanes_t(k_ref[...])
    vt_ref[...] = lanes_t(v_ref[...])


def _sample_in(x, g, w, bias, cst, cw, cb, aexp):
    n = x.shape[0]
    s = lambda *shape: jax.ShapeDtypeStruct(shape, f32)
    return pl.pallas_call(
        _sample_in_body,
        out_shape=(s(n, ATT_WIDTH), s(n, ATT_WIDTH), s(n, ATT_WIDTH), s(n, LANES), s(n, SSD_WIDTH),
                   s(n, 3 * CONV_WIDTH), s(n, SSD_WIDTH), s(n, SSD_GROUPS * D_STATE), s(n, SSD_GROUPS * D_STATE),
                   s(SSD_WIDTH, LANES), s(SSD_WIDTH, LANES), s(SSD_CONV - 1, n, XBC_WIDTH),
                   s(ATT_WIDTH, LANES), s(ATT_WIDTH, LANES), s(ATT_WIDTH, LANES)),
        compiler_params=pltpu.CompilerParams(vmem_limit_bytes=VMEM_LIMIT_BYTES),
        name="sample_in",
    )(x, g, w, bias, cst, cw, cb, aexp)


def _sample_ssd_body(s_ref, dtxt_ref, dect_ref, bm_ref, cm_ref, snew_ref, yt_ref, *, per_step):
    rows = SSD_WIDTH
    half = rows // SSD_GROUPS
    lane = lax.broadcasted_iota(jnp.int32, (rows, LANES), 1)

    def expand(ref, b):
        r = ref[pl.ds(b, 1), :]
        return jnp.concatenate([jnp.broadcast_to(r[:, g * D_STATE:(g + 1) * D_STATE], (half, D_STATE))
                                for g in range(SSD_GROUPS)], axis=0)

    @pl.when(pl.program_id(0) == 0)
    def _():
        yt_ref[...] = jnp.zeros_like(yt_ref)

    y_add = jnp.zeros((rows, LANES), f32)
    for r in range(per_step):
        b = pl.program_id(0) * per_step + r
        sel = lane == b
        dcol = jnp.sum(jnp.where(sel, dect_ref[...], 0.0), axis=1, keepdims=True)
        xcol = jnp.sum(jnp.where(sel, dtxt_ref[...], 0.0), axis=1, keepdims=True)
        s_old = s_ref[r].reshape(rows, D_STATE)
        s_new = dcol * s_old + xcol * expand(bm_ref, b)
        snew_ref[r] = s_new.reshape(SSD_HEADS, SSD_HEAD_DIM, D_STATE)
        ycol = jnp.sum(s_new * expand(cm_ref, b), axis=1, keepdims=True)
        y_add = jnp.where(sel, ycol, y_add)
    yt_ref[...] += y_add


def _sample_ssd(state, dtxt, dect, bm, cm):
    n = state.shape[0]
    per_step = 4 if n % 4 == 0 else 1
    const = lambda shape: pl.BlockSpec(shape, lambda b: (0, 0))
    sblk = pl.BlockSpec((per_step, SSD_HEADS, SSD_HEAD_DIM, D_STATE), lambda b: (b, 0, 0, 0))
    return pl.pallas_call(
        functools.partial(_sample_ssd_body, per_step=per_step),
        out_shape=(jax.ShapeDtypeStruct(state.shape, f32), jax.ShapeDtypeStruct((SSD_WIDTH, LANES), f32)),
        grid=(n // per_step,),
        in_specs=[sblk, const((SSD_WIDTH, LANES)), const((SSD_WIDTH, LANES)),
                  const(bm.shape), const(cm.shape)],
        out_specs=(sblk, const((SSD_WIDTH, LANES))),
        compiler_params=pltpu.CompilerParams(dimension_semantics=("arbitrary",)),
        name="sample_ssd",
    )(state, dtxt, dect, bm, cm)


def _page_bias_body(lf_ref, upper_ref, o_ref):
    pb = lf_ref.shape[1]
    for hh in range(ATT_HEADS):
        x = lf_ref[hh]
        o_ref[:, hh, 0:LANES] = _exact_dot_rhs(x, upper_ref[...])
        o_ref[:, hh, LANES:2 * LANES] = jnp.broadcast_to(jnp.sum(x, axis=1, keepdims=True), x.shape)
    o_ref[:, ATT_HEADS:SUBLANES, :] = jnp.zeros((pb, SUBLANES - ATT_HEADS, 2 * LANES), f32)


def _page_bias(lft):
    depth, _, pool, _ = lft.shape
    pb = _pick_tile(pool, 256)
    pos = jnp.arange(PAGE_SIZE)
    upper = (pos[:, None] > pos[None, :]).astype(bf16)
    return pl.pallas_call(
        _page_bias_body,
        out_shape=jax.ShapeDtypeStruct((depth, pool, SUBLANES, 2 * LANES), f32),
        grid=(depth, pool // pb),
        in_specs=[pl.BlockSpec((None, ATT_HEADS, pb, PAGE_SIZE), lambda d, i: (d, 0, i, 0)),
                  pl.BlockSpec((PAGE_SIZE, PAGE_SIZE), lambda d, i: (0, 0))],
        out_specs=pl.BlockSpec((None, pb, SUBLANES, 2 * LANES), lambda d, i: (d, i, 0, 0)),
        compiler_params=pltpu.CompilerParams(dimension_semantics=("arbitrary", "arbitrary")),
        name="page_bias",
    )(lft, upper)


def _head_rows(hh):
    return slice(HEAD_DIM * hh, HEAD_DIM * (hh + 1))


def _paged_scores(qc_ref, k_of_head):
    hrow = lax.broadcasted_iota(jnp.int32, (SUBLANES, LANES), 0)
    s = jnp.zeros((SUBLANES, LANES), f32)
    for hh in range(ATT_HEADS):
        prod = qc_ref[_head_rows(hh), :] * k_of_head(hh)
        s = jnp.where(hrow == hh, jnp.sum(prod, axis=0, keepdims=True), s)
    return s


def _paged_attn_init(lfn_ref, m_ref, l_ref, acc_ref, r_ref):
    m_ref[...] = jnp.full(m_ref.shape, NEG_BIG, f32)
    l_ref[...] = jnp.zeros_like(l_ref)
    acc_ref[...] = jnp.zeros_like(acc_ref)
    r_ref[...] = lfn_ref[0]


def _paged_attn_pages(qc_ref, k_refs, v_refs, sb_refs, m_ref, l_ref, acc_ref, r_ref):
    for k_ref, v_ref, sb_ref in zip(k_refs, v_refs, sb_refs):
        sb = sb_ref[...]
        r = r_ref[...]
        s = _paged_scores(qc_ref, lambda hh: k_ref[hh]) + (r + sb[:, 0:LANES])
        r_ref[...] = r + sb[:, LANES:2 * LANES]
        m_old = m_ref[...]
        m_new = jnp.maximum(m_old, s)
        alpha = jnp.exp(m_old - m_new)
        p = jnp.exp(s - m_new)
        l_ref[...] = alpha * l_ref[...] + p
        m_ref[...] = m_new
        for hh in range(ATT_HEADS):
            acc_ref[hh] = alpha[hh:hh + 1, :] * acc_ref[hh] + p[hh:hh + 1, :] * v_ref[hh]


def _paged_attn_finish(qc_ref, knc_ref, vnc_ref, o_ref, m_ref, l_ref, acc_ref):
    m = m_ref[...]
    s_new = _paged_scores(qc_ref, lambda hh: knc_ref[_head_rows(hh), :])
    mx = jnp.maximum(jnp.max(m, axis=1, keepdims=True), s_new)
    w = jnp.exp(m - mx)
    w_new = jnp.exp(s_new - mx)
    denom = jnp.sum(l_ref[...] * w, axis=1, keepdims=True) + w_new
    for hh in range(ATT_HEADS):
        num = (jnp.sum(acc_ref[hh] * w[hh:hh + 1, :], axis=1, keepdims=True)
               + w_new[hh:hh + 1, :] * vnc_ref[_head_rows(hh), :])
        o_ref[0, _head_rows(hh), :] = num / denom[hh:hh + 1, :]


def _sample_out_body(x_ref, att_ref, yt_ref, xs_ref, z_ref, gates_ref, sst_ref, scw_ref, dsk_ref, ng_ref,
                     wo_ref, gpost_ref, gpre_ref, gmpost_ref, wup_ref, wdn_ref, o_ref, snew_ref, *, ff_chunk):
    n = x_ref.shape[0]
    y = yt_ref[...].T[0:n, :] + dsk_ref[...] * xs_ref[...]
    yssd = _rms(y * _silu(z_ref[...]), ng_ref[...])
    gates = gates_ref[...]
    u = gates[:, CONV_WIDTH:2 * CONV_WIDTH] * gates[:, 2 * CONV_WIDTH:3 * CONV_WIDTH]
    uc = scw_ref[0:1, :] * sst_ref[0] + scw_ref[1:2, :] * sst_ref[1] + scw_ref[2:3, :] * u
    yconv = gates[:, 0:CONV_WIDTH] * uc
    snew_ref[0] = sst_ref[1]
    snew_ref[1] = u
    dot = functools.partial(jnp.dot, preferred_element_type=f32)
    mix = (dot(att_ref[...].astype(bf16), wo_ref[0:ATT_WIDTH, :])
           + dot(yssd.astype(bf16), wo_ref[ATT_WIDTH:ATT_WIDTH + SSD_WIDTH, :])
           + dot(yconv.astype(bf16), wo_ref[ATT_WIDTH + SSD_WIDTH:D_MODEL, :]))
    o_ref[...] = _mlp_tail(x_ref[...], mix, gpost_ref[...], gpre_ref[...], gmpost_ref[...],
                           wup_ref, wdn_ref, ff_chunk)


def _sample_out(x, att, yt, xs, z, gates, sst, scw, dsk, ng, wo, gpost, gpre, gmpost, wup, wdn):
    n = x.shape[0]
    return pl.pallas_call(
        functools.partial(_sample_out_body, ff_chunk=min(512, wup.shape[2])),
        out_shape=(jax.ShapeDtypeStruct((n, D_MODEL), f32),
                   jax.ShapeDtypeStruct((SCONV - 1, n, CONV_WIDTH), f32)),
        compiler_params=pltpu.CompilerParams(vmem_limit_bytes=VMEM_LIMIT_BYTES),
        name="sample_out_mlp",
    )(x, att, yt, xs, z, gates, sst, scw, dsk, ng, wo, gpost, gpre, gmpost, wup, wdn)


def _prep_w_in(wt):
    offs = [0]
    for s in IN_SIZES:
        offs.append(offs[-1] + s)
    q, k, v, f, z, xbc, dt, gb, gc, hc = [wt[offs[i]:offs[i + 1]] for i in range(len(IN_SIZES))]
    pad = jnp.zeros((LANES - ATT_HEADS - SSD_HEADS, wt.shape[1]), wt.dtype)
    return jnp.concatenate([q, k, v, z, xbc, gb, gc, hc, f, dt, pad], axis=0).astype(bf16)


def kernel(x_prompt, x_sample, cache_k, cache_v, cache_logf, state_ssm, state_ssd_conv, state_sconv, page_table,
           w_in, b_f, ssd_conv_w, ssd_conv_b, dt_bias, a_log, d_skip, ssd_norm_g, sconv_w, w_out,
           g_mix_pre, g_mix_post, g_mlp_pre, g_mlp_post, w_mlp_up, w_mlp_down):
    depth = w_in.shape[0]
    B, L, _ = x_prompt.shape
    n = x_sample.shape[0]
    assert x_sample.shape[1] == 1 and n <= LANES
    tm = _pick_tile(L, 512)
    tq = _pick_tile(L, 512)
    tk = _pick_tile(L, 512)

    kt_cache = jnp.transpose(cache_k, (0, 1, 3, 4, 2))
    vt_cache = jnp.transpose(cache_v, (0, 1, 3, 4, 2))
    page_terms = _page_bias(jnp.transpose(cache_logf, (0, 3, 1, 2)))

    w_in_t = jnp.transpose(w_in, (0, 2, 1))
    wo_all = w_out.astype(bf16)
    nsub = page_table.shape[1] // min(PAGES_PER_STEP, page_table.shape[1])
    fs = D_FF // nsub
    wup_all = jnp.transpose(w_mlp_up.reshape(depth, D_MODEL, nsub, fs), (0, 2, 1, 3)).astype(bf16)
    wdn_all = w_mlp_down.reshape(depth, nsub, fs, D_MODEL).astype(bf16)

    xp = x_prompt
    xs = x_sample.reshape(n, D_MODEL)
    kv_all = None
    outs_p = [[] for _ in range(6)]
    outs_s = [[] for _ in range(6)]
    row = lambda a: a.reshape(1, -1)
    lanes_bcast = lambda a: jnp.broadcast_to(a[:, :, None], a.shape + (LANES,))

    for l in range(depth):
        w = _prep_w_in(w_in_t[l])
        wo = wo_all[l]
        wup = wup_all[l]
        wdn = wdn_all[l]
        bias = jnp.concatenate([b_f[l], dt_bias[l], jnp.zeros((LANES - ATT_HEADS - SSD_HEADS,), f32)]).reshape(1, LANES)
        a_neg = -jnp.exp(a_log[l])
        a_row = jnp.concatenate([jnp.zeros((COL_DT,), f32), a_neg,
                                 jnp.zeros((LANES - COL_DT - SSD_HEADS,), f32)]).reshape(1, LANES)
        a_exp = jnp.repeat(a_neg, SSD_HEAD_DIM).reshape(1, SSD_WIDTH)
        dsk = jnp.repeat(d_skip[l], SSD_HEAD_DIM).reshape(1, SSD_WIDTH)
        ng = row(ssd_norm_g[l])
        cw, cb, scw = ssd_conv_w[l], row(ssd_conv_b[l]), sconv_w[l]
        gpre, gpost, gmpre, gmpost = row(g_mix_pre[l]), row(g_mix_post[l]), row(g_mlp_pre[l]), row(g_mlp_post[l])

        (qtp, kp, vtp, kt_all, vt_all, z, xbc, yconv, lfdt, lft, utail) = _prompt_in(
            xp, gpre, w, bias, scw, kv_all, l, depth, tm=tm, tq=tq, tk=tk)
        kv_all = (kt_all, vt_all)
        att = _prompt_attn(qtp, kp, vtp, tq=tq, tk=tk)
        yssd, s_fin = _prompt_ssd(xbc, lfdt, z, cw, cb, a_row, dsk, ng, tm=tm)

        cst = jnp.transpose(state_ssd_conv[l], (1, 0, 2))
        sst = jnp.transpose(state_sconv[l], (1, 0, 2))
        (q, k, v, lf, z_s, gates, xs_s, bm, cm, dtxt, dect, cnew, qt, knt, vnt) = _sample_in(
            xs, gpre, w, bias, cst, cw, cb, a_exp)
        s_new, yt = _sample_ssd(state_ssm[l], dtxt, dect, bm, cm)
        lf6 = lf[:, COL_LOGF:COL_LOGF + ATT_HEADS]
        lfn = lanes_bcast(jnp.pad(lf6, ((0, 0), (0, SUBLANES - ATT_HEADS))))

        xp, att_s = _out_mlp_attn(
            xp.reshape(B * L, D_MODEL), att, yssd.reshape(B * L, SSD_WIDTH),
            yconv.reshape(B * L, CONV_WIDTH), wo, gpost, gmpre, gmpost, wup, wdn,
            page_table, qt, knt, vnt, lfn, kt_cache, vt_cache, page_terms, l,
            tm=tm)
        xp = xp.reshape(B, L, D_MODEL)
        xs, snew = _sample_out(xs, att_s[:, :, 0], yt, xs_s, z_s, gates, sst, scw, dsk, ng,
                               wo, gpost, gmpre, gmpost, wup, wdn)

        outs_p[2].append(lft[:, 0:ATT_HEADS, :])
        outs_p[3].append(s_fin)
        outs_p[4].append(xbc[:, L - (SSD_CONV - 1):, :])
        outs_p[5].append(utail[:, SUBLANES - (SCONV - 1):, :])
        outs_s[0].append(k.reshape(n, 1, ATT_HEADS, HEAD_DIM))
        outs_s[1].append(v.reshape(n, 1, ATT_HEADS, HEAD_DIM))
        outs_s[2].append(lf6.reshape(n, 1, ATT_HEADS))
        outs_s[3].append(s_new)
        outs_s[4].append(jnp.transpose(cnew, (1, 0, 2)))
        outs_s[5].append(jnp.transpose(snew, (1, 0, 2)))

    heads_last = lambda a: jnp.transpose(a.reshape(depth, B, ATT_HEADS, HEAD_DIM, L), (0, 1, 4, 2, 3))
    k_p = heads_last(kv_all[0])
    v_p = heads_last(kv_all[1])
    lf_p = jnp.transpose(jnp.stack(outs_p[2]), (0, 1, 3, 2))
    rest_p = [jnp.stack(a) for a in outs_p[3:]]
    stacked_s = [jnp.stack(a) for a in outs_s]
    return (xp, xs.reshape(n, 1, D_MODEL), k_p, v_p, lf_p, *rest_p, *stacked_s)
```
